```python
import jax
import jax.numpy as jnp
from jax import lax
import numpy as np

D_MODEL = 2048
BATCH = 2
SEQ = 4096
DEPTH = 4

GRID_W = 64
CTX_LEN = 256
N_MIXERS = 3
N_MOD = 6
EPS = 1e-6
NEG_INF = -1e30
D_FF = 5632
FFN_CONV = 3
CHUNK = 128
A_WIDTH = 2 * D_MODEL
A_GROUPS = 16
NA_HEADS = 16
NA_HEAD_DIM = D_MODEL // NA_HEADS
NA_KH_MAX = 8
NA_KW = 16
RNN_WIDTH = D_MODEL
RNN_HEADS = 16
RNN_HEAD_DIM = RNN_WIDTH // RNN_HEADS
RNN_CONV = 4
RG_C = 8.0
N_A_LAYERS = (DEPTH + N_MIXERS - 1) // N_MIXERS
N_B_LAYERS = (DEPTH + N_MIXERS - 2) // N_MIXERS
N_C_LAYERS = (DEPTH + N_MIXERS - 3) // N_MIXERS

kernel_name = 'hybrid_interleaved_diffusion_block'


def rmsnorm(x, g):
    xf = x.astype(jnp.float32)
    y = xf * lax.rsqrt(jnp.mean(xf * xf, axis=-1, keepdims=True) + EPS)
    return (y * g.astype(jnp.float32)).astype(x.dtype)


def modulate(h, shift, scale):
    return h * (1 + scale) + shift


def dwconv(x, w, b, left):
    k, length = w.shape[0], x.shape[1]
    xp = jnp.pad(x, ((0, 0), (left, k - 1 - left), (0, 0)))
    y = b
    for j in range(k):
        y = y + xp[:, j:j + length] * w[j]
    return y


def conv_ffn(h, w_up, conv_w, conv_b, w_down):
    z = dwconv(h @ w_up, conv_w, conv_b, FFN_CONV // 2)
    g, v = jnp.split(z, 2, axis=-1)
    return (jax.nn.silu(g) * v) @ w_down


def chunk_mlp_mix(h, w_in, g_v, w_s, b_s, w_out):
    bsz, length, _ = h.shape
    z = jax.nn.gelu(h @ w_in)
    u, v = jnp.split(z, 2, axis=-1)
    v = rmsnorm(v, g_v).reshape(bsz, length // CHUNK, CHUNK, A_GROUPS, A_WIDTH // A_GROUPS)
    s = jnp.einsum('gpq,bnqgc->bnpgc', w_s, v) + b_s.T[None, None, :, :, None]
    return (u * s.reshape(bsz, length, A_WIDTH)) @ w_out


def na_mix(hc, hl, w_qkv, rpb, w_out, need_ctx):
    bsz, seq, _ = hl.shape
    rows = seq // GRID_W
    kh = min(NA_KH_MAX, rows)
    scale = NA_HEAD_DIM ** -0.5
    qkv = (hl @ w_qkv).reshape(bsz, rows, GRID_W, 3, NA_HEADS, NA_HEAD_DIM)
    q, k, v = qkv[:, :, :, 0], qkv[:, :, :, 1], qkv[:, :, :, 2]
    if need_ctx:
        qkv_c = (hc @ w_qkv).reshape(bsz, hc.shape[1], 3, NA_HEADS, NA_HEAD_DIM)
        q_c, k_c, v_c = qkv_c[:, :, 0], qkv_c[:, :, 1], qkv_c[:, :, 2]
    else:
        kv_c = (hc @ w_qkv[:, D_MODEL:]).reshape(bsz, hc.shape[1], 2, NA_HEADS, NA_HEAD_DIM)
        k_c, v_c = kv_c[:, :, 0], kv_c[:, :, 1]
    r = jnp.arange(rows)
    r0 = jnp.clip(r - kh // 2, 0, rows - kh)
    key_rows = r0[:, None] + jnp.arange(kh)[None, :]
    col = jnp.arange(GRID_W)
    c0 = jnp.clip(col - NA_KW // 2, 0, GRID_W - NA_KW)
    col_ok = (col[None, :] >= c0[:, None]) & (col[None, :] < c0[:, None] + NA_KW)
    k_blk = k[:, key_rows]
    v_blk = v[:, key_rows]
    s_lat = jnp.einsum('brqhd,brikhd->bhrqik', q, k_blk).astype(jnp.float32) * scale
    dr = key_rows - r[:, None]
    dc = jnp.clip(col[None, :] - col[:, None], -(NA_KW - 1), NA_KW - 1)
    bias = rpb[:, dr[:, None, :, None] + NA_KH_MAX - 1, dc[None, :, None, :] + NA_KW - 1]
    s_lat = jnp.where(col_ok[None, None, None, :, None, :], s_lat + bias[None].astype(jnp.float32), NEG_INF)
    s_ctx = jnp.einsum('brqhd,bkhd->bhrqk', q, k_c).astype(jnp.float32) * scale
    n_win = kh * GRID_W
    logits = jnp.concatenate([s_lat.reshape(bsz, NA_HEADS, rows, GRID_W, n_win), s_ctx], axis=-1)
    p = jax.nn.softmax(logits, axis=-1).astype(v.dtype)
    p_lat = p[..., :n_win].reshape(bsz, NA_HEADS, rows, GRID_W, kh, GRID_W)
    o = jnp.einsum('bhrqik,brikhd->brqhd', p_lat, v_blk) + jnp.einsum('bhrqk,bkhd->brqhd', p[..., n_win:], v_c)
    y_lat = o.reshape(bsz, seq, D_MODEL) @ w_out
    y_ctx = None
    if need_ctx:
        sc = jnp.einsum('bqhd,bkhd->bhqk', q_c, k_c).astype(jnp.float32) * scale
        pc = jax.nn.softmax(sc, axis=-1).astype(v_c.dtype)
        y_ctx = jnp.einsum('bhqk,bkhd->bqhd', pc, v_c).reshape(bsz, hc.shape[1], D_MODEL) @ w_out
    return y_ctx, y_lat


def rglru_gates(xr, w_g, b_g, lam):
    bsz, length, _ = xr.shape
    xh = xr.reshape(bsz, length, RNN_HEADS, RNN_HEAD_DIM)
    g = jnp.einsum('blhi,ghij->gblhj', xh, w_g) + b_g[:, None, None]
    g = jax.nn.sigmoid(g.astype(jnp.float32)).reshape(2, bsz, length, RNN_WIDTH)
    log_a = -RG_C * g[0] * jax.nn.softplus(-lam.astype(jnp.float32))
    a = jnp.exp(log_a)
    b = jnp.sqrt(-jnp.expm1(2.0 * log_a)) * (g[1] * xr.astype(jnp.float32))
    return a, b


def linear_scan(a, b, h0, reverse):
    idx = -1 if reverse else 0
    b = b.at[:, idx].add(a[:, idx] * h0)

    def combine(e1, e2):
        a1, b1 = e1
        a2, b2 = e2
        return a1 * a2, a2 * b1 + b2

    _, h = lax.associative_scan(combine, (a, b), reverse=reverse, axis=1)
    return h


def rglru_mix(hc, hl, w_in, conv_w, conv_b, w_gate, b_gate, lam, w_out, need_ctx):
    left = RNN_CONV // 2
    if need_ctx:
        y_c, x_c = jnp.split(hc @ w_in, 2, axis=-1)
    else:
        x_c = hc @ w_in[:, RNN_WIDTH:]
    x_c = dwconv(x_c, conv_w, conv_b, left)
    y_l, x_l = jnp.split(hl @ w_in, 2, axis=-1)
    x_l = dwconv(x_l, conv_w, conv_b, left)
    h0 = jnp.zeros((hc.shape[0], RNN_WIDTH), jnp.float32)
    h_c_dirs, h_l_dirs = [], []
    for d, rev in enumerate((False, True)):
        a_c, b_c = rglru_gates(x_c, w_gate[d], b_gate[d], lam[d])
        h_c = linear_scan(a_c, b_c, h0, rev)
        h_end = h_c[:, 0] if rev else h_c[:, -1]
        a_l, b_l = rglru_gates(x_l, w_gate[d], b_gate[d], lam[d])
        h_l_dirs.append(linear_scan(a_l, b_l, h_end, rev))
        h_c_dirs.append(h_c)
    h_l = (h_l_dirs[0] + h_l_dirs[1]).astype(hl.dtype)
    y_lat = (jax.nn.gelu(y_l) * h_l) @ w_out
    y_ctx = None
    if need_ctx:
        h_cs = (h_c_dirs[0] + h_c_dirs[1]).astype(hc.dtype)
        y_ctx = (jax.nn.gelu(y_c) * h_cs) @ w_out
    return y_ctx, y_lat


def setup_inputs(seed: int = 0) -> dict:
    key = jax.random.key(seed)
    ks = iter(jax.random.split(key, 32))

    def nrm(shape, scale):
        return jax.random.normal(next(ks), shape, jnp.float32) * scale

    D = D_MODEL
    nA, nB, nC = N_A_LAYERS, N_B_LAYERS, N_C_LAYERS
    a0 = jax.random.uniform(next(ks), (nC, 2, RNN_WIDTH), jnp.float32, 0.9, 0.999)
    s = a0 ** (1.0 / RG_C)
    c_lam = jnp.log(s) - jnp.log1p(-s)
    return {
        'x': nrm((BATCH, SEQ, D), 1.0),
        'c': nrm((BATCH, D), 1.0),
        'ctx': nrm((BATCH, CTX_LEN, D), 1.0),
        'c_ctx': nrm((D,), 1.0),
        'ada_w': nrm((DEPTH, D, N_MOD * D), D ** -0.5),
        'ada_b': nrm((DEPTH, N_MOD * D), 0.02),
        'norm_g': 1.0 + nrm((DEPTH, 2, D), 0.05),
        'ffn_w_up': nrm((DEPTH, D, 2 * D_FF), D ** -0.5),
        'ffn_conv_w': nrm((DEPTH, FFN_CONV, 2 * D_FF), FFN_CONV ** -0.5),
        'ffn_conv_b': nrm((DEPTH, 2 * D_FF), 0.02),
        'ffn_w_down': nrm((DEPTH, D_FF, D), D_FF ** -0.5),
        'a_w_in': nrm((nA, D, 2 * A_WIDTH), D ** -0.5),
        'a_g_v': 1.0 + nrm((nA, A_WIDTH), 0.05),
        'a_w_s': nrm((nA, A_GROUPS, CHUNK, CHUNK), CHUNK ** -0.5),
        'a_b_s': nrm((nA, A_GROUPS, CHUNK), 0.1),
        'a_w_out': nrm((nA, A_WIDTH, D), A_WIDTH ** -0.5),
        'b_w_qkv': nrm((nB, D, 3 * D), D ** -0.5),
        'b_rpb': nrm((nB, NA_HEADS, 2 * NA_KH_MAX - 1, 2 * NA_KW - 1), 0.2),
        'b_w_out': nrm((nB, D, D), D ** -0.5),
        'c_w_in': nrm((nC, D, 2 * RNN_WIDTH), D ** -0.5),
        'c_conv_w': nrm((nC, RNN_CONV, RNN_WIDTH), RNN_CONV ** -0.5),
        'c_conv_b': nrm((nC, RNN_WIDTH), 0.02),
        'c_w_gate': nrm((nC, 2, 2, RNN_HEADS, RNN_HEAD_DIM, RNN_HEAD_DIM), RNN_HEAD_DIM ** -0.5),
        'c_b_gate': nrm((nC, 2, 2, RNN_HEADS, RNN_HEAD_DIM), 0.1),
        'c_lam': c_lam,
        'c_w_out': nrm((nC, RNN_WIDTH, D), RNN_WIDTH ** -0.5),
        'final_g': 1.0 + nrm((D,), 0.05),
    }


def reference(x, c, ctx, c_ctx, ada_w, ada_b, norm_g, ffn_w_up, ffn_conv_w, ffn_conv_b, ffn_w_down,
              a_w_in, a_g_v, a_w_s, a_b_s, a_w_out, b_w_qkv, b_rpb, b_w_out,
              c_w_in, c_conv_w, c_conv_b, c_w_gate, c_b_gate, c_lam, c_w_out, final_g):
    bsz = x.shape[0]
    silu_c = jax.nn.silu(c)
    silu_cc = jax.nn.silu(c_ctx)
    h_lat, h_ctx = x, ctx
    for i in range(DEPTH):
        last = i == DEPTH - 1
        kind, j = i % N_MIXERS, i // N_MIXERS
        ctx_in = (not last) or kind != 0
        m_l = (silu_c @ ada_w[i] + ada_b[i]).reshape(bsz, 1, N_MOD, D_MODEL)
        n_l = modulate(rmsnorm(h_lat, norm_g[i, 0]), m_l[:, :, 0], m_l[:, :, 1])
        if ctx_in:
            m_c = (silu_cc @ ada_w[i] + ada_b[i]).reshape(N_MOD, D_MODEL)
            n_c = modulate(rmsnorm(h_ctx, norm_g[i, 0]), m_c[0], m_c[1])
        if kind == 0:
            y_l = chunk_mlp_mix(n_l, a_w_in[j], a_g_v[j], a_w_s[j], a_b_s[j], a_w_out[j])
            y_c = None if last else chunk_mlp_mix(n_c, a_w_in[j], a_g_v[j], a_w_s[j], a_b_s[j], a_w_out[j])
        elif kind == 1:
            y_c, y_l = na_mix(n_c, n_l, b_w_qkv[j], b_rpb[j], b_w_out[j], not last)
        else:
            y_c, y_l = rglru_mix(n_c, n_l, c_w_in[j], c_conv_w[j], c_conv_b[j], c_w_gate[j], c_b_gate[j],
                                 c_lam[j], c_w_out[j], not last)
        h_lat = h_lat + m_l[:, :, 2] * y_l
        n_l = modulate(rmsnorm(h_lat, norm_g[i, 1]), m_l[:, :, 3], m_l[:, :, 4])
        h_lat = h_lat + m_l[:, :, 5] * conv_ffn(n_l, ffn_w_up[i], ffn_conv_w[i], ffn_conv_b[i], ffn_w_down[i])
        if not last:
            h_ctx = h_ctx + m_c[2] * y_c
            n_c = modulate(rmsnorm(h_ctx, norm_g[i, 1]), m_c[3], m_c[4])
            h_ctx = h_ctx + m_c[5] * conv_ffn(n_c, ffn_w_up[i], ffn_conv_w[i], ffn_conv_b[i], ffn_w_down[i])
    return rmsnorm(h_lat, final_g)
```

```python
import functools

import jax
import jax.numpy as jnp
from jax import lax
from jax.experimental import pallas as pl
from jax.experimental.pallas import tpu as pltpu

F32 = jnp.float32
BF16 = jnp.bfloat16

D_MODEL = 2048
BATCH = 2
SEQ = 4096
DEPTH = 4
GRID_W = 64
CTX_LEN = 256
N_MIXERS = 3
N_MOD = 6
EPS = 1e-6
NEG_INF = -1e30
D_FF = 5632
CHUNK = 128
A_WIDTH = 2 * D_MODEL
A_GROUPS = 16
NA_HEADS = 16
NA_HEAD_DIM = D_MODEL // NA_HEADS
NA_KH = 8
NA_KW = 16
RNN_WIDTH = D_MODEL
RNN_HEADS = 16
RNN_HEAD_DIM = RNN_WIDTH // RNN_HEADS
RNN_CONV = 4
RG_C = 8.0

M_LAT = BATCH * SEQ
M_CTX = BATCH * CTX_LEN
M_ALL = M_LAT + M_CTX
N_GROUPS = BATCH + 1
GRID_ROWS = SEQ // GRID_W

VMEM_LIMIT_BYTES = 56 * 1024 * 1024
SUBLANES = 8
BF16_ROWS = 16

ROW_TILE = 512


def _params(n_axes):
    return pltpu.CompilerParams(
        dimension_semantics=("arbitrary",) * n_axes, vmem_limit_bytes=VMEM_LIMIT_BYTES)


def _group_of_tile(i, tm):
    return jnp.minimum((i * tm) // SEQ, BATCH)


def _mod_spec(layer, which, tm, tn=D_MODEL, grid_rank=1):
    if grid_rank == 1:
        return pl.BlockSpec(
            (None, 1, tn), lambda i: ((layer * N_GROUPS + _group_of_tile(i, tm)) * N_MOD + which, 0, 0))
    return pl.BlockSpec(
        (None, 1, tn), lambda j, i: ((layer * N_GROUPS + _group_of_tile(i, tm)) * N_MOD + which, 0, j))


def _ada_kernel(c_ref, w_ref, b_ref, o_ref):
    c = c_ref[...]
    s = (c * jax.nn.sigmoid(c)).astype(BF16)
    o_ref[...] = jnp.dot(s, w_ref[...].astype(BF16), preferred_element_type=F32) + b_ref[...]


def _ada_mod(cvec, ada_w, ada_b):
    depth, d, n = ada_w.shape
    tn = 1024
    out = pl.pallas_call(
        _ada_kernel,
        grid=(depth, n // tn),
        in_specs=[
            pl.BlockSpec((SUBLANES, d), lambda l, j: (0, 0)),
            pl.BlockSpec((None, d, tn), lambda l, j: (l, 0, j)),
            pl.BlockSpec((None, 1, tn), lambda l, j: (l, 0, j)),
        ],
        out_specs=pl.BlockSpec((None, SUBLANES, tn), lambda l, j: (l, 0, j)),
        out_shape=jax.ShapeDtypeStruct((depth, SUBLANES, n), F32),
        compiler_params=_params(2),
        name="ada_mod",
    )(cvec, ada_w, ada_b.reshape(depth, 1, n))
    return out[:, :N_GROUPS].reshape(depth * N_GROUPS * N_MOD, 1, d)


def _norm_mod_kernel(h_ref, g_ref, shift_ref, scale_ref, o_ref):
    x = h_ref[...]
    y = x * lax.rsqrt(jnp.mean(x * x, axis=-1, keepdims=True) + EPS)
    y = y * g_ref[...]
    o_ref[...] = (y * (1 + scale_ref[...]) + shift_ref[...]).astype(o_ref.dtype)


def _norm_mod(h, norm_g, mod, layer, which_norm, n_rows):
    tm = ROW_TILE
    g = norm_g.reshape(DEPTH * 2, 1, D_MODEL)
    return pl.pallas_call(
        _norm_mod_kernel,
        grid=(n_rows // tm,),
        in_specs=[
            pl.BlockSpec((tm, D_MODEL), lambda i: (i, 0)),
            pl.BlockSpec((None, 1, D_MODEL), lambda i: (layer * 2 + which_norm, 0, 0)),
            _mod_spec(layer, 3 * which_norm, tm),
            _mod_spec(layer, 3 * which_norm + 1, tm),
        ],
        out_specs=pl.BlockSpec((tm, D_MODEL), lambda i: (i, 0)),
        out_shape=jax.ShapeDtypeStruct((M_ALL, D_MODEL), BF16),
        compiler_params=_params(1),
        name="norm_mod",
    )(h, g, mod, mod)


def _final_norm_kernel(h_ref, g_ref, o_ref):
    x = h_ref[...]
    y = x * lax.rsqrt(jnp.mean(x * x, axis=-1, keepdims=True) + EPS)
    o_ref[...] = y * g_ref[...]


def _final_norm(h, final_g):
    tm = ROW_TILE
    return pl.pallas_call(
        _final_norm_kernel,
        grid=(M_LAT // tm,),
        in_specs=[
            pl.BlockSpec((tm, D_MODEL), lambda i: (i, 0)),
            pl.BlockSpec((1, D_MODEL), lambda i: (0, 0)),
        ],
        out_specs=pl.BlockSpec((tm, D_MODEL), lambda i: (i, 0)),
        out_shape=jax.ShapeDtypeStruct((M_LAT, D_MODEL), F32),
        compiler_params=_params(1),
        name="final_norm",
    )(h, final_g.reshape(1, D_MODEL))


def _mm_kernel(x_ref, w_ref, o_ref, wbf_ref, *, act):
    @pl.when(pl.program_id(1) == 0)
    def _():
        wbf_ref[...] = w_ref[...].astype(BF16)

    acc = jnp.dot(x_ref[...], wbf_ref[...], preferred_element_type=F32)
    if act == "gelu":
        acc = jax.nn.gelu(acc)
    o_ref[...] = acc.astype(o_ref.dtype)


def _mm_resid_kernel(x_ref, w_ref, gate_ref, res_ref, o_ref, wbf_ref):
    @pl.when(pl.program_id(1) == 0)
    def _():
        wbf_ref[...] = w_ref[...].astype(BF16)

    acc = jnp.dot(x_ref[...], wbf_ref[...], preferred_element_type=F32)
    o_ref[...] = res_ref[...] + gate_ref[...] * acc


def _matmul(x, w, *, n_rows, tn, out_dtype, act=None):
    tm = ROW_TILE
    k, n = w.shape
    return pl.pallas_call(
        functools.partial(_mm_kernel, act=act),
        grid=(n // tn, n_rows // tm),
        in_specs=[
            pl.BlockSpec((tm, k), lambda j, i: (i, 0)),
            pl.BlockSpec((k, tn), lambda j, i: (0, j)),
        ],
        out_specs=pl.BlockSpec((tm, tn), lambda j, i: (i, j)),
        out_shape=jax.ShapeDtypeStruct((x.shape[0], n), out_dtype),
        scratch_shapes=[pltpu.VMEM((k, tn), BF16)],
        compiler_params=_params(2),
        name="matmul",
    )(x, w)


def _matmul_resid(x, w, h, mod, layer, which_gate, *, n_rows, tn):
    tm = ROW_TILE
    k, n = w.shape
    return pl.pallas_call(
        _mm_resid_kernel,
        grid=(n // tn, n_rows // tm),
        in_specs=[
            pl.BlockSpec((tm, k), lambda j, i: (i, 0)),
            pl.BlockSpec((k, tn), lambda j, i: (0, j)),
            _mod_spec(layer, which_gate, tm, tn, grid_rank=2),
            pl.BlockSpec((tm, tn), lambda j, i: (i, j)),
        ],
        out_specs=pl.BlockSpec((tm, tn), lambda j, i: (i, j)),
        out_shape=jax.ShapeDtypeStruct(h.shape, F32),
        scratch_shapes=[pltpu.VMEM((k, tn), BF16)],
        input_output_aliases={3: 0},
        compiler_params=_params(2),
        name="matmul_resid",
    )(x, w, mod, h)


HALO = BF16_ROWS


def _seq_edge_masks(row0, tm):
    r = row0 + lax.broadcasted_iota(jnp.int32, (tm, 1), 0)
    is_lat = r < M_LAT
    first = jnp.where(is_lat, r & (SEQ - 1), r & (CTX_LEN - 1)) == 0
    last = jnp.where(is_lat, (r + 1) & (SEQ - 1), (r + 1) & (CTX_LEN - 1)) == 0
    return first, last


def _ffn_up_kernel(x_ref, xp_ref, xn_ref, wg_ref, wv_ref, cwg_ref, cwv_ref, cbg_ref, cbv_ref,
                   o_ref, wg_bf, wv_bf, xs_ref):
    i = pl.program_id(1)
    tm = x_ref.shape[0]

    @pl.when(i == 0)
    def _():
        wg_bf[...] = wg_ref[...].astype(BF16)
        wv_bf[...] = wv_ref[...].astype(BF16)

    xs_ref[0:HALO, :] = xp_ref[...]
    xs_ref[HALO:HALO + tm, :] = x_ref[...]
    xs_ref[HALO + tm:, :] = xn_ref[...]
    xs = xs_ref[...]
    first, last = _seq_edge_masks(i * tm, tm)
    n_ext = tm + 2 * HALO

    def conv(z, cw_ref, cb_ref):
        prev = pltpu.roll(z, 1, 0)[HALO:HALO + tm]
        nxt = pltpu.roll(z, n_ext - 1, 0)[HALO:HALO + tm]
        cur = z[HALO:HALO + tm]
        cw = cw_ref[...]
        y = cb_ref[...] + jnp.where(first, 0.0, prev) * cw[0:1]
        y = y + cur * cw[1:2]
        return y + jnp.where(last, 0.0, nxt) * cw[2:3]

    g = conv(jnp.dot(xs, wg_bf[...], preferred_element_type=F32), cwg_ref, cbg_ref)
    v = conv(jnp.dot(xs, wv_bf[...], preferred_element_type=F32), cwv_ref, cbv_ref)
    o_ref[...] = (g * jax.nn.sigmoid(g) * v).astype(o_ref.dtype)


def _ffn_up(n, w_up, conv_w, conv_b, *, n_rows):
    tm, tn = ROW_TILE, 512
    k = w_up.shape[0]
    nj = D_FF // tn
    halo_per_tile = tm // HALO
    last_halo = M_ALL // HALO - 1
    cb = conv_b.reshape(1, 2 * D_FF)
    return pl.pallas_call(
        _ffn_up_kernel,
        grid=(nj, n_rows // tm),
        in_specs=[
            pl.BlockSpec((tm, k), lambda j, i: (i, 0)),
            pl.BlockSpec((HALO, k), lambda j, i: (jnp.maximum(i * halo_per_tile - 1, 0), 0)),
            pl.BlockSpec((HALO, k), lambda j, i: (jnp.minimum((i + 1) * halo_per_tile, last_halo), 0)),
            pl.BlockSpec((k, tn), lambda j, i: (0, j)),
            pl.BlockSpec((k, tn), lambda j, i: (0, j + nj)),
            pl.BlockSpec((3, tn), lambda j, i: (0, j)),
            pl.BlockSpec((3, tn), lambda j, i: (0, j + nj)),
            pl.BlockSpec((1, tn), lambda j, i: (0, j)),
            pl.BlockSpec((1, tn), lambda j, i: (0, j + nj)),
        ],
        out_specs=pl.BlockSpec((tm, tn), lambda j, i: (i, j)),
        out_shape=jax.ShapeDtypeStruct((M_ALL, D_FF), BF16),
        scratch_shapes=[
            pltpu.VMEM((k, tn), BF16),
            pltpu.VMEM((k, tn), BF16),
            pltpu.VMEM((tm + 2 * HALO, k), BF16),
        ],
        compiler_params=_params(2),
        name="ffn_up_conv_gate",
    )(n, n, n, w_up, w_up, conv_w, conv_w, cb, cb)


def _sgu_kernel(u_ref, v_ref, gv_ref, ws_ref, bs_ref, o_ref):
    tm = u_ref.shape[0]
    gw = A_WIDTH // A_GROUPS
    gv = gv_ref[...]
    for c in range(tm // CHUNK):
        rows = slice(c * CHUNK, (c + 1) * CHUNK)
        v = v_ref[rows, :].astype(F32)
        vn = v * lax.rsqrt(jnp.mean(v * v, axis=-1, keepdims=True) + EPS) * gv
        vn = vn.astype(BF16)
        for g in range(A_GROUPS):
            cols = slice(g * gw, (g + 1) * gw)
            s = jnp.dot(ws_ref[g].astype(BF16), vn[:, cols], preferred_element_type=F32)
            s = s + bs_ref[:, g:g + 1]
            o_ref[rows, cols] = (u_ref[rows, cols].astype(F32) * s).astype(o_ref.dtype)


def _sgu(z, g_v, w_s, b_s, *, n_rows):
    tm = 2 * CHUNK
    return pl.pallas_call(
        _sgu_kernel,
        grid=(n_rows // tm,),
        in_specs=[
            pl.BlockSpec((tm, A_WIDTH), lambda i: (i, 0)),
            pl.BlockSpec((tm, A_WIDTH), lambda i: (i, 1)),
            pl.BlockSpec((1, A_WIDTH), lambda i: (0, 0)),
            pl.BlockSpec((A_GROUPS, CHUNK, CHUNK), lambda i: (0, 0, 0)),
            pl.BlockSpec((CHUNK, A_GROUPS), lambda i: (0, 0)),
        ],
        out_specs=pl.BlockSpec((tm, A_WIDTH), lambda i: (i, 0)),
        out_shape=jax.ShapeDtypeStruct((M_ALL, A_WIDTH), BF16),
        compiler_params=_params(1),
        name="sgu",
    )(z, z, g_v.reshape(1, A_WIDTH), w_s, b_s.T)


NA_WIN = NA_KH * GRID_W
NA_VARIANTS = NA_KH


def _na_bias_table(rpb):
    col = jnp.arange(GRID_W)
    dc = jnp.clip(col[None, :] - col[:, None], -(NA_KW - 1), NA_KW - 1) + NA_KW - 1
    c0 = jnp.clip(col - NA_KW // 2, 0, GRID_W - NA_KW)
    ok = (col[None, :] >= c0[:, None]) & (col[None, :] < c0[:, None] + NA_KW)
    dri = jnp.arange(NA_VARIANTS)[:, None] + jnp.arange(NA_KH)[None, :]
    t = rpb[:, dri[:, :, None, None], dc[None, None, :, :]]
    t = jnp.where(ok[None, None, None], t.astype(F32), NEG_INF)
    return t.transpose(0, 1, 3, 2, 4).reshape(NA_HEADS, NA_VARIANTS, GRID_W, NA_WIN)


def _na_lat_kernel(q_ref, k_ref, v_ref, kc_ref, vc_ref, bias_ref, o_ref):
    scale = NA_HEAD_DIM ** -0.5
    kc = kc_ref[...]
    vc = vc_ref[...]
    nt = (((1,), (1,)), ((), ()))

    def row(r, carry):
        r0 = jnp.clip(r - NA_KH // 2, 0, GRID_ROWS - NA_KH)
        q = q_ref[pl.ds(pl.multiple_of(r * GRID_W, GRID_W), GRID_W), :]
        win = pl.ds(pl.multiple_of(r0 * GRID_W, GRID_W), NA_WIN)
        kw = k_ref[win, :]
        vw = v_ref[win, :]
        bias = bias_ref[r0 - r + NA_KH - 1]
        s = lax.dot_general(q, kw, nt, preferred_element_type=F32) * scale
        s = jnp.where(bias > 0.5 * NEG_INF, s + bias, NEG_INF)
        sc = lax.dot_general(q, kc, nt, preferred_element_type=F32) * scale
        m = jnp.maximum(jnp.max(s, axis=-1, keepdims=True), jnp.max(sc, axis=-1, keepdims=True))
        p = jnp.exp(s - m)
        pc = jnp.exp(sc - m)
        denom = jnp.sum(p, axis=-1, keepdims=True) + jnp.sum(pc, axis=-1, keepdims=True)
        o = jnp.dot(p.astype(BF16), vw, preferred_element_type=F32)
        o = o + jnp.dot(pc.astype(BF16), vc, preferred_element_type=F32)
        o_ref[pl.ds(pl.multiple_of(r * GRID_W, GRID_W), GRID_W), :] = (o / denom).astype(o_ref.dtype)
        return carry

    lax.fori_loop(0, GRID_ROWS, row, 0)


def _na_ctx_kernel(q_ref, k_ref, v_ref, o_ref):
    scale = NA_HEAD_DIM ** -0.5
    s = lax.dot_general(q_ref[...], k_ref[...], (((1,), (1,)), ((), ())),
                        preferred_element_type=F32) * scale
    m = jnp.max(s, axis=-1, keepdims=True)
    p = jnp.exp(s - m)
    denom = jnp.sum(p, axis=-1, keepdims=True)
    o = jnp.dot(p.astype(BF16), v_ref[...], preferred_element_type=F32)
    o_ref[...] = (o / denom).astype(o_ref.dtype)


def _na_attention(qkv, rpb):
    dh, nh = NA_HEAD_DIM, NA_HEADS
    ctx_blk0 = M_LAT // CTX_LEN
    bias = _na_bias_table(rpb)
    lat = pl.pallas_call(
        _na_lat_kernel,
        grid=(BATCH, nh),
        in_specs=[
            pl.BlockSpec((SEQ, dh), lambda b, h: (b, h)),
            pl.BlockSpec((SEQ, dh), lambda b, h: (b, nh + h)),
            pl.BlockSpec((SEQ, dh), lambda b, h: (b, 2 * nh + h)),
            pl.BlockSpec((CTX_LEN, dh), lambda b, h: (ctx_blk0 + b, nh + h)),
            pl.BlockSpec((CTX_LEN, dh), lambda b, h: (ctx_blk0 + b, 2 * nh + h)),
            pl.BlockSpec((None, NA_VARIANTS, GRID_W, NA_WIN), lambda b, h: (h, 0, 0, 0)),
        ],
        out_specs=pl.BlockSpec((SEQ, dh), lambda b, h: (b, h)),
        out_shape=jax.ShapeDtypeStruct((M_LAT, D_MODEL), BF16),
        compiler_params=_params(2),
        name="na_latent",
    )(qkv, qkv, qkv, qkv, qkv, bias)
    ctx = pl.pallas_call(
        _na_ctx_kernel,
        grid=(BATCH, nh),
        in_specs=[
            pl.BlockSpec((CTX_LEN, dh), lambda b, h: (ctx_blk0 + b, h)),
            pl.BlockSpec((CTX_LEN, dh), lambda b, h: (ctx_blk0 + b, nh + h)),
            pl.BlockSpec((CTX_LEN, dh), lambda b, h: (ctx_blk0 + b, 2 * nh + h)),
        ],
        out_specs=pl.BlockSpec((CTX_LEN, dh), lambda b, h: (b, h)),
        out_shape=jax.ShapeDtypeStruct((M_CTX, D_MODEL), BF16),
        compiler_params=_params(2),
        name="na_context",
    )(qkv, qkv, qkv)
    return jnp.concatenate([lat, ctx], axis=0)


def _shift_rows(x, offset, length):
    t = lax.broadcasted_iota(jnp.int32, (length, 1), 0)
    rolled = pltpu.roll(x, (-offset) % length, 0)
    return jnp.where((t + offset >= 0) & (t + offset < length), rolled, 0.0)


def _rnn_conv(x, cw, cb):
    length = x.shape[0]
    left = RNN_CONV // 2
    y = cb
    for j in range(RNN_CONV):
        y = y + _shift_rows(x, j - left, length) * cw[j:j + 1]
    return y


def _scan_blocks(a_ref, b_ref, h_ref, carry, length, reverse, accumulate):
    nblk = length // SUBLANES
    row = lax.broadcasted_iota(jnp.int32, (SUBLANES, RNN_HEAD_DIM), 0)

    def block(n, carry):
        blk = (nblk - 1 - n) if reverse else n
        rows = pl.ds(pl.multiple_of(blk * SUBLANES, SUBLANES), SUBLANES)
        a = a_ref[rows, :]
        b = b_ref[rows, :]
        for s in (1, 2, 4):
            if reverse:
                keep = row < SUBLANES - s
                shift = SUBLANES - s
            else:
                keep = row >= s
                shift = s
            a_sh = pltpu.roll(a, shift, 0)
            b_sh = pltpu.roll(b, shift, 0)
            b = jnp.where(keep, a * b_sh + b, b)
            a = jnp.where(keep, a * a_sh, a)
        h = a * carry + b
        if accumulate:
            h_ref[rows, :] = h_ref[rows, :] + h
        else:
            h_ref[rows, :] = h
        edge = h[0:1] if reverse else h[SUBLANES - 1:SUBLANES]
        return jnp.broadcast_to(edge, (SUBLANES, RNN_HEAD_DIM))

    return lax.fori_loop(0, nblk, block, carry)


def _rglru_kernel(xc_ref, xl_ref, yc_ref, yl_ref, cw_ref, cb_ref, wg_ref, bg_ref, lam_ref,
                  oc_ref, ol_ref, ac_ref, bc_ref, al_ref, bl_ref, hc_ref, hl_ref):
    cw = cw_ref[...]
    cb = cb_ref[...]
    xr_c = _rnn_conv(xc_ref[...].astype(F32), cw, cb)
    xr_l = _rnn_conv(xl_ref[...].astype(F32), cw, cb)

    def gates(xr, d, a_ref, b_ref):
        xb = xr.astype(BF16)
        g_r = jnp.dot(xb, wg_ref[2 * d].astype(BF16), preferred_element_type=F32) + bg_ref[2 * d]
        g_i = jnp.dot(xb, wg_ref[2 * d + 1].astype(BF16), preferred_element_type=F32) + bg_ref[2 * d + 1]
        g_r = jax.nn.sigmoid(g_r)
        g_i = jax.nn.sigmoid(g_i)
        neg_lam = -lam_ref[d:d + 1, :]
        softplus = jnp.maximum(neg_lam, 0.0) + jnp.log1p(jnp.exp(-jnp.abs(neg_lam)))
        log_a = -RG_C * g_r * softplus
        a = jnp.exp(log_a)
        a_ref[...] = a
        b_ref[...] = jnp.sqrt(1.0 - jnp.exp(2.0 * log_a)) * (g_i * xr)

    for d, reverse in enumerate((False, True)):
        gates(xr_c, d, ac_ref, bc_ref)
        gates(xr_l, d, al_ref, bl_ref)
        h0 = jnp.zeros((SUBLANES, RNN_HEAD_DIM), F32)
        h_end = _scan_blocks(ac_ref, bc_ref, hc_ref, h0, CTX_LEN, reverse, accumulate=d > 0)
        _scan_blocks(al_ref, bl_ref, hl_ref, h_end, SEQ, reverse, accumulate=d > 0)

    oc_ref[...] = (jax.nn.gelu(yc_ref[...].astype(F32)) * hc_ref[...]).astype(oc_ref.dtype)
    ol_ref[...] = (jax.nn.gelu(yl_ref[...].astype(F32)) * hl_ref[...]).astype(ol_ref.dtype)


def _rglru(xin, conv_w, conv_b, w_gate, b_gate, lam):
    dh, nh = RNN_HEAD_DIM, RNN_HEADS
    ctx_blk0 = M_LAT // CTX_LEN
    wg = w_gate.reshape(4, nh, dh, dh)
    bg = b_gate.reshape(4, nh, 1, dh)
    out_c, out_l = pl.pallas_call(
        _rglru_kernel,
        grid=(BATCH, nh),
        in_specs=[
            pl.BlockSpec((CTX_LEN, dh), lambda b, h: (ctx_blk0 + b, nh + h)),
            pl.BlockSpec((SEQ, dh), lambda b, h: (b, nh + h)),
            pl.BlockSpec((CTX_LEN, dh), lambda b, h: (ctx_blk0 + b, h)),
            pl.BlockSpec((SEQ, dh), lambda b, h: (b, h)),
            pl.BlockSpec((RNN_CONV, dh), lambda b, h: (0, h)),
            pl.BlockSpec((1, dh), lambda b, h: (0, h)),
            pl.BlockSpec((4, None, dh, dh), lambda b, h: (0, h, 0, 0)),
            pl.BlockSpec((4, None, 1, dh), lambda b, h: (0, h, 0, 0)),
            pl.BlockSpec((2, dh), lambda b, h: (0, h)),
        ],
        out_specs=[
            pl.BlockSpec((CTX_LEN, dh), lambda b, h: (b, h)),
            pl.BlockSpec((SEQ, dh), lambda b, h: (b, h)),
        ],
        out_shape=[
            jax.ShapeDtypeStruct((M_CTX, RNN_WIDTH), BF16),
            jax.ShapeDtypeStruct((M_LAT, RNN_WIDTH), BF16),
        ],
        scratch_shapes=[
            pltpu.VMEM((CTX_LEN, dh), F32), pltpu.VMEM((CTX_LEN, dh), F32),
            pltpu.VMEM((SEQ, dh), F32), pltpu.VMEM((SEQ, dh), F32),
            pltpu.VMEM((CTX_LEN, dh), F32), pltpu.VMEM((SEQ, dh), F32),
        ],
        compiler_params=_params(2),
        name="rglru",
    )(xin, xin, xin, xin, conv_w, conv_b.reshape(1, RNN_WIDTH), wg, bg, lam)
    return jnp.concatenate([out_l, out_c], axis=0)


def kernel(x, c, ctx, c_ctx, ada_w, ada_b, norm_g, ffn_w_up, ffn_conv_w, ffn_conv_b, ffn_w_down,
           a_w_in, a_g_v, a_w_s, a_b_s, a_w_out, b_w_qkv, b_rpb, b_w_out,
           c_w_in, c_conv_w, c_conv_b, c_w_gate, c_b_gate, c_lam, c_w_out, final_g):
    h = jnp.concatenate([x.reshape(M_LAT, D_MODEL), ctx.reshape(M_CTX, D_MODEL)], axis=0)
    cvec = jnp.concatenate(
        [c, c_ctx[None], jnp.zeros((SUBLANES - N_GROUPS, D_MODEL), F32)], axis=0)
    mod = _ada_mod(cvec, ada_w, ada_b)

    for i in range(DEPTH):
        last = i == DEPTH - 1
        kind, j = i % N_MIXERS, i // N_MIXERS
        ffn_rows = M_LAT if last else M_ALL
        n_rows = M_LAT if (last and kind == 0) else M_ALL

        n = _norm_mod(h, norm_g, mod, i, 0, n_rows)
        if kind == 0:
            z = _matmul(n, a_w_in[j], n_rows=n_rows, tn=1024, out_dtype=BF16, act="gelu")
            y = _sgu(z, a_g_v[j], a_w_s[j], a_b_s[j], n_rows=n_rows)
            w_out = a_w_out[j]
        elif kind == 1:
            qkv = _matmul(n, b_w_qkv[j], n_rows=n_rows, tn=1024, out_dtype=BF16)
            y = _na_attention(qkv, b_rpb[j])
            w_out = b_w_out[j]
        else:
            xin = _matmul(n, c_w_in[j], n_rows=n_rows, tn=1024, out_dtype=F32)
            y = _rglru(xin, c_conv_w[j], c_conv_b[j], c_w_gate[j], c_b_gate[j], c_lam[j])
            w_out = c_w_out[j]
        h = _matmul_resid(y, w_out, h, mod, i, 2, n_rows=n_rows, tn=512)

        n = _norm_mod(h, norm_g, mod, i, 1, ffn_rows)
        a = _ffn_up(n, ffn_w_up[i], ffn_conv_w[i], ffn_conv_b[i], n_rows=ffn_rows)
        h = _matmul_resid(a, ffn_w_down[i], h, mod, i, 5, n_rows=ffn_rows, tn=512)

    return _final_norm(h, final_g).reshape(BATCH, SEQ, D_MODEL)
```

```python
import functools

import jax
import jax.numpy as jnp
from jax import lax
from jax.experimental import pallas as pl
from jax.experimental.pallas import tpu as pltpu

F32 = jnp.float32
BF16 = jnp.bfloat16

D_MODEL = 2048
BATCH = 2
SEQ = 4096
DEPTH = 4
GRID_W = 64
CTX_LEN = 256
N_MIXERS = 3
N_MOD = 6
EPS = 1e-6
NEG_INF = -1e30
D_FF = 5632
CHUNK = 128
A_WIDTH = 2 * D_MODEL
A_GROUPS = 16
NA_HEADS = 16
NA_HEAD_DIM = D_MODEL // NA_HEADS
NA_KH = 8
NA_KW = 16
RNN_WIDTH = D_MODEL
RNN_HEADS = 16
RNN_HEAD_DIM = RNN_WIDTH // RNN_HEADS
RNN_CONV = 4
RG_C = 8.0

M_LAT = BATCH * SEQ
M_CTX = BATCH * CTX_LEN
M_ALL = M_LAT + M_CTX
GRID_ROWS = SEQ // GRID_W

VMEM_LIMIT_BYTES = 56 * 1024 * 1024
SUBLANES = 8
LANES = 128
BF16_ROWS = 16

ROW_TILE = {M_ALL: M_ALL // 8, M_LAT: M_LAT // 8}
ROW_TILE_WIDE_K = {M_ALL: M_ALL // 16, M_LAT: M_LAT // 16}
NORM_ROW_TILE = 512


def _params(n_axes):
    return pltpu.CompilerParams(
        dimension_semantics=("arbitrary",) * n_axes, vmem_limit_bytes=VMEM_LIMIT_BYTES)


def _mod_spec(layer, which, tn=D_MODEL, grid_rank=1):
    if grid_rank == 1:
        return pl.BlockSpec((None, SUBLANES, tn), lambda i: (layer * N_MOD + which, 0, 0))
    return pl.BlockSpec((None, SUBLANES, tn), lambda j, i: (layer * N_MOD + which, 0, j))


def _rows_of_group(mod_ref, row0, tm):
    r = row0 + lax.broadcasted_iota(jnp.int32, (tm, 1), 0)
    m = mod_ref[...]
    out = m[BATCH:BATCH + 1]
    for b in reversed(range(BATCH)):
        out = jnp.where(r < (b + 1) * SEQ, m[b:b + 1], out)
    return out


def _ada_kernel(c_ref, w_ref, b_ref, o_ref):
    c = c_ref[...]
    s = (c * jax.nn.sigmoid(c)).astype(BF16)
    o_ref[...] = jnp.dot(s, w_ref[...].astype(BF16), preferred_element_type=F32) + b_ref[...]


def _ada_mod(cvec, ada_w, ada_b):
    depth, d, n = ada_w.shape
    tn = 1024
    out = pl.pallas_call(
        _ada_kernel,
        grid=(depth, n // tn),
        in_specs=[
            pl.BlockSpec((SUBLANES, d), lambda l, j: (0, 0)),
            pl.BlockSpec((None, d, tn), lambda l, j: (l, 0, j)),
            pl.BlockSpec((None, 1, tn), lambda l, j: (l, 0, j)),
        ],
        out_specs=pl.BlockSpec((None, SUBLANES, tn), lambda l, j: (l, 0, j)),
        out_shape=jax.ShapeDtypeStruct((depth, SUBLANES, n), F32),
        compiler_params=_params(2),
        name="ada_mod",
    )(cvec, ada_w, ada_b.reshape(depth, 1, n))
    return out.reshape(depth, SUBLANES, N_MOD, d).transpose(0, 2, 1, 3).reshape(depth * N_MOD, SUBLANES, d)


def _norm_mod_kernel(h_ref, g_ref, shift_ref, scale_ref, o_ref):
    tm = h_ref.shape[0]
    row0 = pl.program_id(0) * tm
    x = h_ref[...]
    y = x * lax.rsqrt(jnp.mean(x * x, axis=-1, keepdims=True) + EPS)
    y = y * g_ref[...]
    y = y * (1 + _rows_of_group(scale_ref, row0, tm)) + _rows_of_group(shift_ref, row0, tm)
    o_ref[...] = y.astype(o_ref.dtype)


def _norm_mod(h, norm_g, mod, layer, which_norm, n_rows):
    tm = NORM_ROW_TILE
    g = norm_g.reshape(DEPTH * 2, 1, D_MODEL)
    return pl.pallas_call(
        _norm_mod_kernel,
        grid=(n_rows // tm,),
        in_specs=[
            pl.BlockSpec((tm, D_MODEL), lambda i: (i, 0)),
            pl.BlockSpec((None, 1, D_MODEL), lambda i: (layer * 2 + which_norm, 0, 0)),
            _mod_spec(layer, 3 * which_norm),
            _mod_spec(layer, 3 * which_norm + 1),
        ],
        out_specs=pl.BlockSpec((tm, D_MODEL), lambda i: (i, 0)),
        out_shape=jax.ShapeDtypeStruct((M_ALL, D_MODEL), BF16),
        compiler_params=_params(1),
        name="norm_mod",
    )(h, g, mod, mod)


def _final_norm_kernel(h_ref, g_ref, o_ref):
    x = h_ref[...]
    y = x * lax.rsqrt(jnp.mean(x * x, axis=-1, keepdims=True) + EPS)
    o_ref[...] = y * g_ref[...]


def _final_norm(h, final_g):
    tm = NORM_ROW_TILE
    return pl.pallas_call(
        _final_norm_kernel,
        grid=(M_LAT // tm,),
        in_specs=[
            pl.BlockSpec((tm, D_MODEL), lambda i: (i, 0)),
            pl.BlockSpec((1, D_MODEL), lambda i: (0, 0)),
        ],
        out_specs=pl.BlockSpec((tm, D_MODEL), lambda i: (i, 0)),
        out_shape=jax.ShapeDtypeStruct((M_LAT, D_MODEL), F32),
        compiler_params=_params(1),
        name="final_norm",
    )(h, final_g.reshape(1, D_MODEL))


def _mm_kernel(x_ref, w_ref, o_ref, wbf_ref, *, act):
    @pl.when(pl.program_id(1) == 0)
    def _():
        wbf_ref[...] = w_ref[...].astype(BF16)

    acc = jnp.dot(x_ref[...], wbf_ref[...], preferred_element_type=F32)
    if act == "gelu":
        acc = jax.nn.gelu(acc)
    o_ref[...] = acc.astype(o_ref.dtype)


def _mm_resid_kernel(x_ref, w_ref, gate_ref, res_ref, o_ref, wbf_ref):
    i = pl.program_id(1)
    tm = x_ref.shape[0]

    @pl.when(i == 0)
    def _():
        wbf_ref[...] = w_ref[...].astype(BF16)

    acc = jnp.dot(x_ref[...], wbf_ref[...], preferred_element_type=F32)
    o_ref[...] = res_ref[...] + _rows_of_group(gate_ref, i * tm, tm) * acc


def _matmul(x, w, w_idx, *, n_rows, tn, out_dtype, act=None):
    tm = ROW_TILE[n_rows]
    _, k, n = w.shape
    return pl.pallas_call(
        functools.partial(_mm_kernel, act=act),
        grid=(n // tn, n_rows // tm),
        in_specs=[
            pl.BlockSpec((tm, k), lambda j, i: (i, 0)),
            pl.BlockSpec((None, k, tn), lambda j, i: (w_idx, 0, j)),
        ],
        out_specs=pl.BlockSpec((tm, tn), lambda j, i: (i, j)),
        out_shape=jax.ShapeDtypeStruct((x.shape[0], n), out_dtype),
        scratch_shapes=[pltpu.VMEM((k, tn), BF16)],
        compiler_params=_params(2),
        name="matmul",
    )(x, w)


def _matmul_resid(x, w, w_idx, h, mod, layer, which_gate, *, n_rows, tn):
    _, k, n = w.shape
    tm = (ROW_TILE_WIDE_K if k > 2 * D_MODEL else ROW_TILE)[n_rows]
    return pl.pallas_call(
        _mm_resid_kernel,
        grid=(n // tn, n_rows // tm),
        in_specs=[
            pl.BlockSpec((tm, k), lambda j, i: (i, 0)),
            pl.BlockSpec((None, k, tn), lambda j, i: (w_idx, 0, j)),
            _mod_spec(layer, which_gate, tn, grid_rank=2),
            pl.BlockSpec((tm, tn), lambda j, i: (i, j)),
        ],
        out_specs=pl.BlockSpec((tm, tn), lambda j, i: (i, j)),
        out_shape=jax.ShapeDtypeStruct(h.shape, F32),
        scratch_shapes=[pltpu.VMEM((k, tn), BF16)],
        input_output_aliases={3: 0},
        compiler_params=_params(2),
        name="matmul_resid",
    )(x, w, mod, h)


HALO = BF16_ROWS


def _seq_edge_masks(row0, tm):
    r = row0 + lax.broadcasted_iota(jnp.int32, (tm, 1), 0)
    is_lat = r < M_LAT
    first = jnp.where(is_lat, r & (SEQ - 1), r & (CTX_LEN - 1)) == 0
    last = jnp.where(is_lat, (r + 1) & (SEQ - 1), (r + 1) & (CTX_LEN - 1)) == 0
    return first, last


def _ffn_up_kernel(x_ref, xp_ref, xn_ref, wg_ref, wv_ref, cwg_ref, cwv_ref, cbg_ref, cbv_ref,
                   o_ref, wg_bf, wv_bf, xs_ref):
    i = pl.program_id(1)
    tm = x_ref.shape[0]

    @pl.when(i == 0)
    def _():
        wg_bf[...] = wg_ref[...].astype(BF16)
        wv_bf[...] = wv_ref[...].astype(BF16)

    xs_ref[0:HALO, :] = xp_ref[...]
    xs_ref[HALO:HALO + tm, :] = x_ref[...]
    xs_ref[HALO + tm:, :] = xn_ref[...]
    xs = xs_ref[...]
    first, last = _seq_edge_masks(i * tm, tm)
    n_ext = tm + 2 * HALO

    def conv(z, cw_ref, cb_ref):
        prev = pltpu.roll(z, 1, 0)[HALO:HALO + tm]
        nxt = pltpu.roll(z, n_ext - 1, 0)[HALO:HALO + tm]
        cur = z[HALO:HALO + tm]
        cw = cw_ref[...]
        y = cb_ref[...] + jnp.where(first, 0.0, prev) * cw[0:1]
        y = y + cur * cw[1:2]
        return y + jnp.where(last, 0.0, nxt) * cw[2:3]

    g = conv(jnp.dot(xs, wg_bf[...], preferred_element_type=F32), cwg_ref, cbg_ref)
    v = conv(jnp.dot(xs, wv_bf[...], preferred_element_type=F32), cwv_ref, cbv_ref)
    o_ref[...] = (g * jax.nn.sigmoid(g) * v).astype(o_ref.dtype)


def _ffn_up(n, w_up, conv_w, conv_b, layer, *, n_rows):
    tm, tn = ROW_TILE[n_rows], 512
    k = w_up.shape[1]
    nj = D_FF // tn
    halo_per_tile = tm // HALO
    last_halo = M_ALL // HALO - 1
    cb = conv_b.reshape(DEPTH, 1, 2 * D_FF)
    return pl.pallas_call(
        _ffn_up_kernel,
        grid=(nj, n_rows // tm),
        in_specs=[
            pl.BlockSpec((tm, k), lambda j, i: (i, 0)),
            pl.BlockSpec((HALO, k), lambda j, i: (jnp.maximum(i * halo_per_tile - 1, 0), 0)),
            pl.BlockSpec((HALO, k), lambda j, i: (jnp.minimum((i + 1) * halo_per_tile, last_halo), 0)),
            pl.BlockSpec((None, k, tn), lambda j, i: (layer, 0, j)),
            pl.BlockSpec((None, k, tn), lambda j, i: (layer, 0, j + nj)),
            pl.BlockSpec((None, 3, tn), lambda j, i: (layer, 0, j)),
            pl.BlockSpec((None, 3, tn), lambda j, i: (layer, 0, j + nj)),
            pl.BlockSpec((None, 1, tn), lambda j, i: (layer, 0, j)),
            pl.BlockSpec((None, 1, tn), lambda j, i: (layer, 0, j + nj)),
        ],
        out_specs=pl.BlockSpec((tm, tn), lambda j, i: (i, j)),
        out_shape=jax.ShapeDtypeStruct((M_ALL, D_FF), BF16),
        scratch_shapes=[
            pltpu.VMEM((k, tn), BF16),
            pltpu.VMEM((k, tn), BF16),
            pltpu.VMEM((tm + 2 * HALO, k), BF16),
        ],
        compiler_params=_params(2),
        name="ffn_up_conv_gate",
    )(n, n, n, w_up, w_up, conv_w, conv_w, cb, cb)


def _sgu_kernel(u_ref, v_ref, gv_ref, ws_ref, bs_ref, o_ref):
    tm = u_ref.shape[0]
    gw = A_WIDTH // A_GROUPS
    gv = gv_ref[...]
    for c in range(tm // CHUNK):
        rows = slice(c * CHUNK, (c + 1) * CHUNK)
        v = v_ref[rows, :].astype(F32)
        vn = v * lax.rsqrt(jnp.mean(v * v, axis=-1, keepdims=True) + EPS) * gv
        vn = vn.astype(BF16)
        for g in range(A_GROUPS):
            cols = slice(g * gw, (g + 1) * gw)
            s = jnp.dot(ws_ref[g].astype(BF16), vn[:, cols], preferred_element_type=F32)
            s = s + bs_ref[:, g:g + 1]
            o_ref[rows, cols] = (u_ref[rows, cols].astype(F32) * s).astype(o_ref.dtype)


def _sgu(z, g_v, w_s, b_s, j, *, n_rows):
    tm = 2 * CHUNK
    n_a = g_v.shape[0]
    return pl.pallas_call(
        _sgu_kernel,
        grid=(n_rows // tm,),
        in_specs=[
            pl.BlockSpec((tm, A_WIDTH), lambda i: (i, 0)),
            pl.BlockSpec((tm, A_WIDTH), lambda i: (i, 1)),
            pl.BlockSpec((None, 1, A_WIDTH), lambda i: (j, 0, 0)),
            pl.BlockSpec((None, A_GROUPS, CHUNK, CHUNK), lambda i: (j, 0, 0, 0)),
            pl.BlockSpec((None, CHUNK, A_GROUPS), lambda i: (j, 0, 0)),
        ],
        out_specs=pl.BlockSpec((tm, A_WIDTH), lambda i: (i, 0)),
        out_shape=jax.ShapeDtypeStruct((M_ALL, A_WIDTH), BF16),
        compiler_params=_params(1),
        name="sgu",
    )(z, z, g_v.reshape(n_a, 1, A_WIDTH), w_s, b_s.transpose(0, 2, 1))


NA_WIN = NA_KH * GRID_W
NA_DR = 2 * NA_KH - 1
NA_DC = 2 * NA_KW - 1
NA_PAIRS = NA_DR - 1
NA_ROW_UNROLL = 4


def _na_bias_pairs(rpb_ref, head, pair_ref):
    shape = (GRID_W, 2 * GRID_W)
    q = lax.broadcasted_iota(jnp.int32, shape, 0)
    lane = lax.broadcasted_iota(jnp.int32, shape, 1)
    kcol = lane & (GRID_W - 1)
    upper = lane >= GRID_W
    dc = jnp.clip(kcol - q, -(NA_KW - 1), NA_KW - 1) + NA_KW - 1
    c0 = jnp.clip(q - NA_KW // 2, 0, GRID_W - NA_KW)
    ok = (kcol >= c0) & (kcol < c0 + NA_KW)
    for dr in range(NA_PAIRS):
        acc = jnp.full(shape, NEG_INF, F32)
        for c in range(NA_DC):
            lo = rpb_ref[head, dr * NA_DC + c]
            hi = rpb_ref[head, (dr + 1) * NA_DC + c]
            acc = jnp.where(dc == c, jnp.where(upper, hi, lo), acc)
        pair_ref[dr] = jnp.where(ok, acc, NEG_INF)


def _na_lat_kernel(rpb_ref, q_ref, k_ref, v_ref, kc_ref, vc_ref, o_ref, pair_ref):
    scale = NA_HEAD_DIM ** -0.5
    _na_bias_pairs(rpb_ref, pl.program_id(1), pair_ref)
    kc = kc_ref[...]
    vc = vc_ref[...]
    nt = (((1,), (1,)), ((), ()))

    def row(r, carry):
        r0 = jnp.clip(r - NA_KH // 2, 0, GRID_ROWS - NA_KH)
        q = q_ref[pl.ds(pl.multiple_of(r * GRID_W, GRID_W), GRID_W), :]
        win = pl.ds(pl.multiple_of(r0 * GRID_W, GRID_W), NA_WIN)
        kw = k_ref[win, :]
        vw = v_ref[win, :]
        dr0 = r0 - r + NA_KH - 1
        bias = jnp.concatenate([pair_ref[dr0 + 2 * t] for t in range(NA_KH // 2)], axis=1)
        s = lax.dot_general(q, kw, nt, preferred_element_type=F32) * scale
        s = jnp.where(bias > 0.5 * NEG_INF, s + bias, NEG_INF)
        sc = lax.dot_general(q, kc, nt, preferred_element_type=F32) * scale
        m = jnp.maximum(jnp.max(s, axis=-1, keepdims=True), jnp.max(sc, axis=-1, keepdims=True))
        p = jnp.exp(s - m)
        pc = jnp.exp(sc - m)
        denom = jnp.sum(p, axis=-1, keepdims=True) + jnp.sum(pc, axis=-1, keepdims=True)
        o = jnp.dot(p.astype(BF16), vw, preferred_element_type=F32)
        o = o + jnp.dot(pc.astype(BF16), vc, preferred_element_type=F32)
        o_ref[pl.ds(pl.multiple_of(r * GRID_W, GRID_W), GRID_W), :] = (o / denom).astype(o_ref.dtype)
        return carry

    lax.fori_loop(0, GRID_ROWS, row, 0, unroll=NA_ROW_UNROLL)


def _na_ctx_kernel(q_ref, k_ref, v_ref, o_ref):
    scale = NA_HEAD_DIM ** -0.5
    s = lax.dot_general(q_ref[...], k_ref[...], (((1,), (1,)), ((), ())),
                        preferred_element_type=F32) * scale
    m = jnp.max(s, axis=-1, keepdims=True)
    p = jnp.exp(s - m)
    denom = jnp.sum(p, axis=-1, keepdims=True)
    o = jnp.dot(p.astype(BF16), v_ref[...], preferred_element_type=F32)
    o_ref[...] = (o / denom).astype(o_ref.dtype)


def _na_attention(qkv, rpb):
    dh, nh = NA_HEAD_DIM, NA_HEADS
    ctx_blk0 = M_LAT // CTX_LEN
    lat = pl.pallas_call(
        _na_lat_kernel,
        grid=(BATCH, nh),
        in_specs=[
            pl.BlockSpec(memory_space=pltpu.SMEM),
            pl.BlockSpec((SEQ, dh), lambda b, h: (b, h)),
            pl.BlockSpec((SEQ, dh), lambda b, h: (b, nh + h)),
            pl.BlockSpec((SEQ, dh), lambda b, h: (b, 2 * nh + h)),
            pl.BlockSpec((CTX_LEN, dh), lambda b, h: (ctx_blk0 + b, nh + h)),
            pl.BlockSpec((CTX_LEN, dh), lambda b, h: (ctx_blk0 + b, 2 * nh + h)),
        ],
        out_specs=pl.BlockSpec((SEQ, dh), lambda b, h: (b, h)),
        out_shape=jax.ShapeDtypeStruct((M_ALL, D_MODEL), BF16),
        scratch_shapes=[pltpu.VMEM((NA_PAIRS, GRID_W, 2 * GRID_W), F32)],
        compiler_params=_params(2),
        name="na_latent",
    )(rpb.reshape(nh, NA_DR * NA_DC), qkv, qkv, qkv, qkv, qkv)
    ctx = pl.pallas_call(
        _na_ctx_kernel,
        grid=(BATCH, nh),
        in_specs=[
            pl.BlockSpec((CTX_LEN, dh), lambda b, h: (ctx_blk0 + b, h)),
            pl.BlockSpec((CTX_LEN, dh), lambda b, h: (ctx_blk0 + b, nh + h)),
            pl.BlockSpec((CTX_LEN, dh), lambda b, h: (ctx_blk0 + b, 2 * nh + h)),
        ],
        out_specs=pl.BlockSpec((CTX_LEN, dh), lambda b, h: (b, h)),
        out_shape=jax.ShapeDtypeStruct((M_CTX, D_MODEL), BF16),
        compiler_params=_params(2),
        name="na_context",
    )(qkv, qkv, qkv)
    return lax.dynamic_update_slice(lat, ctx, (M_LAT, 0))


RNN_LEN = CTX_LEN + SEQ
SCAN_UNROLL = 8


def _shift_rows(x, offset, length):
    t = lax.broadcasted_iota(jnp.int32, (length, 1), 0)
    rolled = pltpu.roll(x, (-offset) % length, 0)
    return jnp.where((t + offset >= 0) & (t + offset < length), rolled, 0.0)


def _rnn_conv(x, cw, cb):
    length = x.shape[0]
    left = RNN_CONV // 2
    y = cb
    for j in range(RNN_CONV):
        y = y + _shift_rows(x, j - left, length) * cw[j:j + 1]
    return y


def _block_scan(a, b, reverse):
    row = lax.broadcasted_iota(jnp.int32, a.shape, 0)
    for s in (1, 2, 4):
        if reverse:
            keep = row < SUBLANES - s
            shift = SUBLANES - s
        else:
            keep = row >= s
            shift = s
        a_sh = pltpu.roll(a, shift, 0)
        b_sh = pltpu.roll(b, shift, 0)
        b = jnp.where(keep, a * b_sh + b, b)
        a = jnp.where(keep, a * a_sh, a)
    return a, b


def _rglru_kernel(xc_ref, xl_ref, yc_ref, yl_ref, cw_ref, cb_ref, wg_ref, bg_ref, lam_ref,
                  oc_ref, ol_ref, af_ref, bf_ref, ar_ref, br_ref, hf_ref, hr_ref):
    cw = cw_ref[...]
    cb = cb_ref[...]
    xr_c = _rnn_conv(xc_ref[...].astype(F32), cw, cb)
    xr_l = _rnn_conv(xl_ref[...].astype(F32), cw, cb)

    def gates(xr, d, a_ref, b_ref, rows):
        xb = xr.astype(BF16)
        g_r = jnp.dot(xb, wg_ref[2 * d].astype(BF16), preferred_element_type=F32) + bg_ref[2 * d]
        g_i = jnp.dot(xb, wg_ref[2 * d + 1].astype(BF16), preferred_element_type=F32) + bg_ref[2 * d + 1]
        g_r = jax.nn.sigmoid(g_r)
        g_i = jax.nn.sigmoid(g_i)
        neg_lam = -lam_ref[d:d + 1, :]
        softplus = jnp.maximum(neg_lam, 0.0) + jnp.log1p(jnp.exp(-jnp.abs(neg_lam)))
        log_a = -RG_C * g_r * softplus
        a_ref[rows, :] = jnp.exp(log_a)
        b_ref[rows, :] = jnp.sqrt(1.0 - jnp.exp(2.0 * log_a)) * (g_i * xr)

    ctx_f, lat_f = slice(0, CTX_LEN), slice(CTX_LEN, RNN_LEN)
    lat_r, ctx_r = slice(0, SEQ), slice(SEQ, RNN_LEN)
    gates(xr_c, 0, af_ref, bf_ref, ctx_f)
    gates(xr_l, 0, af_ref, bf_ref, lat_f)
    gates(xr_c, 1, ar_ref, br_ref, ctx_r)
    gates(xr_l, 1, ar_ref, br_ref, lat_r)

    nblk = RNN_LEN // SUBLANES
    last_row = slice(SUBLANES - 1, SUBLANES)
    first_row = slice(0, 1)
    blk_shape = (SUBLANES, RNN_HEAD_DIM)

    def block(n, carry):
        cf, cr = carry
        rows_f = pl.ds(pl.multiple_of(n * SUBLANES, SUBLANES), SUBLANES)
        rows_r = pl.ds(pl.multiple_of((nblk - 1 - n) * SUBLANES, SUBLANES), SUBLANES)
        a_f, b_f = _block_scan(af_ref[rows_f, :], bf_ref[rows_f, :], reverse=False)
        a_r, b_r = _block_scan(ar_ref[rows_r, :], br_ref[rows_r, :], reverse=True)
        hf_ref[rows_f, :] = a_f * cf + b_f
        hr_ref[rows_r, :] = a_r * cr + b_r
        cf = jnp.broadcast_to(a_f[last_row], blk_shape) * cf + jnp.broadcast_to(b_f[last_row], blk_shape)
        cr = jnp.broadcast_to(a_r[first_row], blk_shape) * cr + jnp.broadcast_to(b_r[first_row], blk_shape)
        return cf, cr

    zero = jnp.zeros(blk_shape, F32)
    lax.fori_loop(0, nblk, block, (zero, zero), unroll=SCAN_UNROLL)

    h_c = hf_ref[ctx_f, :] + hr_ref[ctx_r, :]
    h_l = hf_ref[lat_f, :] + hr_ref[lat_r, :]
    oc_ref[...] = (jax.nn.gelu(yc_ref[...].astype(F32)) * h_c).astype(oc_ref.dtype)
    ol_ref[...] = (jax.nn.gelu(yl_ref[...].astype(F32)) * h_l).astype(ol_ref.dtype)


def _rglru(xin, conv_w, conv_b, w_gate, b_gate, lam, j):
    dh, nh = RNN_HEAD_DIM, RNN_HEADS
    n_c = conv_w.shape[0]
    ctx_blk0 = M_LAT // CTX_LEN
    wg = w_gate.reshape(n_c, 4, nh, dh, dh)
    bg = b_gate.reshape(n_c, 4, nh, 1, dh)
    out_c, out_l = pl.pallas_call(
        _rglru_kernel,
        grid=(BATCH, nh),
        in_specs=[
            pl.BlockSpec((CTX_LEN, dh), lambda b, h: (ctx_blk0 + b, nh + h)),
            pl.BlockSpec((SEQ, dh), lambda b, h: (b, nh + h)),
            pl.BlockSpec((CTX_LEN, dh), lambda b, h: (ctx_blk0 + b, h)),
            pl.BlockSpec((SEQ, dh), lambda b, h: (b, h)),
            pl.BlockSpec((None, RNN_CONV, dh), lambda b, h: (j, 0, h)),
            pl.BlockSpec((None, 1, dh), lambda b, h: (j, 0, h)),
            pl.BlockSpec((None, 4, None, dh, dh), lambda b, h: (j, 0, h, 0, 0)),
            pl.BlockSpec((None, 4, None, 1, dh), lambda b, h: (j, 0, h, 0, 0)),
            pl.BlockSpec((None, 2, dh), lambda b, h: (j, 0, h)),
        ],
        out_specs=[
            pl.BlockSpec((CTX_LEN, dh), lambda b, h: (b, h)),
            pl.BlockSpec((SEQ, dh), lambda b, h: (b, h)),
        ],
        out_shape=[
            jax.ShapeDtypeStruct((M_CTX, RNN_WIDTH), BF16),
            jax.ShapeDtypeStruct((M_ALL, RNN_WIDTH), BF16),
        ],
        scratch_shapes=[pltpu.VMEM((RNN_LEN, dh), F32)] * 6,
        compiler_params=_params(2),
        name="rglru",
    )(xin, xin, xin, xin, conv_w, conv_b.reshape(n_c, 1, RNN_WIDTH), wg, bg, lam)
    return lax.dynamic_update_slice(out_l, out_c, (M_LAT, 0))


def kernel(x, c, ctx, c_ctx, ada_w, ada_b, norm_g, ffn_w_up, ffn_conv_w, ffn_conv_b, ffn_w_down,
           a_w_in, a_g_v, a_w_s, a_b_s, a_w_out, b_w_qkv, b_rpb, b_w_out,
           c_w_in, c_conv_w, c_conv_b, c_w_gate, c_b_gate, c_lam, c_w_out, final_g):
    h = jnp.concatenate([x.reshape(M_LAT, D_MODEL), ctx.reshape(M_CTX, D_MODEL)], axis=0)
    cvec = jnp.concatenate(
        [c, c_ctx[None], jnp.zeros((SUBLANES - BATCH - 1, D_MODEL), F32)], axis=0)
    mod = _ada_mod(cvec, ada_w, ada_b)

    for i in range(DEPTH):
        last = i == DEPTH - 1
        kind, j = i % N_MIXERS, i // N_MIXERS
        ffn_rows = M_LAT if last else M_ALL
        n_rows = M_LAT if (last and kind == 0) else M_ALL

        n = _norm_mod(h, norm_g, mod, i, 0, n_rows)
        if kind == 0:
            z = _matmul(n, a_w_in, j, n_rows=n_rows, tn=1024, out_dtype=BF16, act="gelu")
            y = _sgu(z, a_g_v, a_w_s, a_b_s, j, n_rows=n_rows)
            w_out = a_w_out
        elif kind == 1:
            qkv = _matmul(n, b_w_qkv, j, n_rows=n_rows, tn=1024, out_dtype=BF16)
            y = _na_attention(qkv, b_rpb[j])
            w_out = b_w_out
        else:
            xin = _matmul(n, c_w_in, j, n_rows=n_rows, tn=1024, out_dtype=F32)
            y = _rglru(xin, c_conv_w, c_conv_b, c_w_gate, c_b_gate, c_lam, j)
            w_out = c_w_out
        h = _matmul_resid(y, w_out, j, h, mod, i, 2, n_rows=n_rows, tn=512 if kind == 0 else 1024)

        n = _norm_mod(h, norm_g, mod, i, 1, ffn_rows)
        a = _ffn_up(n, ffn_w_up, ffn_conv_w, ffn_conv_b, i, n_rows=ffn_rows)
        h = _matmul_resid(a, ffn_w_down, i, h, mod, i, 5, n_rows=ffn_rows, tn=512)

    return _final_norm(h, final_g).reshape(BATCH, SEQ, D_MODEL)
```

```python
import functools

import jax
import jax.numpy as jnp
from jax import lax
from jax.experimental import pallas as pl
from jax.experimental.pallas import tpu as pltpu

F32 = jnp.float32
BF16 = jnp.bfloat16

D_MODEL = 2048
BATCH = 2
SEQ = 4096
DEPTH = 4
GRID_W = 64
CTX_LEN = 256
N_MIXERS = 3
N_MOD = 6
EPS = 1e-6
NEG_INF = -1e30
D_FF = 5632
CHUNK = 128
A_WIDTH = 2 * D_MODEL
A_GROUPS = 16
NA_HEADS = 16
NA_HEAD_DIM = D_MODEL // NA_HEADS
NA_KH = 8
NA_KW = 16
RNN_WIDTH = D_MODEL
RNN_HEADS = 16
RNN_HEAD_DIM = RNN_WIDTH // RNN_HEADS
RNN_CONV = 4
RG_C = 8.0

M_LAT = BATCH * SEQ
M_CTX = BATCH * CTX_LEN
M_ALL = M_LAT + M_CTX
GRID_ROWS = SEQ // GRID_W

VMEM_LIMIT_BYTES = 56 * 1024 * 1024
SUBLANES = 8
LANES = 128
BF16_ROWS = 16

ROW_TILE = {M_ALL: M_ALL // 8, M_LAT: M_LAT // 8}
ROW_TILE_WIDE_K = {M_ALL: M_ALL // 16, M_LAT: M_LAT // 16}
NORM_ROW_TILE = 512


def _params(n_axes):
    return pltpu.CompilerParams(
        dimension_semantics=("arbitrary",) * n_axes, vmem_limit_bytes=VMEM_LIMIT_BYTES)


def _mod_spec(layer, which, tn=D_MODEL, grid_rank=1):
    if grid_rank == 1:
        return pl.BlockSpec((None, SUBLANES, tn), lambda i: (layer * N_MOD + which, 0, 0))
    return pl.BlockSpec((None, SUBLANES, tn), lambda j, i: (layer * N_MOD + which, 0, j))


def _rows_of_group(mod_ref, row0, tm):
    r = row0 + lax.broadcasted_iota(jnp.int32, (tm, 1), 0)
    m = mod_ref[...]
    out = m[BATCH:BATCH + 1]
    for b in reversed(range(BATCH)):
        out = jnp.where(r < (b + 1) * SEQ, m[b:b + 1], out)
    return out


def _ada_kernel(c_ref, w_ref, b_ref, o_ref):
    c = c_ref[...]
    s = (c * jax.nn.sigmoid(c)).astype(BF16)
    o_ref[...] = jnp.dot(s, w_ref[...].astype(BF16), preferred_element_type=F32) + b_ref[...]


def _ada_mod(cvec, ada_w, ada_b):
    depth, d, n = ada_w.shape
    tn = 1024
    out = pl.pallas_call(
        _ada_kernel,
        grid=(depth, n // tn),
        in_specs=[
            pl.BlockSpec((SUBLANES, d), lambda l, j: (0, 0)),
            pl.BlockSpec((None, d, tn), lambda l, j: (l, 0, j)),
            pl.BlockSpec((None, 1, tn), lambda l, j: (l, 0, j)),
        ],
        out_specs=pl.BlockSpec((None, SUBLANES, tn), lambda l, j: (l, 0, j)),
        out_shape=jax.ShapeDtypeStruct((depth, SUBLANES, n), F32),
        compiler_params=_params(2),
        name="ada_mod",
    )(cvec, ada_w, ada_b.reshape(depth, 1, n))
    return out.reshape(depth, SUBLANES, N_MOD, d).transpose(0, 2, 1, 3).reshape(depth * N_MOD, SUBLANES, d)


def _norm_mod_kernel(h_ref, g_ref, shift_ref, scale_ref, o_ref):
    group = pl.ds(jnp.minimum(pl.program_id(0) * h_ref.shape[0] // SEQ, BATCH), 1)
    x = h_ref[...]
    y = x * lax.rsqrt(jnp.mean(x * x, axis=-1, keepdims=True) + EPS)
    y = y * g_ref[...]
    y = y * (1 + scale_ref[group, :]) + shift_ref[group, :]
    o_ref[...] = y.astype(o_ref.dtype)


def _norm_mod(h, norm_g, mod, layer, which_norm, n_rows):
    tm = NORM_ROW_TILE
    g = norm_g.reshape(DEPTH * 2, 1, D_MODEL)
    return pl.pallas_call(
        _norm_mod_kernel,
        grid=(n_rows // tm,),
        in_specs=[
            pl.BlockSpec((tm, D_MODEL), lambda i: (i, 0)),
            pl.BlockSpec((None, 1, D_MODEL), lambda i: (layer * 2 + which_norm, 0, 0)),
            _mod_spec(layer, 3 * which_norm),
            _mod_spec(layer, 3 * which_norm + 1),
        ],
        out_specs=pl.BlockSpec((tm, D_MODEL), lambda i: (i, 0)),
        out_shape=jax.ShapeDtypeStruct((M_ALL, D_MODEL), BF16),
        compiler_params=_params(1),
        name="norm_mod",
    )(h, g, mod, mod)


def _final_norm_kernel(h_ref, g_ref, o_ref):
    x = h_ref[...]
    y = x * lax.rsqrt(jnp.mean(x * x, axis=-1, keepdims=True) + EPS)
    o_ref[...] = y * g_ref[...]


def _final_norm(h, final_g):
    tm = NORM_ROW_TILE
    return pl.pallas_call(
        _final_norm_kernel,
        grid=(M_LAT // tm,),
        in_specs=[
            pl.BlockSpec((tm, D_MODEL), lambda i: (i, 0)),
            pl.BlockSpec((1, D_MODEL), lambda i: (0, 0)),
        ],
        out_specs=pl.BlockSpec((tm, D_MODEL), lambda i: (i, 0)),
        out_shape=jax.ShapeDtypeStruct((M_LAT, D_MODEL), F32),
        compiler_params=_params(1),
        name="final_norm",
    )(h, final_g.reshape(1, D_MODEL))


def _mm_kernel(x_ref, w_ref, o_ref, wbf_ref, *, act):
    @pl.when(pl.program_id(1) == 0)
    def _():
        wbf_ref[...] = w_ref[...].astype(BF16)

    acc = jnp.dot(x_ref[...], wbf_ref[...], preferred_element_type=F32)
    if act == "gelu":
        acc = jax.nn.gelu(acc)
    o_ref[...] = acc.astype(o_ref.dtype)


def _mm_resid_kernel(x_ref, w_ref, gate_ref, res_ref, o_ref, wbf_ref):
    i = pl.program_id(1)
    tm = x_ref.shape[0]

    @pl.when(i == 0)
    def _():
        wbf_ref[...] = w_ref[...].astype(BF16)

    acc = jnp.dot(x_ref[...], wbf_ref[...], preferred_element_type=F32)
    o_ref[...] = res_ref[...] + _rows_of_group(gate_ref, i * tm, tm) * acc


def _matmul(x, w, w_idx, *, n_rows, tn, out_dtype, act=None):
    tm = ROW_TILE[n_rows]
    _, k, n = w.shape
    return pl.pallas_call(
        functools.partial(_mm_kernel, act=act),
        grid=(n // tn, n_rows // tm),
        in_specs=[
            pl.BlockSpec((tm, k), lambda j, i: (i, 0)),
            pl.BlockSpec((None, k, tn), lambda j, i: (w_idx, 0, j)),
        ],
        out_specs=pl.BlockSpec((tm, tn), lambda j, i: (i, j)),
        out_shape=jax.ShapeDtypeStruct((x.shape[0], n), out_dtype),
        scratch_shapes=[pltpu.VMEM((k, tn), BF16)],
        compiler_params=_params(2),
        name="matmul",
    )(x, w)


def _matmul_resid(x, w, w_idx, h, mod, layer, which_gate, *, n_rows, tn):
    _, k, n = w.shape
    tm = (ROW_TILE_WIDE_K if k > 2 * D_MODEL else ROW_TILE)[n_rows]
    return pl.pallas_call(
        _mm_resid_kernel,
        grid=(n // tn, n_rows // tm),
        in_specs=[
            pl.BlockSpec((tm, k), lambda j, i: (i, 0)),
            pl.BlockSpec((None, k, tn), lambda j, i: (w_idx, 0, j)),
            _mod_spec(layer, which_gate, tn, grid_rank=2),
            pl.BlockSpec((tm, tn), lambda j, i: (i, j)),
        ],
        out_specs=pl.BlockSpec((tm, tn), lambda j, i: (i, j)),
        out_shape=jax.ShapeDtypeStruct(h.shape, F32),
        scratch_shapes=[pltpu.VMEM((k, tn), BF16)],
        input_output_aliases={3: 0},
        compiler_params=_params(2),
        name="matmul_resid",
    )(x, w, mod, h)


HALO = BF16_ROWS


def _seq_edge_masks(row0, tm):
    r = row0 + lax.broadcasted_iota(jnp.int32, (tm, 1), 0)
    is_lat = r < M_LAT
    first = jnp.where(is_lat, r & (SEQ - 1), r & (CTX_LEN - 1)) == 0
    last = jnp.where(is_lat, (r + 1) & (SEQ - 1), (r + 1) & (CTX_LEN - 1)) == 0
    return first, last


def _ffn_up_kernel(x_ref, xp_ref, xn_ref, wg_ref, wv_ref, cwg_ref, cwv_ref, cbg_ref, cbv_ref,
                   o_ref, wg_bf, wv_bf, xs_ref, zg_ref, zv_ref):
    i = pl.program_id(1)
    tm = x_ref.shape[0]

    @pl.when(i == 0)
    def _():
        wg_bf[...] = wg_ref[...].astype(BF16)
        wv_bf[...] = wv_ref[...].astype(BF16)

    xs_ref[0:HALO, :] = xp_ref[...]
    xs_ref[HALO:HALO + tm, :] = x_ref[...]
    xs_ref[HALO + tm:, :] = xn_ref[...]
    xs = xs_ref[...]
    first, last = _seq_edge_masks(i * tm, tm)

    def conv(z_ref, cw_ref, cb_ref):
        prev = z_ref[HALO - 1:HALO - 1 + tm, :]
        cur = z_ref[HALO:HALO + tm, :]
        nxt = z_ref[HALO + 1:HALO + 1 + tm, :]
        cw = cw_ref[...]
        y = cb_ref[...] + jnp.where(first, 0.0, prev) * cw[0:1]
        y = y + cur * cw[1:2]
        return y + jnp.where(last, 0.0, nxt) * cw[2:3]

    zg_ref[...] = jnp.dot(xs, wg_bf[...], preferred_element_type=F32)
    zv_ref[...] = jnp.dot(xs, wv_bf[...], preferred_element_type=F32)
    g = conv(zg_ref, cwg_ref, cbg_ref)
    v = conv(zv_ref, cwv_ref, cbv_ref)
    o_ref[...] = (g * jax.nn.sigmoid(g) * v).astype(o_ref.dtype)


def _ffn_up(n, w_up, conv_w, conv_b, layer, *, n_rows):
    tm, tn = ROW_TILE[n_rows], 512
    k = w_up.shape[1]
    nj = D_FF // tn
    halo_per_tile = tm // HALO
    last_halo = M_ALL // HALO - 1
    cb = conv_b.reshape(DEPTH, 1, 2 * D_FF)
    return pl.pallas_call(
        _ffn_up_kernel,
        grid=(nj, n_rows // tm),
        in_specs=[
            pl.BlockSpec((tm, k), lambda j, i: (i, 0)),
            pl.BlockSpec((HALO, k), lambda j, i: (jnp.maximum(i * halo_per_tile - 1, 0), 0)),
            pl.BlockSpec((HALO, k), lambda j, i: (jnp.minimum((i + 1) * halo_per_tile, last_halo), 0)),
            pl.BlockSpec((None, k, tn), lambda j, i: (layer, 0, j)),
            pl.BlockSpec((None, k, tn), lambda j, i: (layer, 0, j + nj)),
            pl.BlockSpec((None, 3, tn), lambda j, i: (layer, 0, j)),
            pl.BlockSpec((None, 3, tn), lambda j, i: (layer, 0, j + nj)),
            pl.BlockSpec((None, 1, tn), lambda j, i: (layer, 0, j)),
            pl.BlockSpec((None, 1, tn), lambda j, i: (layer, 0, j + nj)),
        ],
        out_specs=pl.BlockSpec((tm, tn), lambda j, i: (i, j)),
        out_shape=jax.ShapeDtypeStruct((M_ALL, D_FF), BF16),
        scratch_shapes=[
            pltpu.VMEM((k, tn), BF16),
            pltpu.VMEM((k, tn), BF16),
            pltpu.VMEM((tm + 2 * HALO, k), BF16),
            pltpu.VMEM((tm + 2 * HALO, tn), F32),
            pltpu.VMEM((tm + 2 * HALO, tn), F32),
        ],
        compiler_params=_params(2),
        name="ffn_up_conv_gate",
    )(n, n, n, w_up, w_up, conv_w, conv_w, cb, cb)


def _sgu_kernel(u_ref, v_ref, gv_ref, ws_ref, bs_ref, o_ref):
    tm = u_ref.shape[0]
    gw = A_WIDTH // A_GROUPS
    gv = gv_ref[...]
    for c in range(tm // CHUNK):
        rows = slice(c * CHUNK, (c + 1) * CHUNK)
        v = v_ref[rows, :].astype(F32)
        vn = v * lax.rsqrt(jnp.mean(v * v, axis=-1, keepdims=True) + EPS) * gv
        vn = vn.astype(BF16)
        for g in range(A_GROUPS):
            cols = slice(g * gw, (g + 1) * gw)
            s = jnp.dot(ws_ref[g].astype(BF16), vn[:, cols], preferred_element_type=F32)
            s = s + bs_ref[:, g:g + 1]
            o_ref[rows, cols] = (u_ref[rows, cols].astype(F32) * s).astype(o_ref.dtype)


def _sgu(z, g_v, w_s, b_s, j, *, n_rows):
    tm = 2 * CHUNK
    n_a = g_v.shape[0]
    return pl.pallas_call(
        _sgu_kernel,
        grid=(n_rows // tm,),
        in_specs=[
            pl.BlockSpec((tm, A_WIDTH), lambda i: (i, 0)),
            pl.BlockSpec((tm, A_WIDTH), lambda i: (i, 1)),
            pl.BlockSpec((None, 1, A_WIDTH), lambda i: (j, 0, 0)),
            pl.BlockSpec((None, A_GROUPS, CHUNK, CHUNK), lambda i: (j, 0, 0, 0)),
            pl.BlockSpec((None, CHUNK, A_GROUPS), lambda i: (j, 0, 0)),
        ],
        out_specs=pl.BlockSpec((tm, A_WIDTH), lambda i: (i, 0)),
        out_shape=jax.ShapeDtypeStruct((M_ALL, A_WIDTH), BF16),
        compiler_params=_params(1),
        name="sgu",
    )(z, z, g_v.reshape(n_a, 1, A_WIDTH), w_s, b_s.transpose(0, 2, 1))


NA_DR = 2 * NA_KH - 1
NA_DC = 2 * NA_KW - 1
NA_PAIR_TILES = NA_DR + 1
NA_QROWS = 4
NA_KROWS = NA_KH + NA_QROWS
NA_BLOCK_UNROLL = 2


def _na_bias_pairs(rpb_ref, head, pair_ref):
    shape = (GRID_W, 2 * GRID_W)
    q = lax.broadcasted_iota(jnp.int32, shape, 0)
    lane = lax.broadcasted_iota(jnp.int32, shape, 1)
    kcol = lane & (GRID_W - 1)
    upper = lane >= GRID_W
    dc = jnp.clip(kcol - q, -(NA_KW - 1), NA_KW - 1) + NA_KW - 1
    c0 = jnp.clip(q - NA_KW // 2, 0, GRID_W - NA_KW)
    ok = (kcol >= c0) & (kcol < c0 + NA_KW)
    for d in range(-1, NA_DR):
        acc = jnp.full(shape, NEG_INF, F32)
        for c in range(NA_DC):
            lo = rpb_ref[head, d * NA_DC + c] if d >= 0 else NEG_INF
            hi = rpb_ref[head, (d + 1) * NA_DC + c] if d + 1 < NA_DR else NEG_INF
            acc = jnp.where(dc == c, jnp.where(upper, hi, lo), acc)
        pair_ref[d + 1] = jnp.where(ok, acc, NEG_INF)


def _na_lat_kernel(rpb_ref, q_ref, k_ref, v_ref, kc_ref, vc_ref, o_ref, pair_ref):
    scale = NA_HEAD_DIM ** -0.5
    _na_bias_pairs(rpb_ref, pl.program_id(1), pair_ref)
    kc = kc_ref[...]
    vc = vc_ref[...]
    nt = (((1,), (1,)), ((), ()))
    upper = lax.broadcasted_iota(jnp.int32, (GRID_W, 2 * GRID_W), 1) >= GRID_W
    n_q = NA_QROWS * GRID_W

    def block(m, carry):
        r_first = m * NA_QROWS
        ws = jnp.clip(r_first - NA_KH // 2, 0, GRID_ROWS - NA_KROWS)
        qrows = pl.ds(pl.multiple_of(m * n_q, n_q), n_q)
        win = pl.ds(pl.multiple_of(ws * GRID_W, GRID_W), NA_KROWS * GRID_W)
        q = q_ref[qrows, :]
        kw = k_ref[win, :]
        vw = v_ref[win, :]
        bias_rows = []
        for a in range(NA_QROWS):
            r = r_first + a
            r0 = jnp.clip(r - NA_KH // 2, 0, GRID_ROWS - NA_KH)
            tiles = []
            for p in range(NA_KROWS // 2):
                key_row = ws + 2 * p
                in_lo = ((key_row >= r0) & (key_row < r0 + NA_KH)).astype(jnp.int32)
                in_hi = ((key_row + 1 >= r0) & (key_row + 1 < r0 + NA_KH)).astype(jnp.int32)
                d = key_row - r + NA_KH - 1
                tile = pair_ref[jnp.clip(d + 1, 0, NA_PAIR_TILES - 1)]
                tiles.append(jnp.where(jnp.where(upper, in_hi, in_lo) != 0, tile, NEG_INF))
            bias_rows.append(jnp.concatenate(tiles, axis=1))
        bias = jnp.concatenate(bias_rows, axis=0)
        s = lax.dot_general(q, kw, nt, preferred_element_type=F32) * scale
        s = jnp.where(bias > 0.5 * NEG_INF, s + bias, NEG_INF)
        sc = lax.dot_general(q, kc, nt, preferred_element_type=F32) * scale
        mx = jnp.maximum(jnp.max(s, axis=-1, keepdims=True), jnp.max(sc, axis=-1, keepdims=True))
        p_w = jnp.exp(s - mx)
        p_c = jnp.exp(sc - mx)
        denom = jnp.sum(p_w, axis=-1, keepdims=True) + jnp.sum(p_c, axis=-1, keepdims=True)
        o = jnp.dot(p_w.astype(BF16), vw, preferred_element_type=F32)
        o = o + jnp.dot(p_c.astype(BF16), vc, preferred_element_type=F32)
        o_ref[qrows, :] = (o / denom).astype(o_ref.dtype)
        return carry

    lax.fori_loop(0, GRID_ROWS // NA_QROWS, block, 0, unroll=NA_BLOCK_UNROLL)


def _na_ctx_kernel(q_ref, k_ref, v_ref, o_ref):
    scale = NA_HEAD_DIM ** -0.5
    s = lax.dot_general(q_ref[...], k_ref[...], (((1,), (1,)), ((), ())),
                        preferred_element_type=F32) * scale
    m = jnp.max(s, axis=-1, keepdims=True)
    p = jnp.exp(s - m)
    denom = jnp.sum(p, axis=-1, keepdims=True)
    o = jnp.dot(p.astype(BF16), v_ref[...], preferred_element_type=F32)
    o_ref[...] = (o / denom).astype(o_ref.dtype)


def _na_attention(qkv, rpb):
    dh, nh = NA_HEAD_DIM, NA_HEADS
    ctx_blk0 = M_LAT // CTX_LEN
    lat = pl.pallas_call(
        _na_lat_kernel,
        grid=(BATCH, nh),
        in_specs=[
            pl.BlockSpec(memory_space=pltpu.SMEM),
            pl.BlockSpec((SEQ, dh), lambda b, h: (b, h)),
            pl.BlockSpec((SEQ, dh), lambda b, h: (b, nh + h)),
            pl.BlockSpec((SEQ, dh), lambda b, h: (b, 2 * nh + h)),
            pl.BlockSpec((CTX_LEN, dh), lambda b, h: (ctx_blk0 + b, nh + h)),
            pl.BlockSpec((CTX_LEN, dh), lambda b, h: (ctx_blk0 + b, 2 * nh + h)),
        ],
        out_specs=pl.BlockSpec((SEQ, dh), lambda b, h: (b, h)),
        out_shape=jax.ShapeDtypeStruct((M_ALL, D_MODEL), BF16),
        scratch_shapes=[pltpu.VMEM((NA_PAIR_TILES, GRID_W, 2 * GRID_W), F32)],
        compiler_params=_params(2),
        name="na_latent",
    )(rpb.reshape(nh, NA_DR * NA_DC), qkv, qkv, qkv, qkv, qkv)
    ctx = pl.pallas_call(
        _na_ctx_kernel,
        grid=(BATCH, nh),
        in_specs=[
            pl.BlockSpec((CTX_LEN, dh), lambda b, h: (ctx_blk0 + b, h)),
            pl.BlockSpec((CTX_LEN, dh), lambda b, h: (ctx_blk0 + b, nh + h)),
            pl.BlockSpec((CTX_LEN, dh), lambda b, h: (ctx_blk0 + b, 2 * nh + h)),
        ],
        out_specs=pl.BlockSpec((CTX_LEN, dh), lambda b, h: (b, h)),
        out_shape=jax.ShapeDtypeStruct((M_CTX, D_MODEL), BF16),
        compiler_params=_params(2),
        name="na_context",
    )(qkv, qkv, qkv)
    return lax.dynamic_update_slice(lat, ctx, (M_LAT, 0))


RNN_LEN = CTX_LEN + SEQ
SCAN_UNROLL = 8


def _shift_rows(x, offset, length):
    t = lax.broadcasted_iota(jnp.int32, (length, 1), 0)
    rolled = pltpu.roll(x, (-offset) % length, 0)
    return jnp.where((t + offset >= 0) & (t + offset < length), rolled, 0.0)


def _rnn_conv(x, cw, cb):
    length = x.shape[0]
    left = RNN_CONV // 2
    y = cb
    for j in range(RNN_CONV):
        y = y + _shift_rows(x, j - left, length) * cw[j:j + 1]
    return y


def _block_scan(a, b, reverse):
    row = lax.broadcasted_iota(jnp.int32, a.shape, 0)
    for s in (1, 2, 4):
        if reverse:
            keep = row < SUBLANES - s
            shift = SUBLANES - s
        else:
            keep = row >= s
            shift = s
        a_sh = pltpu.roll(a, shift, 0)
        b_sh = pltpu.roll(b, shift, 0)
        b = jnp.where(keep, a * b_sh + b, b)
        a = jnp.where(keep, a * a_sh, a)
    return a, b


def _rglru_kernel(xc_ref, xl_ref, yc_ref, yl_ref, cw_ref, cb_ref, wg_ref, bg_ref, lam_ref,
                  oc_ref, ol_ref, af_ref, bf_ref, ar_ref, br_ref, hf_ref, hr_ref):
    cw = cw_ref[...]
    cb = cb_ref[...]
    xr_c = _rnn_conv(xc_ref[...].astype(F32), cw, cb)
    xr_l = _rnn_conv(xl_ref[...].astype(F32), cw, cb)

    def gates(xr, d, a_ref, b_ref, rows):
        xb = xr.astype(BF16)
        g_r = jnp.dot(xb, wg_ref[2 * d].astype(BF16), preferred_element_type=F32) + bg_ref[2 * d]
        g_i = jnp.dot(xb, wg_ref[2 * d + 1].astype(BF16), preferred_element_type=F32) + bg_ref[2 * d + 1]
        g_r = jax.nn.sigmoid(g_r)
        g_i = jax.nn.sigmoid(g_i)
        neg_lam = -lam_ref[d:d + 1, :]
        softplus = jnp.maximum(neg_lam, 0.0) + jnp.log1p(jnp.exp(-jnp.abs(neg_lam)))
        log_a = -RG_C * g_r * softplus
        a = jnp.exp(log_a)
        a_ref[rows, :] = a
        b_ref[rows, :] = jnp.sqrt(1.0 - a * a) * (g_i * xr)

    ctx_f, lat_f = slice(0, CTX_LEN), slice(CTX_LEN, RNN_LEN)
    lat_r, ctx_r = slice(0, SEQ), slice(SEQ, RNN_LEN)
    gates(xr_c, 0, af_ref, bf_ref, ctx_f)
    gates(xr_l, 0, af_ref, bf_ref, lat_f)
    gates(xr_c, 1, ar_ref, br_ref, ctx_r)
    gates(xr_l, 1, ar_ref, br_ref, lat_r)

    nblk = RNN_LEN // SUBLANES
    last_row = slice(SUBLANES - 1, SUBLANES)
    first_row = slice(0, 1)
    blk_shape = (SUBLANES, RNN_HEAD_DIM)

    def block(n, carry):
        cf, cr = carry
        rows_f = pl.ds(pl.multiple_of(n * SUBLANES, SUBLANES), SUBLANES)
        rows_r = pl.ds(pl.multiple_of((nblk - 1 - n) * SUBLANES, SUBLANES), SUBLANES)
        a_f, b_f = _block_scan(af_ref[rows_f, :], bf_ref[rows_f, :], reverse=False)
        a_r, b_r = _block_scan(ar_ref[rows_r, :], br_ref[rows_r, :], reverse=True)
        hf_ref[rows_f, :] = a_f * cf + b_f
        hr_ref[rows_r, :] = a_r * cr + b_r
        cf = jnp.broadcast_to(a_f[last_row], blk_shape) * cf + jnp.broadcast_to(b_f[last_row], blk_shape)
        cr = jnp.broadcast_to(a_r[first_row], blk_shape) * cr + jnp.broadcast_to(b_r[first_row], blk_shape)
        return cf, cr

    zero = jnp.zeros(blk_shape, F32)
    lax.fori_loop(0, nblk, block, (zero, zero), unroll=SCAN_UNROLL)

    h_c = hf_ref[ctx_f, :] + hr_ref[ctx_r, :]
    h_l = hf_ref[lat_f, :] + hr_ref[lat_r, :]
    oc_ref[...] = (jax.nn.gelu(yc_ref[...].astype(F32)) * h_c).astype(oc_ref.dtype)
    ol_ref[...] = (jax.nn.gelu(yl_ref[...].astype(F32)) * h_l).astype(ol_ref.dtype)


def _rglru(xin, conv_w, conv_b, w_gate, b_gate, lam, j):
    dh, nh = RNN_HEAD_DIM, RNN_HEADS
    n_c = conv_w.shape[0]
    ctx_blk0 = M_LAT // CTX_LEN
    wg = w_gate.reshape(n_c, 4, nh, dh, dh)
    bg = b_gate.reshape(n_c, 4, nh, 1, dh)
    out_c, out_l = pl.pallas_call(
        _rglru_kernel,
        grid=(BATCH, nh),
        in_specs=[
            pl.BlockSpec((CTX_LEN, dh), lambda b, h: (ctx_blk0 + b, nh + h)),
            pl.BlockSpec((SEQ, dh), lambda b, h: (b, nh + h)),
            pl.BlockSpec((CTX_LEN, dh), lambda b, h: (ctx_blk0 + b, h)),
            pl.BlockSpec((SEQ, dh), lambda b, h: (b, h)),
            pl.BlockSpec((None, RNN_CONV, dh), lambda b, h: (j, 0, h)),
            pl.BlockSpec((None, 1, dh), lambda b, h: (j, 0, h)),
            pl.BlockSpec((None, 4, None, dh, dh), lambda b, h: (j, 0, h, 0, 0)),
            pl.BlockSpec((None, 4, None, 1, dh), lambda b, h: (j, 0, h, 0, 0)),
            pl.BlockSpec((None, 2, dh), lambda b, h: (j, 0, h)),
        ],
        out_specs=[
            pl.BlockSpec((CTX_LEN, dh), lambda b, h: (b, h)),
            pl.BlockSpec((SEQ, dh), lambda b, h: (b, h)),
        ],
        out_shape=[
            jax.ShapeDtypeStruct((M_CTX, RNN_WIDTH), BF16),
            jax.ShapeDtypeStruct((M_ALL, RNN_WIDTH), BF16),
        ],
        scratch_shapes=[pltpu.VMEM((RNN_LEN, dh), F32)] * 6,
        compiler_params=_params(2),
        name="rglru",
    )(xin, xin, xin, xin, conv_w, conv_b.reshape(n_c, 1, RNN_WIDTH), wg, bg, lam)
    return lax.dynamic_update_slice(out_l, out_c, (M_LAT, 0))


def kernel(x, c, ctx, c_ctx, ada_w, ada_b, norm_g, ffn_w_up, ffn_conv_w, ffn_conv_b, ffn_w_down,
           a_w_in, a_g_v, a_w_s, a_b_s, a_w_out, b_w_qkv, b_rpb, b_w_out,
           c_w_in, c_conv_w, c_conv_b, c_w_gate, c_b_gate, c_lam, c_w_out, final_g):
    h = jnp.concatenate([x.reshape(M_LAT, D_MODEL), ctx.reshape(M_CTX, D_MODEL)], axis=0)
    cvec = jnp.concatenate(
        [c, c_ctx[None], jnp.zeros((SUBLANES - BATCH - 1, D_MODEL), F32)], axis=0)
    mod = _ada_mod(cvec, ada_w, ada_b)

    for i in range(DEPTH):
        last = i == DEPTH - 1
        kind, j = i % N_MIXERS, i // N_MIXERS
        ffn_rows = M_LAT if last else M_ALL
        n_rows = M_LAT if (last and kind == 0) else M_ALL

        n = _norm_mod(h, norm_g, mod, i, 0, n_rows)
        if kind == 0:
            z = _matmul(n, a_w_in, j, n_rows=n_rows, tn=1024, out_dtype=BF16, act="gelu")
            y = _sgu(z, a_g_v, a_w_s, a_b_s, j, n_rows=n_rows)
            w_out = a_w_out
        elif kind == 1:
            qkv = _matmul(n, b_w_qkv, j, n_rows=n_rows, tn=1024, out_dtype=BF16)
            y = _na_attention(qkv, b_rpb[j])
            w_out = b_w_out
        else:
            xin = _matmul(n, c_w_in, j, n_rows=n_rows, tn=1024, out_dtype=F32)
            y = _rglru(xin, c_conv_w, c_conv_b, c_w_gate, c_b_gate, c_lam, j)
            w_out = c_w_out
        h = _matmul_resid(y, w_out, j, h, mod, i, 2, n_rows=n_rows, tn=512 if kind == 0 else 1024)

        n = _norm_mod(h, norm_g, mod, i, 1, ffn_rows)
        a = _ffn_up(n, ffn_w_up, ffn_conv_w, ffn_conv_b, i, n_rows=ffn_rows)
        h = _matmul_resid(a, ffn_w_down, i, h, mod, i, 5, n_rows=ffn_rows, tn=512)

    return _final_norm(h, final_g).reshape(BATCH, SEQ, D_MODEL)
```

```python
import functools

import jax
import jax.numpy as jnp
from jax import lax
from jax.experimental import pallas as pl
from jax.experimental.pallas import tpu as pltpu

F32 = jnp.float32
BF16 = jnp.bfloat16

D_MODEL = 2048
BATCH = 2
SEQ = 4096
DEPTH = 4
GRID_W = 64
CTX_LEN = 256
N_MIXERS = 3
N_MOD = 6
EPS = 1e-6
NEG_INF = -1e30
D_FF = 5632
CHUNK = 128
A_WIDTH = 2 * D_MODEL
A_GROUPS = 16
NA_HEADS = 16
NA_HEAD_DIM = D_MODEL // NA_HEADS
NA_KH = 8
NA_KW = 16
RNN_WIDTH = D_MODEL
RNN_HEADS = 16
RNN_HEAD_DIM = RNN_WIDTH // RNN_HEADS
RNN_CONV = 4
RG_C = 8.0

M_LAT = BATCH * SEQ
M_CTX = BATCH * CTX_LEN
M_ALL = M_LAT + M_CTX
GRID_ROWS = SEQ // GRID_W

VMEM_LIMIT_BYTES = 56 * 1024 * 1024
SUBLANES = 8
LANES = 128
BF16_ROWS = 16
MXU_WIDTH = 256

ROW_TILE = {M_ALL: M_ALL // 8, M_LAT: M_LAT // 8}
ROW_TILE_WIDE_K = {M_ALL: M_ALL // 16, M_LAT: M_LAT // 16}
NORM_ROW_TILE = 512


def _params(n_axes, flags=None):
    return pltpu.CompilerParams(
        dimension_semantics=("arbitrary",) * n_axes, vmem_limit_bytes=VMEM_LIMIT_BYTES, flags=flags)


def _sigmoid(x):
    return 0.5 * jnp.tanh(0.5 * x) + 0.5


def _mod_spec(layer, which, tn=D_MODEL, grid_rank=1):
    if grid_rank == 1:
        return pl.BlockSpec((None, SUBLANES, tn), lambda i: (layer * N_MOD + which, 0, 0))
    return pl.BlockSpec((None, SUBLANES, tn), lambda j, i: (layer * N_MOD + which, 0, j))


def _rows_of_group(mod_ref, row0, tm):
    r = row0 + lax.broadcasted_iota(jnp.int32, (tm, 1), 0)
    m = mod_ref[...]
    out = m[BATCH:BATCH + 1]
    for b in reversed(range(BATCH)):
        out = jnp.where(r < (b + 1) * SEQ, m[b:b + 1], out)
    return out


def _ada_kernel(c_ref, w_ref, b_ref, o_ref):
    c = c_ref[...]
    s = (c * _sigmoid(c)).astype(BF16)
    o_ref[...] = jnp.dot(s, w_ref[...].astype(BF16), preferred_element_type=F32) + b_ref[...]


def _ada_mod(cvec, ada_w, ada_b):
    depth, d, n = ada_w.shape
    tn = 1024
    out = pl.pallas_call(
        _ada_kernel,
        grid=(depth, n // tn),
        in_specs=[
            pl.BlockSpec((SUBLANES, d), lambda l, j: (0, 0)),
            pl.BlockSpec((None, d, tn), lambda l, j: (l, 0, j)),
            pl.BlockSpec((None, 1, tn), lambda l, j: (l, 0, j)),
        ],
        out_specs=pl.BlockSpec((None, SUBLANES, tn), lambda l, j: (l, 0, j)),
        out_shape=jax.ShapeDtypeStruct((depth, SUBLANES, n), F32),
        compiler_params=_params(2),
        name="ada_mod",
    )(cvec, ada_w, ada_b.reshape(depth, 1, n))
    return out.reshape(depth, SUBLANES, N_MOD, d).transpose(0, 2, 1, 3).reshape(depth * N_MOD, SUBLANES, d)


def _norm_mod_kernel(h_ref, g_ref, shift_ref, scale_ref, o_ref):
    group = pl.ds(jnp.minimum(pl.program_id(0) * h_ref.shape[0] // SEQ, BATCH), 1)
    x = h_ref[...]
    y = x * lax.rsqrt(jnp.mean(x * x, axis=-1, keepdims=True) + EPS)
    y = y * g_ref[...]
    y = y * (1 + scale_ref[group, :]) + shift_ref[group, :]
    o_ref[...] = y.astype(o_ref.dtype)


def _norm_mod(h, norm_g, mod, layer, which_norm, n_rows):
    tm = NORM_ROW_TILE
    g = norm_g.reshape(DEPTH * 2, 1, D_MODEL)
    return pl.pallas_call(
        _norm_mod_kernel,
        grid=(n_rows // tm,),
        in_specs=[
            pl.BlockSpec((tm, D_MODEL), lambda i: (i, 0)),
            pl.BlockSpec((None, 1, D_MODEL), lambda i: (layer * 2 + which_norm, 0, 0)),
            _mod_spec(layer, 3 * which_norm),
            _mod_spec(layer, 3 * which_norm + 1),
        ],
        out_specs=pl.BlockSpec((tm, D_MODEL), lambda i: (i, 0)),
        out_shape=jax.ShapeDtypeStruct((M_ALL, D_MODEL), BF16),
        compiler_params=_params(1),
        name="norm_mod",
    )(h, g, mod, mod)


def _final_norm_kernel(h_ref, g_ref, o_ref):
    x = h_ref[...]
    y = x * lax.rsqrt(jnp.mean(x * x, axis=-1, keepdims=True) + EPS)
    o_ref[...] = y * g_ref[...]


def _final_norm(h, final_g):
    tm = NORM_ROW_TILE
    return pl.pallas_call(
        _final_norm_kernel,
        grid=(M_LAT // tm,),
        in_specs=[
            pl.BlockSpec((tm, D_MODEL), lambda i: (i, 0)),
            pl.BlockSpec((1, D_MODEL), lambda i: (0, 0)),
        ],
        out_specs=pl.BlockSpec((tm, D_MODEL), lambda i: (i, 0)),
        out_shape=jax.ShapeDtypeStruct((M_LAT, D_MODEL), F32),
        compiler_params=_params(1),
        name="final_norm",
    )(h, final_g.reshape(1, D_MODEL))


def _mm_kernel(x_ref, w_ref, o_ref, wbf_ref, *, act):
    @pl.when(pl.program_id(1) == 0)
    def _():
        wbf_ref[...] = w_ref[...].astype(BF16)

    acc = jnp.dot(x_ref[...], wbf_ref[...], preferred_element_type=F32)
    if act == "gelu":
        acc = jax.nn.gelu(acc)
    o_ref[...] = acc.astype(o_ref.dtype)


def _mm_resid_kernel(x_ref, w_ref, gate_ref, res_ref, o_ref, wbf_ref):
    i = pl.program_id(1)
    tm = x_ref.shape[0]

    @pl.when(i == 0)
    def _():
        wbf_ref[...] = w_ref[...].astype(BF16)

    acc = jnp.dot(x_ref[...], wbf_ref[...], preferred_element_type=F32)
    o_ref[...] = res_ref[...] + _rows_of_group(gate_ref, i * tm, tm) * acc


def _matmul(x, w, w_idx, *, n_rows, tn, out_dtype, act=None):
    tm = ROW_TILE[n_rows]
    _, k, n = w.shape
    return pl.pallas_call(
        functools.partial(_mm_kernel, act=act),
        grid=(n // tn, n_rows // tm),
        in_specs=[
            pl.BlockSpec((tm, k), lambda j, i: (i, 0)),
            pl.BlockSpec((None, k, tn), lambda j, i: (w_idx, 0, j)),
        ],
        out_specs=pl.BlockSpec((tm, tn), lambda j, i: (i, j)),
        out_shape=jax.ShapeDtypeStruct((x.shape[0], n), out_dtype),
        scratch_shapes=[pltpu.VMEM((k, tn), BF16)],
        compiler_params=_params(2),
        name="matmul",
    )(x, w)


def _matmul_resid(x, w, w_idx, h, mod, layer, which_gate, *, n_rows, tn):
    _, k, n = w.shape
    tm = (ROW_TILE_WIDE_K if k > 2 * D_MODEL else ROW_TILE)[n_rows]
    return pl.pallas_call(
        _mm_resid_kernel,
        grid=(n // tn, n_rows // tm),
        in_specs=[
            pl.BlockSpec((tm, k), lambda j, i: (i, 0)),
            pl.BlockSpec((None, k, tn), lambda j, i: (w_idx, 0, j)),
            _mod_spec(layer, which_gate, tn, grid_rank=2),
            pl.BlockSpec((tm, tn), lambda j, i: (i, j)),
        ],
        out_specs=pl.BlockSpec((tm, tn), lambda j, i: (i, j)),
        out_shape=jax.ShapeDtypeStruct(h.shape, F32),
        scratch_shapes=[pltpu.VMEM((k, tn), BF16)],
        input_output_aliases={3: 0},
        compiler_params=_params(2),
        name="matmul_resid",
    )(x, w, mod, h)


HALO = BF16_ROWS


SEQ_STARTS = tuple(b * SEQ for b in range(BATCH)) + tuple(M_LAT + b * CTX_LEN for b in range(BATCH))
FIX_ROWS = BF16_ROWS


def _is_seq_edge(row):
    edge = row == M_ALL
    for s in SEQ_STARTS:
        edge = edge | (row == s)
    return edge


def _conv3(prev, cur, nxt, cw_ref, cb_ref):
    cw = cw_ref[...]
    y = cb_ref[...] + prev * cw[0:1]
    y = y + cur * cw[1:2]
    return y + nxt * cw[2:3]


def _silu_gate(g, v):
    half_g = 0.5 * g
    return (half_g * jnp.tanh(half_g) + half_g) * v


def _ffn_up_kernel(x_ref, xp_ref, xn_ref, wg_ref, wv_ref, cwg_ref, cwv_ref, cbg_ref, cbv_ref,
                   o_ref, wg_bf, wv_bf, xs_ref, zg_ref, zv_ref):
    i = pl.program_id(1)
    tm = x_ref.shape[0]
    row0 = i * tm

    @pl.when(i == 0)
    def _():
        wg_bf[...] = wg_ref[...].astype(BF16)
        wv_bf[...] = wv_ref[...].astype(BF16)

    no_rows = jnp.zeros_like(xp_ref)
    xs_ref[0:HALO, :] = jnp.where(_is_seq_edge(row0), no_rows, xp_ref[...])
    xs_ref[HALO:HALO + tm, :] = x_ref[...]
    xs_ref[HALO + tm:, :] = jnp.where(_is_seq_edge(row0 + tm), no_rows, xn_ref[...])
    xs = xs_ref[...]
    zg_ref[...] = jnp.dot(xs, wg_bf[...], preferred_element_type=F32)
    zv_ref[...] = jnp.dot(xs, wv_bf[...], preferred_element_type=F32)

    def conv(z_ref, cw_ref, cb_ref):
        return _conv3(z_ref[HALO - 1:HALO - 1 + tm, :], z_ref[HALO:HALO + tm, :],
                      z_ref[HALO + 1:HALO + 1 + tm, :], cw_ref, cb_ref)

    o_ref[...] = _silu_gate(conv(zg_ref, cwg_ref, cbg_ref), conv(zv_ref, cwv_ref, cbv_ref)).astype(o_ref.dtype)

    n_win = 2 * FIX_ROWS + 2 * SUBLANES
    w_row = lax.broadcasted_iota(jnp.int32, (n_win, 1), 0)
    start_row = SUBLANES + FIX_ROWS

    def fixed_conv(z_ref, cw_ref, cb_ref, win):
        z = z_ref[win, :]
        prev = jnp.where(w_row == start_row, 0.0, pltpu.roll(z, 1, 0))
        nxt = jnp.where(w_row == start_row - 1, 0.0, pltpu.roll(z, n_win - 1, 0))
        return _conv3(prev, z, nxt, cw_ref, cb_ref)[SUBLANES:SUBLANES + 2 * FIX_ROWS]

    for s in SEQ_STARTS[1:]:
        @pl.when((s > row0) & (s < row0 + tm))
        def _():
            win = pl.ds(pl.multiple_of(HALO + s - row0 - start_row, SUBLANES), n_win)
            rows = pl.ds(pl.multiple_of(s - row0 - FIX_ROWS, FIX_ROWS), 2 * FIX_ROWS)
            g = fixed_conv(zg_ref, cwg_ref, cbg_ref, win)
            v = fixed_conv(zv_ref, cwv_ref, cbv_ref, win)
            o_ref[rows, :] = _silu_gate(g, v).astype(o_ref.dtype)


def _ffn_up(n, w_up, conv_w, conv_b, layer, *, n_rows):
    tm, tn = ROW_TILE[n_rows], 512
    k = w_up.shape[1]
    nj = D_FF // tn
    halo_per_tile = tm // HALO
    last_halo = M_ALL // HALO - 1
    cb = conv_b.reshape(DEPTH, 1, 2 * D_FF)
    return pl.pallas_call(
        _ffn_up_kernel,
        grid=(nj, n_rows // tm),
        in_specs=[
            pl.BlockSpec((tm, k), lambda j, i: (i, 0)),
            pl.BlockSpec((HALO, k), lambda j, i: (jnp.maximum(i * halo_per_tile - 1, 0), 0)),
            pl.BlockSpec((HALO, k), lambda j, i: (jnp.minimum((i + 1) * halo_per_tile, last_halo), 0)),
            pl.BlockSpec((None, k, tn), lambda j, i: (layer, 0, j)),
            pl.BlockSpec((None, k, tn), lambda j, i: (layer, 0, j + nj)),
            pl.BlockSpec((None, 3, tn), lambda j, i: (layer, 0, j)),
            pl.BlockSpec((None, 3, tn), lambda j, i: (layer, 0, j + nj)),
            pl.BlockSpec((None, 1, tn), lambda j, i: (layer, 0, j)),
            pl.BlockSpec((None, 1, tn), lambda j, i: (layer, 0, j + nj)),
        ],
        out_specs=pl.BlockSpec((tm, tn), lambda j, i: (i, j)),
        out_shape=jax.ShapeDtypeStruct((M_ALL, D_FF), BF16),
        scratch_shapes=[
            pltpu.VMEM((k, tn), BF16),
            pltpu.VMEM((k, tn), BF16),
            pltpu.VMEM((tm + 2 * HALO, k), BF16),
            pltpu.VMEM((tm + 2 * HALO, tn), F32),
            pltpu.VMEM((tm + 2 * HALO, tn), F32),
        ],
        compiler_params=_params(2),
        name="ffn_up_conv_gate",
    )(n, n, n, w_up, w_up, conv_w, conv_w, cb, cb)


def _sgu_kernel(u_ref, v_ref, gv_ref, ws_ref, bs_ref, o_ref):
    tm = u_ref.shape[0]
    gw = A_WIDTH // A_GROUPS
    gv = gv_ref[...]
    for c in range(tm // CHUNK):
        rows = slice(c * CHUNK, (c + 1) * CHUNK)
        v = v_ref[rows, :].astype(F32)
        vn = v * lax.rsqrt(jnp.mean(v * v, axis=-1, keepdims=True) + EPS) * gv
        vn = vn.astype(BF16)
        for g in range(A_GROUPS):
            cols = slice(g * gw, (g + 1) * gw)
            s = jnp.dot(ws_ref[g].astype(BF16), vn[:, cols], preferred_element_type=F32)
            s = s + bs_ref[:, g:g + 1]
            o_ref[rows, cols] = (u_ref[rows, cols].astype(F32) * s).astype(o_ref.dtype)


def _sgu(z, g_v, w_s, b_s, j, *, n_rows):
    tm = 2 * CHUNK
    n_a = g_v.shape[0]
    return pl.pallas_call(
        _sgu_kernel,
        grid=(n_rows // tm,),
        in_specs=[
            pl.BlockSpec((tm, A_WIDTH), lambda i: (i, 0)),
            pl.BlockSpec((tm, A_WIDTH), lambda i: (i, 1)),
            pl.BlockSpec((None, 1, A_WIDTH), lambda i: (j, 0, 0)),
            pl.BlockSpec((None, A_GROUPS, CHUNK, CHUNK), lambda i: (j, 0, 0, 0)),
            pl.BlockSpec((None, CHUNK, A_GROUPS), lambda i: (j, 0, 0)),
        ],
        out_specs=pl.BlockSpec((tm, A_WIDTH), lambda i: (i, 0)),
        out_shape=jax.ShapeDtypeStruct((M_ALL, A_WIDTH), BF16),
        compiler_params=_params(1),
        name="sgu",
    )(z, z, g_v.reshape(n_a, 1, A_WIDTH), w_s, b_s.transpose(0, 2, 1))


NA_DR = 2 * NA_KH - 1
NA_DC = 2 * NA_KW - 1
NA_PAIR_TILES = NA_DR + 1
NA_QROWS = 4
NA_KROWS = NA_KH + NA_QROWS
NA_BLOCK_UNROLL = 2


def _na_bias_pairs(rpb_ref, head, pair_ref):
    shape = (GRID_W, 2 * GRID_W)
    q = lax.broadcasted_iota(jnp.int32, shape, 0)
    lane = lax.broadcasted_iota(jnp.int32, shape, 1)
    kcol = lane & (GRID_W - 1)
    upper = lane >= GRID_W
    dc = jnp.clip(kcol - q, -(NA_KW - 1), NA_KW - 1) + NA_KW - 1
    c0 = jnp.clip(q - NA_KW // 2, 0, GRID_W - NA_KW)
    ok = (kcol >= c0) & (kcol < c0 + NA_KW)
    for d in range(-1, NA_DR):
        acc = jnp.full(shape, NEG_INF, F32)
        for c in range(NA_DC):
            lo = rpb_ref[head, d * NA_DC + c] if d >= 0 else NEG_INF
            hi = rpb_ref[head, (d + 1) * NA_DC + c] if d + 1 < NA_DR else NEG_INF
            acc = jnp.where(dc == c, jnp.where(upper, hi, lo), acc)
        pair_ref[d + 1] = jnp.where(ok, acc, NEG_INF)


def _na_window_start(m):
    return min(max(m * NA_QROWS - NA_KH // 2, 0), GRID_ROWS - NA_KROWS)


def _na_block_bias(pair_ref, m):
    upper = lax.broadcasted_iota(jnp.int32, (GRID_W, 2 * GRID_W), 1) >= GRID_W
    masked = jnp.full((GRID_W, 2 * GRID_W), NEG_INF, F32)
    ws = _na_window_start(m)
    rows = []
    for a in range(NA_QROWS):
        r = m * NA_QROWS + a
        r0 = min(max(r - NA_KH // 2, 0), GRID_ROWS - NA_KH)
        tiles = []
        for p in range(NA_KROWS // 2):
            key_row = ws + 2 * p
            in_lo = r0 <= key_row < r0 + NA_KH
            in_hi = r0 <= key_row + 1 < r0 + NA_KH
            if not (in_lo or in_hi):
                tiles.append(masked)
                continue
            tile = pair_ref[key_row - r + NA_KH]
            if not in_hi:
                tile = jnp.where(upper, NEG_INF, tile)
            elif not in_lo:
                tile = jnp.where(upper, tile, NEG_INF)
            tiles.append(tile)
        rows.append(jnp.concatenate(tiles, axis=1))
    return jnp.concatenate(rows, axis=0)


def _na_lat_kernel(rpb_ref, q_ref, k_ref, v_ref, kc_ref, vc_ref, o_ref, pair_ref, bias_ref):
    scale = NA_HEAD_DIM ** -0.5
    _na_bias_pairs(rpb_ref, pl.program_id(1), pair_ref)
    kc = kc_ref[...]
    vc = vc_ref[...]
    nt = (((1,), (1,)), ((), ()))
    n_q = NA_QROWS * GRID_W
    n_blocks = GRID_ROWS // NA_QROWS

    def attend(qrows, win, bias):
        q = q_ref[qrows, :]
        kw = k_ref[win, :]
        vw = v_ref[win, :]
        s = lax.dot_general(q, kw, nt, preferred_element_type=F32) * scale
        s = jnp.where(bias > 0.5 * NEG_INF, s + bias, NEG_INF)
        sc = lax.dot_general(q, kc, nt, preferred_element_type=F32) * scale
        mx = jnp.maximum(jnp.max(s, axis=-1, keepdims=True), jnp.max(sc, axis=-1, keepdims=True))
        p_w = jnp.exp(s - mx)
        p_c = jnp.exp(sc - mx)
        denom = jnp.sum(p_w, axis=-1, keepdims=True) + jnp.sum(p_c, axis=-1, keepdims=True)
        o = jnp.dot(p_w.astype(BF16), vw, preferred_element_type=F32)
        o = o + jnp.dot(p_c.astype(BF16), vc, preferred_element_type=F32)
        o_ref[qrows, :] = (o / denom).astype(o_ref.dtype)

    def edge_block(m):
        ws = _na_window_start(m)
        attend(slice(m * n_q, (m + 1) * n_q), slice(ws * GRID_W, (ws + NA_KROWS) * GRID_W),
               _na_block_bias(pair_ref, m))

    bias_ref[...] = _na_block_bias(pair_ref, 1)

    def interior_block(m, carry):
        qrows = pl.ds(pl.multiple_of(m * n_q, n_q), n_q)
        ws = m * NA_QROWS - NA_KH // 2
        win = pl.ds(pl.multiple_of(ws * GRID_W, GRID_W), NA_KROWS * GRID_W)
        attend(qrows, win, bias_ref[...])
        return carry

    edge_block(0)
    lax.fori_loop(1, n_blocks - 1, interior_block, 0, unroll=NA_BLOCK_UNROLL)
    edge_block(n_blocks - 1)


def _na_ctx_kernel(q_ref, k_ref, v_ref, o_ref):
    scale = NA_HEAD_DIM ** -0.5
    s = lax.dot_general(q_ref[...], k_ref[...], (((1,), (1,)), ((), ())),
                        preferred_element_type=F32) * scale
    m = jnp.max(s, axis=-1, keepdims=True)
    p = jnp.exp(s - m)
    denom = jnp.sum(p, axis=-1, keepdims=True)
    o = jnp.dot(p.astype(BF16), v_ref[...], preferred_element_type=F32)
    o_ref[...] = (o / denom).astype(o_ref.dtype)


def _na_attention(qkv, rpb):
    dh, nh = NA_HEAD_DIM, NA_HEADS
    ctx_blk0 = M_LAT // CTX_LEN
    lat = pl.pallas_call(
        _na_lat_kernel,
        grid=(BATCH, nh),
        in_specs=[
            pl.BlockSpec(memory_space=pltpu.SMEM),
            pl.BlockSpec((SEQ, dh), lambda b, h: (b, h)),
            pl.BlockSpec((SEQ, dh), lambda b, h: (b, nh + h)),
            pl.BlockSpec((SEQ, dh), lambda b, h: (b, 2 * nh + h)),
            pl.BlockSpec((CTX_LEN, dh), lambda b, h: (ctx_blk0 + b, nh + h)),
            pl.BlockSpec((CTX_LEN, dh), lambda b, h: (ctx_blk0 + b, 2 * nh + h)),
        ],
        out_specs=pl.BlockSpec((SEQ, dh), lambda b, h: (b, h)),
        out_shape=jax.ShapeDtypeStruct((M_ALL, D_MODEL), BF16),
        scratch_shapes=[
            pltpu.VMEM((NA_PAIR_TILES, GRID_W, 2 * GRID_W), F32),
            pltpu.VMEM((NA_QROWS * GRID_W, NA_KROWS * GRID_W), F32),
        ],
        compiler_params=_params(2),
        name="na_latent",
    )(rpb.reshape(nh, NA_DR * NA_DC), qkv, qkv, qkv, qkv, qkv)
    ctx = pl.pallas_call(
        _na_ctx_kernel,
        grid=(BATCH, nh),
        in_specs=[
            pl.BlockSpec((CTX_LEN, dh), lambda b, h: (ctx_blk0 + b, h)),
            pl.BlockSpec((CTX_LEN, dh), lambda b, h: (ctx_blk0 + b, nh + h)),
            pl.BlockSpec((CTX_LEN, dh), lambda b, h: (ctx_blk0 + b, 2 * nh + h)),
        ],
        out_specs=pl.BlockSpec((CTX_LEN, dh), lambda b, h: (b, h)),
        out_shape=jax.ShapeDtypeStruct((M_CTX, D_MODEL), BF16),
        compiler_params=_params(2),
        name="na_context",
    )(qkv, qkv, qkv)
    return lax.dynamic_update_slice(lat, ctx, (M_LAT, 0))


RNN_LEN = CTX_LEN + SEQ
SCAN_UNROLL = 8


def _shift_rows(x, offset, length):
    t = lax.broadcasted_iota(jnp.int32, (length, 1), 0)
    rolled = pltpu.roll(x, (-offset) % length, 0)
    return jnp.where((t + offset >= 0) & (t + offset < length), rolled, 0.0)


def _rnn_conv(x, cw, cb):
    length = x.shape[0]
    left = RNN_CONV // 2
    y = cb
    for j in range(RNN_CONV):
        y = y + _shift_rows(x, j - left, length) * cw[j:j + 1]
    return y


def _block_scan(a, b, reverse):
    row = lax.broadcasted_iota(jnp.int32, a.shape, 0)
    for s in (1, 2, 4):
        if reverse:
            keep = row < SUBLANES - s
            shift = SUBLANES - s
        else:
            keep = row >= s
            shift = s
        a_sh = pltpu.roll(a, shift, 0)
        b_sh = pltpu.roll(b, shift, 0)
        b = jnp.where(keep, a * b_sh + b, b)
        a = jnp.where(keep, a * a_sh, a)
    return a, b


def _rglru_kernel(xc_ref, xl_ref, yc_ref, yl_ref, cw_ref, cb_ref, wg_ref, bg_ref, lam_ref,
                  oc_ref, ol_ref, af_ref, bf_ref, ar_ref, br_ref, hf_ref, hr_ref):
    cw = cw_ref[...]
    cb = cb_ref[...]
    xr_c = _rnn_conv(xc_ref[...].astype(F32), cw, cb)
    xr_l = _rnn_conv(xl_ref[...].astype(F32), cw, cb)

    def gates(xr, d, a_ref, b_ref, rows):
        xb = xr.astype(BF16)
        g_r = jnp.dot(xb, wg_ref[2 * d].astype(BF16), preferred_element_type=F32) + bg_ref[2 * d]
        g_i = jnp.dot(xb, wg_ref[2 * d + 1].astype(BF16), preferred_element_type=F32) + bg_ref[2 * d + 1]
        g_r = _sigmoid(g_r)
        g_i = _sigmoid(g_i)
        neg_lam = -lam_ref[d:d + 1, :]
        softplus = jnp.maximum(neg_lam, 0.0) + jnp.log1p(jnp.exp(-jnp.abs(neg_lam)))
        log_a = -RG_C * g_r * softplus
        a = jnp.exp(log_a)
        a_ref[rows, :] = a
        b_ref[rows, :] = jnp.sqrt(1.0 - a * a) * (g_i * xr)

    ctx_f, lat_f = slice(0, CTX_LEN), slice(CTX_LEN, RNN_LEN)
    lat_r, ctx_r = slice(0, SEQ), slice(SEQ, RNN_LEN)
    gates(xr_c, 0, af_ref, bf_ref, ctx_f)
    gates(xr_l, 0, af_ref, bf_ref, lat_f)
    gates(xr_c, 1, ar_ref, br_ref, ctx_r)
    gates(xr_l, 1, ar_ref, br_ref, lat_r)

    nblk = RNN_LEN // SUBLANES
    last_row = slice(SUBLANES - 1, SUBLANES)
    first_row = slice(0, 1)
    blk_shape = (SUBLANES, RNN_HEAD_DIM)

    def block(n, carry):
        cf, cr = carry
        rows_f = pl.ds(pl.multiple_of(n * SUBLANES, SUBLANES), SUBLANES)
        rows_r = pl.ds(pl.multiple_of((nblk - 1 - n) * SUBLANES, SUBLANES), SUBLANES)
        a_f, b_f = _block_scan(af_ref[rows_f, :], bf_ref[rows_f, :], reverse=False)
        a_r, b_r = _block_scan(ar_ref[rows_r, :], br_ref[rows_r, :], reverse=True)
        hf_ref[rows_f, :] = a_f * cf + b_f
        hr_ref[rows_r, :] = a_r * cr + b_r
        cf = jnp.broadcast_to(a_f[last_row], blk_shape) * cf + jnp.broadcast_to(b_f[last_row], blk_shape)
        cr = jnp.broadcast_to(a_r[first_row], blk_shape) * cr + jnp.broadcast_to(b_r[first_row], blk_shape)
        return cf, cr

    zero = jnp.zeros(blk_shape, F32)
    lax.fori_loop(0, nblk, block, (zero, zero), unroll=SCAN_UNROLL)

    h_c = hf_ref[ctx_f, :] + hr_ref[ctx_r, :]
    h_l = hf_ref[lat_f, :] + hr_ref[lat_r, :]
    oc_ref[...] = (jax.nn.gelu(yc_ref[...].astype(F32)) * h_c).astype(oc_ref.dtype)
    ol_ref[...] = (jax.nn.gelu(yl_ref[...].astype(F32)) * h_l).astype(ol_ref.dtype)


def _rglru(xin, conv_w, conv_b, w_gate, b_gate, lam, j):
    dh, nh = RNN_HEAD_DIM, RNN_HEADS
    n_c = conv_w.shape[0]
    ctx_blk0 = M_LAT // CTX_LEN
    wg = w_gate.reshape(n_c, 4, nh, dh, dh)
    bg = b_gate.reshape(n_c, 4, nh, 1, dh)
    out_c, out_l = pl.pallas_call(
        _rglru_kernel,
        grid=(BATCH, nh),
        in_specs=[
            pl.BlockSpec((CTX_LEN, dh), lambda b, h: (ctx_blk0 + b, nh + h)),
            pl.BlockSpec((SEQ, dh), lambda b, h: (b, nh + h)),
            pl.BlockSpec((CTX_LEN, dh), lambda b, h: (ctx_blk0 + b, h)),
            pl.BlockSpec((SEQ, dh), lambda b, h: (b, h)),
            pl.BlockSpec((None, RNN_CONV, dh), lambda b, h: (j, 0, h)),
            pl.BlockSpec((None, 1, dh), lambda b, h: (j, 0, h)),
            pl.BlockSpec((None, 4, None, dh, dh), lambda b, h: (j, 0, h, 0, 0)),
            pl.BlockSpec((None, 4, None, 1, dh), lambda b, h: (j, 0, h, 0, 0)),
            pl.BlockSpec((None, 2, dh), lambda b, h: (j, 0, h)),
        ],
        out_specs=[
            pl.BlockSpec((CTX_LEN, dh), lambda b, h: (b, h)),
            pl.BlockSpec((SEQ, dh), lambda b, h: (b, h)),
        ],
        out_shape=[
            jax.ShapeDtypeStruct((M_CTX, RNN_WIDTH), BF16),
            jax.ShapeDtypeStruct((M_ALL, RNN_WIDTH), BF16),
        ],
        scratch_shapes=[pltpu.VMEM((RNN_LEN, dh), F32)] * 6,
        compiler_params=_params(2),
        name="rglru",
    )(xin, xin, xin, xin, conv_w, conv_b.reshape(n_c, 1, RNN_WIDTH), wg, bg, lam)
    return lax.dynamic_update_slice(out_l, out_c, (M_LAT, 0))


def kernel(x, c, ctx, c_ctx, ada_w, ada_b, norm_g, ffn_w_up, ffn_conv_w, ffn_conv_b, ffn_w_down,
           a_w_in, a_g_v, a_w_s, a_b_s, a_w_out, b_w_qkv, b_rpb, b_w_out,
           c_w_in, c_conv_w, c_conv_b, c_w_gate, c_b_gate, c_lam, c_w_out, final_g):
    h = jnp.concatenate([x.reshape(M_LAT, D_MODEL), ctx.reshape(M_CTX, D_MODEL)], axis=0)
    cvec = jnp.concatenate(
        [c, c_ctx[None], jnp.zeros((SUBLANES - BATCH - 1, D_MODEL), F32)], axis=0)
    mod = _ada_mod(cvec, ada_w, ada_b)

    for i in range(DEPTH):
        last = i == DEPTH - 1
        kind, j = i % N_MIXERS, i // N_MIXERS
        ffn_rows = M_LAT if last else M_ALL
        n_rows = M_LAT if (last and kind == 0) else M_ALL

        n = _norm_mod(h, norm_g, mod, i, 0, n_rows)
        if kind == 0:
            z = _matmul(n, a_w_in, j, n_rows=n_rows, tn=1024, out_dtype=BF16, act="gelu")
            y = _sgu(z, a_g_v, a_w_s, a_b_s, j, n_rows=n_rows)
            w_out = a_w_out
        elif kind == 1:
            qkv = _matmul(n, b_w_qkv, j, n_rows=n_rows, tn=1024, out_dtype=BF16)
            y = _na_attention(qkv, b_rpb[j])
            w_out = b_w_out
        else:
            xin = _matmul(n, c_w_in, j, n_rows=n_rows, tn=1024, out_dtype=F32)
            y = _rglru(xin, c_conv_w, c_conv_b, c_w_gate, c_b_gate, c_lam, j)
            w_out = c_w_out
        h = _matmul_resid(y, w_out, j, h, mod, i, 2, n_rows=n_rows, tn=512 if kind == 0 else 1024)

        n = _norm_mod(h, norm_g, mod, i, 1, ffn_rows)
        a = _ffn_up(n, ffn_w_up, ffn_conv_w, ffn_conv_b, i, n_rows=ffn_rows)
        h = _matmul_resid(a, ffn_w_down, i, h, mod, i, 5, n_rows=ffn_rows, tn=512)

    return _final_norm(h, final_g).reshape(BATCH, SEQ, D_MODEL)
```

```python
import functools

import jax
import jax.numpy as jnp
from jax import lax
from jax.experimental import pallas as pl
from jax.experimental.pallas import tpu as pltpu

F32 = jnp.float32
BF16 = jnp.bfloat16

D_MODEL = 2048
BATCH = 2
SEQ = 4096
DEPTH = 4
GRID_W = 64
CTX_LEN = 256
N_MIXERS = 3
N_MOD = 6
EPS = 1e-6
NEG_INF = -1e30
LOG2_E = 1.4426950408889634
D_FF = 5632
CHUNK = 128
A_WIDTH = 2 * D_MODEL
A_GROUPS = 16
NA_HEADS = 16
NA_HEAD_DIM = D_MODEL // NA_HEADS
NA_KH = 8
NA_KW = 16
RNN_WIDTH = D_MODEL
RNN_HEADS = 16
RNN_HEAD_DIM = RNN_WIDTH // RNN_HEADS
RNN_CONV = 4
RG_C = 8.0

M_LAT = BATCH * SEQ
M_CTX = BATCH * CTX_LEN
M_ALL = M_LAT + M_CTX
GRID_ROWS = SEQ // GRID_W

VMEM_LIMIT_BYTES = 56 * 1024 * 1024
SUBLANES = 8
LANES = 128
BF16_ROWS = 16
MXU_WIDTH = 256

ROW_TILE = {M_ALL: M_ALL // 8, M_LAT: M_LAT // 8}
ROW_TILE_WIDE_K = {M_ALL: M_ALL // 16, M_LAT: M_LAT // 16}
NORM_ROW_TILE = 512


def _params(n_axes, flags=None):
    return pltpu.CompilerParams(
        dimension_semantics=("arbitrary",) * n_axes, vmem_limit_bytes=VMEM_LIMIT_BYTES, flags=flags)


def _sigmoid(x):
    return 0.5 * jnp.tanh(0.5 * x) + 0.5


def _mod_spec(layer, which, tn=D_MODEL, grid_rank=1):
    if grid_rank == 1:
        return pl.BlockSpec((None, SUBLANES, tn), lambda i: (layer * N_MOD + which, 0, 0))
    return pl.BlockSpec((None, SUBLANES, tn), lambda j, i: (layer * N_MOD + which, 0, j))


def _rows_of_group(mod_ref, row0, tm):
    r = row0 + lax.broadcasted_iota(jnp.int32, (tm, 1), 0)
    m = mod_ref[...]
    out = m[BATCH:BATCH + 1]
    for b in reversed(range(BATCH)):
        out = jnp.where(r < (b + 1) * SEQ, m[b:b + 1], out)
    return out


def _ada_kernel(c_ref, w_ref, b_ref, o_ref):
    c = c_ref[...]
    s = (c * _sigmoid(c)).astype(BF16)
    o_ref[...] = jnp.dot(s, w_ref[...].astype(BF16), preferred_element_type=F32) + b_ref[...]


def _ada_mod(cvec, ada_w, ada_b):
    depth, d, n = ada_w.shape
    tn = 1024
    out = pl.pallas_call(
        _ada_kernel,
        grid=(depth, n // tn),
        in_specs=[
            pl.BlockSpec((SUBLANES, d), lambda l, j: (0, 0)),
            pl.BlockSpec((None, d, tn), lambda l, j: (l, 0, j)),
            pl.BlockSpec((None, 1, tn), lambda l, j: (l, 0, j)),
        ],
        out_specs=pl.BlockSpec((None, SUBLANES, tn), lambda l, j: (l, 0, j)),
        out_shape=jax.ShapeDtypeStruct((depth, SUBLANES, n), F32),
        compiler_params=_params(2),
        name="ada_mod",
    )(cvec, ada_w, ada_b.reshape(depth, 1, n))
    return out.reshape(depth, SUBLANES, N_MOD, d).transpose(0, 2, 1, 3).reshape(depth * N_MOD, SUBLANES, d)


def _norm_mod_kernel(h_ref, g_ref, shift_ref, scale_ref, o_ref):
    group = pl.ds(jnp.minimum(pl.program_id(0) * h_ref.shape[0] // SEQ, BATCH), 1)
    x = h_ref[...]
    y = x * lax.rsqrt(jnp.mean(x * x, axis=-1, keepdims=True) + EPS)
    y = y * g_ref[...]
    y = y * (1 + scale_ref[group, :]) + shift_ref[group, :]
    o_ref[...] = y.astype(o_ref.dtype)


def _norm_mod(h, norm_g, mod, layer, which_norm, n_rows):
    tm = NORM_ROW_TILE
    g = norm_g.reshape(DEPTH * 2, 1, D_MODEL)
    return pl.pallas_call(
        _norm_mod_kernel,
        grid=(n_rows // tm,),
        in_specs=[
            pl.BlockSpec((tm, D_MODEL), lambda i: (i, 0)),
            pl.BlockSpec((None, 1, D_MODEL), lambda i: (layer * 2 + which_norm, 0, 0)),
            _mod_spec(layer, 3 * which_norm),
            _mod_spec(layer, 3 * which_norm + 1),
        ],
        out_specs=pl.BlockSpec((tm, D_MODEL), lambda i: (i, 0)),
        out_shape=jax.ShapeDtypeStruct((M_ALL, D_MODEL), BF16),
        compiler_params=_params(1),
        name="norm_mod",
    )(h, g, mod, mod)


def _final_norm_kernel(h_ref, g_ref, o_ref):
    x = h_ref[...]
    y = x * lax.rsqrt(jnp.mean(x * x, axis=-1, keepdims=True) + EPS)
    o_ref[...] = y * g_ref[...]


def _final_norm(h, final_g):
    tm = NORM_ROW_TILE
    return pl.pallas_call(
        _final_norm_kernel,
        grid=(M_LAT // tm,),
        in_specs=[
            pl.BlockSpec((tm, D_MODEL), lambda i: (i, 0)),
            pl.BlockSpec((1, D_MODEL), lambda i: (0, 0)),
        ],
        out_specs=pl.BlockSpec((tm, D_MODEL), lambda i: (i, 0)),
        out_shape=jax.ShapeDtypeStruct((M_LAT, D_MODEL), F32),
        compiler_params=_params(1),
        name="final_norm",
    )(h, final_g.reshape(1, D_MODEL))


def _mm_kernel(x_ref, w_ref, o_ref, wbf_ref, *, act):
    @pl.when(pl.program_id(1) == 0)
    def _():
        wbf_ref[...] = w_ref[...].astype(BF16)

    acc = jnp.dot(x_ref[...], wbf_ref[...], preferred_element_type=F32)
    if act == "gelu":
        acc = jax.nn.gelu(acc)
    o_ref[...] = acc.astype(o_ref.dtype)


def _mm_resid_kernel(x_ref, w_ref, gate_ref, res_ref, o_ref, wbf_ref):
    i = pl.program_id(1)
    tm = x_ref.shape[0]

    @pl.when(i == 0)
    def _():
        wbf_ref[...] = w_ref[...].astype(BF16)

    acc = jnp.dot(x_ref[...], wbf_ref[...], preferred_element_type=F32)
    o_ref[...] = res_ref[...] + _rows_of_group(gate_ref, i * tm, tm) * acc


def _matmul(x, w, w_idx, *, n_rows, tn, out_dtype, act=None):
    tm = ROW_TILE[n_rows]
    _, k, n = w.shape
    return pl.pallas_call(
        functools.partial(_mm_kernel, act=act),
        grid=(n // tn, n_rows // tm),
        in_specs=[
            pl.BlockSpec((tm, k), lambda j, i: (i, 0)),
            pl.BlockSpec((None, k, tn), lambda j, i: (w_idx, 0, j)),
        ],
        out_specs=pl.BlockSpec((tm, tn), lambda j, i: (i, j)),
        out_shape=jax.ShapeDtypeStruct((x.shape[0], n), out_dtype),
        scratch_shapes=[pltpu.VMEM((k, tn), BF16)],
        compiler_params=_params(2),
        name="matmul",
    )(x, w)


def _matmul_resid(x, w, w_idx, h, mod, layer, which_gate, *, n_rows, tn):
    _, k, n = w.shape
    w_tile_bytes = k * tn * 4
    tm = (ROW_TILE_WIDE_K if w_tile_bytes > 8 * 2**20 else ROW_TILE)[n_rows]
    w_buffers = pl.Buffered(1) if w_tile_bytes > 12 * 2**20 else None
    return pl.pallas_call(
        _mm_resid_kernel,
        grid=(n // tn, n_rows // tm),
        in_specs=[
            pl.BlockSpec((tm, k), lambda j, i: (i, 0)),
            pl.BlockSpec((None, k, tn), lambda j, i: (w_idx, 0, j), pipeline_mode=w_buffers),
            _mod_spec(layer, which_gate, tn, grid_rank=2),
            pl.BlockSpec((tm, tn), lambda j, i: (i, j)),
        ],
        out_specs=pl.BlockSpec((tm, tn), lambda j, i: (i, j)),
        out_shape=jax.ShapeDtypeStruct(h.shape, F32),
        scratch_shapes=[pltpu.VMEM((k, tn), BF16)],
        input_output_aliases={3: 0},
        compiler_params=_params(2),
        name="matmul_resid",
    )(x, w, mod, h)


HALO = BF16_ROWS


SEQ_STARTS = tuple(b * SEQ for b in range(BATCH)) + tuple(M_LAT + b * CTX_LEN for b in range(BATCH))
FIX_ROWS = BF16_ROWS


def _is_seq_edge(row):
    edge = row == M_ALL
    for s in SEQ_STARTS:
        edge = edge | (row == s)
    return edge


def _conv3(prev, cur, nxt, cw_ref, cb_ref):
    cw = cw_ref[...]
    y = cb_ref[...] + prev * cw[0:1]
    y = y + cur * cw[1:2]
    return y + nxt * cw[2:3]


def _silu_gate(g, v):
    half_g = 0.5 * g
    return (half_g * jnp.tanh(half_g) + half_g) * v


def _ffn_up_kernel(x_ref, xp_ref, xn_ref, wg_ref, wv_ref, cwg_ref, cwv_ref, cbg_ref, cbv_ref,
                   o_ref, wg_bf, wv_bf, xs_ref, zg_ref, zv_ref):
    i = pl.program_id(1)
    tm = x_ref.shape[0]
    row0 = i * tm

    @pl.when(i == 0)
    def _():
        wg_bf[...] = wg_ref[...].astype(BF16)
        wv_bf[...] = wv_ref[...].astype(BF16)

    no_rows = jnp.zeros_like(xp_ref)
    xs_ref[0:HALO, :] = jnp.where(_is_seq_edge(row0), no_rows, xp_ref[...])
    xs_ref[HALO:HALO + tm, :] = x_ref[...]
    xs_ref[HALO + tm:, :] = jnp.where(_is_seq_edge(row0 + tm), no_rows, xn_ref[...])
    xs = xs_ref[...]
    zg_ref[...] = jnp.dot(xs, wg_bf[...], preferred_element_type=F32)
    zv_ref[...] = jnp.dot(xs, wv_bf[...], preferred_element_type=F32)

    def conv(z_ref, cw_ref, cb_ref):
        return _conv3(z_ref[HALO - 1:HALO - 1 + tm, :], z_ref[HALO:HALO + tm, :],
                      z_ref[HALO + 1:HALO + 1 + tm, :], cw_ref, cb_ref)

    o_ref[...] = _silu_gate(conv(zg_ref, cwg_ref, cbg_ref), conv(zv_ref, cwv_ref, cbv_ref)).astype(o_ref.dtype)

    n_win = 2 * FIX_ROWS + 2 * SUBLANES
    w_row = lax.broadcasted_iota(jnp.int32, (n_win, 1), 0)
    start_row = SUBLANES + FIX_ROWS

    def fixed_conv(z_ref, cw_ref, cb_ref, win):
        z = z_ref[win, :]
        prev = jnp.where(w_row == start_row, 0.0, pltpu.roll(z, 1, 0))
        nxt = jnp.where(w_row == start_row - 1, 0.0, pltpu.roll(z, n_win - 1, 0))
        return _conv3(prev, z, nxt, cw_ref, cb_ref)[SUBLANES:SUBLANES + 2 * FIX_ROWS]

    for s in SEQ_STARTS[1:]:
        @pl.when((s > row0) & (s < row0 + tm))
        def _():
            win = pl.ds(pl.multiple_of(HALO + s - row0 - start_row, SUBLANES), n_win)
            rows = pl.ds(pl.multiple_of(s - row0 - FIX_ROWS, FIX_ROWS), 2 * FIX_ROWS)
            g = fixed_conv(zg_ref, cwg_ref, cbg_ref, win)
            v = fixed_conv(zv_ref, cwv_ref, cbv_ref, win)
            o_ref[rows, :] = _silu_gate(g, v).astype(o_ref.dtype)


def _ffn_up(n, w_up, conv_w, conv_b, layer, *, n_rows):
    tm, tn = ROW_TILE[n_rows], 512
    k = w_up.shape[1]
    nj = D_FF // tn
    halo_per_tile = tm // HALO
    last_halo = M_ALL // HALO - 1
    cb = conv_b.reshape(DEPTH, 1, 2 * D_FF)
    return pl.pallas_call(
        _ffn_up_kernel,
        grid=(nj, n_rows // tm),
        in_specs=[
            pl.BlockSpec((tm, k), lambda j, i: (i, 0)),
            pl.BlockSpec((HALO, k), lambda j, i: (jnp.maximum(i * halo_per_tile - 1, 0), 0)),
            pl.BlockSpec((HALO, k), lambda j, i: (jnp.minimum((i + 1) * halo_per_tile, last_halo), 0)),
            pl.BlockSpec((None, k, tn), lambda j, i: (layer, 0, j)),
            pl.BlockSpec((None, k, tn), lambda j, i: (layer, 0, j + nj)),
            pl.BlockSpec((None, 3, tn), lambda j, i: (layer, 0, j)),
            pl.BlockSpec((None, 3, tn), lambda j, i: (layer, 0, j + nj)),
            pl.BlockSpec((None, 1, tn), lambda j, i: (layer, 0, j)),
            pl.BlockSpec((None, 1, tn), lambda j, i: (layer, 0, j + nj)),
        ],
        out_specs=pl.BlockSpec((tm, tn), lambda j, i: (i, j)),
        out_shape=jax.ShapeDtypeStruct((M_ALL, D_FF), BF16),
        scratch_shapes=[
            pltpu.VMEM((k, tn), BF16),
            pltpu.VMEM((k, tn), BF16),
            pltpu.VMEM((tm + 2 * HALO, k), BF16),
            pltpu.VMEM((tm + 2 * HALO, tn), F32),
            pltpu.VMEM((tm + 2 * HALO, tn), F32),
        ],
        compiler_params=_params(2),
        name="ffn_up_conv_gate",
    )(n, n, n, w_up, w_up, conv_w, conv_w, cb, cb)


def _sgu_kernel(u_ref, v_ref, gv_ref, ws_ref, bs_ref, o_ref):
    tm = u_ref.shape[0]
    gw = A_WIDTH // A_GROUPS
    gv = gv_ref[...]
    for c in range(tm // CHUNK):
        rows = slice(c * CHUNK, (c + 1) * CHUNK)
        v = v_ref[rows, :].astype(F32)
        vn = v * lax.rsqrt(jnp.mean(v * v, axis=-1, keepdims=True) + EPS) * gv
        vn = vn.astype(BF16)
        for g in range(A_GROUPS):
            cols = slice(g * gw, (g + 1) * gw)
            s = jnp.dot(ws_ref[g].astype(BF16), vn[:, cols], preferred_element_type=F32)
            s = s + bs_ref[:, g:g + 1]
            o_ref[rows, cols] = (u_ref[rows, cols].astype(F32) * s).astype(o_ref.dtype)


def _sgu(z, g_v, w_s, b_s, j, *, n_rows):
    tm = 4 * CHUNK
    n_a = g_v.shape[0]
    return pl.pallas_call(
        _sgu_kernel,
        grid=(n_rows // tm,),
        in_specs=[
            pl.BlockSpec((tm, A_WIDTH), lambda i: (i, 0)),
            pl.BlockSpec((tm, A_WIDTH), lambda i: (i, 1)),
            pl.BlockSpec((None, 1, A_WIDTH), lambda i: (j, 0, 0)),
            pl.BlockSpec((None, A_GROUPS, CHUNK, CHUNK), lambda i: (j, 0, 0, 0)),
            pl.BlockSpec((None, CHUNK, A_GROUPS), lambda i: (j, 0, 0)),
        ],
        out_specs=pl.BlockSpec((tm, A_WIDTH), lambda i: (i, 0)),
        out_shape=jax.ShapeDtypeStruct((M_ALL, A_WIDTH), BF16),
        compiler_params=_params(1),
        name="sgu",
    )(z, z, g_v.reshape(n_a, 1, A_WIDTH), w_s, b_s.transpose(0, 2, 1))


NA_DR = 2 * NA_KH - 1
NA_DC = 2 * NA_KW - 1
NA_PAIR_TILES = NA_DR + 1
NA_QROWS = 4
NA_KROWS = NA_KH + NA_QROWS
NA_BLOCK_UNROLL = 2


def _na_bias_pairs(rpb_ref, head, pair_ref):
    shape = (GRID_W, 2 * GRID_W)
    q = lax.broadcasted_iota(jnp.int32, shape, 0)
    lane = lax.broadcasted_iota(jnp.int32, shape, 1)
    kcol = lane & (GRID_W - 1)
    upper = lane >= GRID_W
    dc = jnp.clip(kcol - q, -(NA_KW - 1), NA_KW - 1) + NA_KW - 1
    c0 = jnp.clip(q - NA_KW // 2, 0, GRID_W - NA_KW)
    ok = (kcol >= c0) & (kcol < c0 + NA_KW)
    for d in range(-1, NA_DR):
        acc = jnp.full(shape, NEG_INF, F32)
        for c in range(NA_DC):
            lo = rpb_ref[head, d * NA_DC + c] if d >= 0 else NEG_INF
            hi = rpb_ref[head, (d + 1) * NA_DC + c] if d + 1 < NA_DR else NEG_INF
            acc = jnp.where(dc == c, jnp.where(upper, hi, lo), acc)
        pair_ref[d + 1] = jnp.where(ok, acc * LOG2_E, NEG_INF)


def _na_window_start(m):
    return min(max(m * NA_QROWS - NA_KH // 2, 0), GRID_ROWS - NA_KROWS)


def _na_block_bias(pair_ref, m):
    upper = lax.broadcasted_iota(jnp.int32, (GRID_W, 2 * GRID_W), 1) >= GRID_W
    masked = jnp.full((GRID_W, 2 * GRID_W), NEG_INF, F32)
    ws = _na_window_start(m)
    rows = []
    for a in range(NA_QROWS):
        r = m * NA_QROWS + a
        r0 = min(max(r - NA_KH // 2, 0), GRID_ROWS - NA_KH)
        tiles = []
        for p in range(NA_KROWS // 2):
            key_row = ws + 2 * p
            in_lo = r0 <= key_row < r0 + NA_KH
            in_hi = r0 <= key_row + 1 < r0 + NA_KH
            if not (in_lo or in_hi):
                tiles.append(masked)
                continue
            tile = pair_ref[key_row - r + NA_KH]
            if not in_hi:
                tile = jnp.where(upper, NEG_INF, tile)
            elif not in_lo:
                tile = jnp.where(upper, tile, NEG_INF)
            tiles.append(tile)
        rows.append(jnp.concatenate(tiles, axis=1))
    return jnp.concatenate(rows, axis=0)


def _na_lat_kernel(rpb_ref, q_ref, k_ref, v_ref, kc_ref, vc_ref, o_ref, pair_ref, bias_ref):
    scale = NA_HEAD_DIM ** -0.5 * LOG2_E
    _na_bias_pairs(rpb_ref, pl.program_id(1), pair_ref)
    kc = kc_ref[...]
    vc = vc_ref[...]
    nt = (((1,), (1,)), ((), ()))
    n_q = NA_QROWS * GRID_W
    n_blocks = GRID_ROWS // NA_QROWS

    def attend(qrows, win, bias):
        q = q_ref[qrows, :]
        kw = k_ref[win, :]
        vw = v_ref[win, :]
        s = lax.dot_general(q, kw, nt, preferred_element_type=F32) * scale
        s = jnp.where(bias > 0.5 * NEG_INF, s + bias, NEG_INF)
        sc = lax.dot_general(q, kc, nt, preferred_element_type=F32) * scale
        mx = jnp.maximum(jnp.max(s, axis=-1, keepdims=True), jnp.max(sc, axis=-1, keepdims=True))
        p_w = jnp.exp2(s - mx)
        p_c = jnp.exp2(sc - mx)
        denom = jnp.sum(p_w, axis=-1, keepdims=True) + jnp.sum(p_c, axis=-1, keepdims=True)
        o = jnp.dot(p_w.astype(BF16), vw, preferred_element_type=F32)
        o = o + jnp.dot(p_c.astype(BF16), vc, preferred_element_type=F32)
        o_ref[qrows, :] = (o / denom).astype(o_ref.dtype)

    def edge_block(m):
        ws = _na_window_start(m)
        attend(slice(m * n_q, (m + 1) * n_q), slice(ws * GRID_W, (ws + NA_KROWS) * GRID_W),
               _na_block_bias(pair_ref, m))

    bias_ref[...] = _na_block_bias(pair_ref, 1)

    def interior_block(m, carry):
        qrows = pl.ds(pl.multiple_of(m * n_q, n_q), n_q)
        ws = m * NA_QROWS - NA_KH // 2
        win = pl.ds(pl.multiple_of(ws * GRID_W, GRID_W), NA_KROWS * GRID_W)
        attend(qrows, win, bias_ref[...])
        return carry

    edge_block(0)
    lax.fori_loop(1, n_blocks - 1, interior_block, 0, unroll=NA_BLOCK_UNROLL)
    edge_block(n_blocks - 1)


def _na_ctx_kernel(q_ref, k_ref, v_ref, o_ref):
    scale = NA_HEAD_DIM ** -0.5
    s = lax.dot_general(q_ref[...], k_ref[...], (((1,), (1,)), ((), ())),
                        preferred_element_type=F32) * scale
    m = jnp.max(s, axis=-1, keepdims=True)
    p = jnp.exp(s - m)
    denom = jnp.sum(p, axis=-1, keepdims=True)
    o = jnp.dot(p.astype(BF16), v_ref[...], preferred_element_type=F32)
    o_ref[...] = (o / denom).astype(o_ref.dtype)


def _na_attention(qkv, rpb):
    dh, nh = NA_HEAD_DIM, NA_HEADS
    ctx_blk0 = M_LAT // CTX_LEN
    lat = pl.pallas_call(
        _na_lat_kernel,
        grid=(BATCH, nh),
        in_specs=[
            pl.BlockSpec(memory_space=pltpu.SMEM),
            pl.BlockSpec((SEQ, dh), lambda b, h: (b, h)),
            pl.BlockSpec((SEQ, dh), lambda b, h: (b, nh + h)),
            pl.BlockSpec((SEQ, dh), lambda b, h: (b, 2 * nh + h)),
            pl.BlockSpec((CTX_LEN, dh), lambda b, h: (ctx_blk0 + b, nh + h)),
            pl.BlockSpec((CTX_LEN, dh), lambda b, h: (ctx_blk0 + b, 2 * nh + h)),
        ],
        out_specs=pl.BlockSpec((SEQ, dh), lambda b, h: (b, h)),
        out_shape=jax.ShapeDtypeStruct((M_ALL, D_MODEL), BF16),
        scratch_shapes=[
            pltpu.VMEM((NA_PAIR_TILES, GRID_W, 2 * GRID_W), F32),
            pltpu.VMEM((NA_QROWS * GRID_W, NA_KROWS * GRID_W), F32),
        ],
        compiler_params=_params(2),
        name="na_latent",
    )(rpb.reshape(nh, NA_DR * NA_DC), qkv, qkv, qkv, qkv, qkv)
    ctx = pl.pallas_call(
        _na_ctx_kernel,
        grid=(BATCH, nh),
        in_specs=[
            pl.BlockSpec((CTX_LEN, dh), lambda b, h: (ctx_blk0 + b, h)),
            pl.BlockSpec((CTX_LEN, dh), lambda b, h: (ctx_blk0 + b, nh + h)),
            pl.BlockSpec((CTX_LEN, dh), lambda b, h: (ctx_blk0 + b, 2 * nh + h)),
        ],
        out_specs=pl.BlockSpec((CTX_LEN, dh), lambda b, h: (b, h)),
        out_shape=jax.ShapeDtypeStruct((M_CTX, D_MODEL), BF16),
        compiler_params=_params(2),
        name="na_context",
    )(qkv, qkv, qkv)
    return lax.dynamic_update_slice(lat, ctx, (M_LAT, 0))


RNN_LEN = CTX_LEN + SEQ
SCAN_UNROLL = 8


def _shift_rows(x, offset, length):
    t = lax.broadcasted_iota(jnp.int32, (length, 1), 0)
    rolled = pltpu.roll(x, (-offset) % length, 0)
    return jnp.where((t + offset >= 0) & (t + offset < length), rolled, 0.0)


def _rnn_conv(x, cw, cb):
    length = x.shape[0]
    left = RNN_CONV // 2
    y = cb
    for j in range(RNN_CONV):
        y = y + _shift_rows(x, j - left, length) * cw[j:j + 1]
    return y


def _block_scan(a, b, reverse):
    row = lax.broadcasted_iota(jnp.int32, a.shape, 0)
    for s in (1, 2, 4):
        if reverse:
            keep = row < SUBLANES - s
            shift = SUBLANES - s
        else:
            keep = row >= s
            shift = s
        a_sh = pltpu.roll(a, shift, 0)
        b_sh = pltpu.roll(b, shift, 0)
        b = jnp.where(keep, a * b_sh + b, b)
        a = jnp.where(keep, a * a_sh, a)
    return a, b


def _rglru_kernel(xc_ref, xl_ref, yc_ref, yl_ref, cw_ref, cb_ref, wg_ref, bg_ref, lam_ref,
                  oc_ref, ol_ref, af_ref, bf_ref, ar_ref, br_ref, hf_ref, hr_ref):
    cw = cw_ref[...]
    cb = cb_ref[...]
    xr_c = _rnn_conv(xc_ref[...].astype(F32), cw, cb)
    xr_l = _rnn_conv(xl_ref[...].astype(F32), cw, cb)

    def gates(xr, d, a_ref, b_ref, rows):
        xb = xr.astype(BF16)
        g_r = jnp.dot(xb, wg_ref[2 * d].astype(BF16), preferred_element_type=F32) + bg_ref[2 * d]
        g_i = jnp.dot(xb, wg_ref[2 * d + 1].astype(BF16), preferred_element_type=F32) + bg_ref[2 * d + 1]
        g_r = _sigmoid(g_r)
        g_i = _sigmoid(g_i)
        neg_lam = -lam_ref[d:d + 1, :]
        softplus = jnp.maximum(neg_lam, 0.0) + jnp.log1p(jnp.exp(-jnp.abs(neg_lam)))
        log_a = -RG_C * g_r * softplus
        a = jnp.exp(log_a)
        a_ref[rows, :] = a
        b_ref[rows, :] = jnp.sqrt(1.0 - a * a) * (g_i * xr)

    ctx_f, lat_f = slice(0, CTX_LEN), slice(CTX_LEN, RNN_LEN)
    lat_r, ctx_r = slice(0, SEQ), slice(SEQ, RNN_LEN)
    gates(xr_c, 0, af_ref, bf_ref, ctx_f)
    gates(xr_l, 0, af_ref, bf_ref, lat_f)
    gates(xr_c, 1, ar_ref, br_ref, ctx_r)
    gates(xr_l, 1, ar_ref, br_ref, lat_r)

    nblk = RNN_LEN // SUBLANES
    last_row = slice(SUBLANES - 1, SUBLANES)
    first_row = slice(0, 1)
    blk_shape = (SUBLANES, RNN_HEAD_DIM)

    def block(n, carry):
        cf, cr = carry
        rows_f = pl.ds(pl.multiple_of(n * SUBLANES, SUBLANES), SUBLANES)
        rows_r = pl.ds(pl.multiple_of((nblk - 1 - n) * SUBLANES, SUBLANES), SUBLANES)
        a_f, b_f = _block_scan(af_ref[rows_f, :], bf_ref[rows_f, :], reverse=False)
        a_r, b_r = _block_scan(ar_ref[rows_r, :], br_ref[rows_r, :], reverse=True)
        hf_ref[rows_f, :] = a_f * cf + b_f
        hr_ref[rows_r, :] = a_r * cr + b_r
        cf = jnp.broadcast_to(a_f[last_row], blk_shape) * cf + jnp.broadcast_to(b_f[last_row], blk_shape)
        cr = jnp.broadcast_to(a_r[first_row], blk_shape) * cr + jnp.broadcast_to(b_r[first_row], blk_shape)
        return cf, cr

    zero = jnp.zeros(blk_shape, F32)
    lax.fori_loop(0, nblk, block, (zero, zero), unroll=SCAN_UNROLL)

    h_c = hf_ref[ctx_f, :] + hr_ref[ctx_r, :]
    h_l = hf_ref[lat_f, :] + hr_ref[lat_r, :]
    oc_ref[...] = (jax.nn.gelu(yc_ref[...].astype(F32)) * h_c).astype(oc_ref.dtype)
    ol_ref[...] = (jax.nn.gelu(yl_ref[...].astype(F32)) * h_l).astype(ol_ref.dtype)


def _rglru(xin, conv_w, conv_b, w_gate, b_gate, lam, j):
    dh, nh = RNN_HEAD_DIM, RNN_HEADS
    n_c = conv_w.shape[0]
    ctx_blk0 = M_LAT // CTX_LEN
    wg = w_gate.reshape(n_c, 4, nh, dh, dh)
    bg = b_gate.reshape(n_c, 4, nh, 1, dh)
    out_c, out_l = pl.pallas_call(
        _rglru_kernel,
        grid=(BATCH, nh),
        in_specs=[
            pl.BlockSpec((CTX_LEN, dh), lambda b, h: (ctx_blk0 + b, nh + h)),
            pl.BlockSpec((SEQ, dh), lambda b, h: (b, nh + h)),
            pl.BlockSpec((CTX_LEN, dh), lambda b, h: (ctx_blk0 + b, h)),
            pl.BlockSpec((SEQ, dh), lambda b, h: (b, h)),
            pl.BlockSpec((None, RNN_CONV, dh), lambda b, h: (j, 0, h)),
            pl.BlockSpec((None, 1, dh), lambda b, h: (j, 0, h)),
            pl.BlockSpec((None, 4, None, dh, dh), lambda b, h: (j, 0, h, 0, 0)),
            pl.BlockSpec((None, 4, None, 1, dh), lambda b, h: (j, 0, h, 0, 0)),
            pl.BlockSpec((None, 2, dh), lambda b, h: (j, 0, h)),
        ],
        out_specs=[
            pl.BlockSpec((CTX_LEN, dh), lambda b, h: (b, h)),
            pl.BlockSpec((SEQ, dh), lambda b, h: (b, h)),
        ],
        out_shape=[
            jax.ShapeDtypeStruct((M_CTX, RNN_WIDTH), BF16),
            jax.ShapeDtypeStruct((M_ALL, RNN_WIDTH), BF16),
        ],
        scratch_shapes=[pltpu.VMEM((RNN_LEN, dh), F32)] * 6,
        compiler_params=_params(2),
        name="rglru",
    )(xin, xin, xin, xin, conv_w, conv_b.reshape(n_c, 1, RNN_WIDTH), wg, bg, lam)
    return lax.dynamic_update_slice(out_l, out_c, (M_LAT, 0))


def kernel(x, c, ctx, c_ctx, ada_w, ada_b, norm_g, ffn_w_up, ffn_conv_w, ffn_conv_b, ffn_w_down,
           a_w_in, a_g_v, a_w_s, a_b_s, a_w_out, b_w_qkv, b_rpb, b_w_out,
           c_w_in, c_conv_w, c_conv_b, c_w_gate, c_b_gate, c_lam, c_w_out, final_g):
    h = jnp.concatenate([x.reshape(M_LAT, D_MODEL), ctx.reshape(M_CTX, D_MODEL)], axis=0)
    cvec = jnp.concatenate(
        [c, c_ctx[None], jnp.zeros((SUBLANES - BATCH - 1, D_MODEL), F32)], axis=0)
    mod = _ada_mod(cvec, ada_w, ada_b)

    for i in range(DEPTH):
        last = i == DEPTH - 1
        kind, j = i % N_MIXERS, i // N_MIXERS
        ffn_rows = M_LAT if last else M_ALL
        n_rows = M_LAT if (last and kind == 0) else M_ALL

        n = _norm_mod(h, norm_g, mod, i, 0, n_rows)
        if kind == 0:
            z = _matmul(n, a_w_in, j, n_rows=n_rows, tn=1024, out_dtype=BF16, act="gelu")
            y = _sgu(z, a_g_v, a_w_s, a_b_s, j, n_rows=n_rows)
            w_out = a_w_out
        elif kind == 1:
            qkv = _matmul(n, b_w_qkv, j, n_rows=n_rows, tn=1024, out_dtype=BF16)
            y = _na_attention(qkv, b_rpb[j])
            w_out = b_w_out
        else:
            xin = _matmul(n, c_w_in, j, n_rows=n_rows, tn=1024, out_dtype=F32)
            y = _rglru(xin, c_conv_w, c_conv_b, c_w_gate, c_b_gate, c_lam, j)
            w_out = c_w_out
        h = _matmul_resid(y, w_out, j, h, mod, i, 2, n_rows=n_rows, tn=1024)

        n = _norm_mod(h, norm_g, mod, i, 1, ffn_rows)
        a = _ffn_up(n, ffn_w_up, ffn_conv_w, ffn_conv_b, i, n_rows=ffn_rows)
        h = _matmul_resid(a, ffn_w_down, i, h, mod, i, 5, n_rows=ffn_rows, tn=512)

    return _final_norm(h, final_g).reshape(BATCH, SEQ, D_MODEL)
```

```python
import functools

import jax
import jax.numpy as jnp
from jax import lax
from jax.experimental import pallas as pl
from jax.experimental.pallas import tpu as pltpu

F32 = jnp.float32
BF16 = jnp.bfloat16

D_MODEL = 2048
BATCH = 2
SEQ = 4096
DEPTH = 4
GRID_W = 64
CTX_LEN = 256
N_MIXERS = 3
N_MOD = 6
EPS = 1e-6
NEG_INF = -1e30
LOG2_E = 1.4426950408889634
D_FF = 5632
CHUNK = 128
A_WIDTH = 2 * D_MODEL
A_GROUPS = 16
NA_HEADS = 16
NA_HEAD_DIM = D_MODEL // NA_HEADS
NA_KH = 8
NA_KW = 16
RNN_WIDTH = D_MODEL
RNN_HEADS = 16
RNN_HEAD_DIM = RNN_WIDTH // RNN_HEADS
RNN_CONV = 4
RG_C = 8.0

M_LAT = BATCH * SEQ
M_CTX = BATCH * CTX_LEN
M_ALL = M_LAT + M_CTX
GRID_ROWS = SEQ // GRID_W

VMEM_LIMIT_BYTES = 56 * 1024 * 1024
SUBLANES = 8
LANES = 128
BF16_ROWS = 16
MXU_WIDTH = 256

ROW_TILE = {M_ALL: M_ALL // 8, M_LAT: M_LAT // 8}
ROW_TILE_WIDE_K = {M_ALL: M_ALL // 16, M_LAT: M_LAT // 16}
NORM_ROW_TILE = 512


def _params(n_axes, flags=None):
    return pltpu.CompilerParams(
        dimension_semantics=("arbitrary",) * n_axes, vmem_limit_bytes=VMEM_LIMIT_BYTES, flags=flags)


def _sigmoid(x):
    return 0.5 * jnp.tanh(0.5 * x) + 0.5


def _mod_spec(layer, which, tn=D_MODEL, grid_rank=1):
    if grid_rank == 1:
        return pl.BlockSpec((None, SUBLANES, tn), lambda i: (layer * N_MOD + which, 0, 0))
    return pl.BlockSpec((None, SUBLANES, tn), lambda j, i: (layer * N_MOD + which, 0, j))


def _rows_of_group(mod_ref, row0, tm):
    r = row0 + lax.broadcasted_iota(jnp.int32, (tm, 1), 0)
    m = mod_ref[...]
    out = m[BATCH:BATCH + 1]
    for b in reversed(range(BATCH)):
        out = jnp.where(r < (b + 1) * SEQ, m[b:b + 1], out)
    return out


def _ada_kernel(c_ref, w_ref, b_ref, o_ref):
    c = c_ref[...]
    s = (c * _sigmoid(c)).astype(BF16)
    o_ref[...] = jnp.dot(s, w_ref[...].astype(BF16), preferred_element_type=F32) + b_ref[...]


def _ada_mod(cvec, ada_w, ada_b):
    depth, d, n = ada_w.shape
    tn = 1024
    out = pl.pallas_call(
        _ada_kernel,
        grid=(depth, n // tn),
        in_specs=[
            pl.BlockSpec((SUBLANES, d), lambda l, j: (0, 0)),
            pl.BlockSpec((None, d, tn), lambda l, j: (l, 0, j)),
            pl.BlockSpec((None, 1, tn), lambda l, j: (l, 0, j)),
        ],
        out_specs=pl.BlockSpec((None, SUBLANES, tn), lambda l, j: (l, 0, j)),
        out_shape=jax.ShapeDtypeStruct((depth, SUBLANES, n), F32),
        compiler_params=_params(2),
        name="ada_mod",
    )(cvec, ada_w, ada_b.reshape(depth, 1, n))
    return out.reshape(depth, SUBLANES, N_MOD, d).transpose(0, 2, 1, 3).reshape(depth * N_MOD, SUBLANES, d)


def _norm_mod_kernel(h_ref, g_ref, shift_ref, scale_ref, o_ref):
    group = pl.ds(jnp.minimum(pl.program_id(0) * h_ref.shape[0] // SEQ, BATCH), 1)
    x = h_ref[...]
    y = x * lax.rsqrt(jnp.mean(x * x, axis=-1, keepdims=True) + EPS)
    y = y * g_ref[...]
    y = y * (1 + scale_ref[group, :]) + shift_ref[group, :]
    o_ref[...] = y.astype(o_ref.dtype)


def _norm_mod(h, norm_g, mod, layer, which_norm, n_rows):
    tm = NORM_ROW_TILE
    g = norm_g.reshape(DEPTH * 2, 1, D_MODEL)
    return pl.pallas_call(
        _norm_mod_kernel,
        grid=(n_rows // tm,),
        in_specs=[
            pl.BlockSpec((tm, D_MODEL), lambda i: (i, 0)),
            pl.BlockSpec((None, 1, D_MODEL), lambda i: (layer * 2 + which_norm, 0, 0)),
            _mod_spec(layer, 3 * which_norm),
            _mod_spec(layer, 3 * which_norm + 1),
        ],
        out_specs=pl.BlockSpec((tm, D_MODEL), lambda i: (i, 0)),
        out_shape=jax.ShapeDtypeStruct((M_ALL, D_MODEL), BF16),
        compiler_params=_params(1),
        name="norm_mod",
    )(h, g, mod, mod)


def _final_norm_kernel(h_ref, g_ref, o_ref):
    x = h_ref[...]
    y = x * lax.rsqrt(jnp.mean(x * x, axis=-1, keepdims=True) + EPS)
    o_ref[...] = y * g_ref[...]


def _final_norm(h, final_g):
    tm = NORM_ROW_TILE
    return pl.pallas_call(
        _final_norm_kernel,
        grid=(M_LAT // tm,),
        in_specs=[
            pl.BlockSpec((tm, D_MODEL), lambda i: (i, 0)),
            pl.BlockSpec((1, D_MODEL), lambda i: (0, 0)),
        ],
        out_specs=pl.BlockSpec((tm, D_MODEL), lambda i: (i, 0)),
        out_shape=jax.ShapeDtypeStruct((M_LAT, D_MODEL), F32),
        compiler_params=_params(1),
        name="final_norm",
    )(h, final_g.reshape(1, D_MODEL))


def _mm_kernel(x_ref, w_ref, o_ref, wbf_ref, *, act):
    @pl.when(pl.program_id(1) == 0)
    def _():
        wbf_ref[...] = w_ref[...].astype(BF16)

    acc = jnp.dot(x_ref[...], wbf_ref[...], preferred_element_type=F32)
    if act == "gelu":
        acc = jax.nn.gelu(acc)
    o_ref[...] = acc.astype(o_ref.dtype)


def _mm_resid_kernel(x_ref, w_ref, gate_ref, res_ref, o_ref, wbf_ref):
    i = pl.program_id(1)
    tm = x_ref.shape[0]

    @pl.when(i == 0)
    def _():
        wbf_ref[...] = w_ref[...].astype(BF16)

    acc = jnp.dot(x_ref[...], wbf_ref[...], preferred_element_type=F32)
    o_ref[...] = res_ref[...] + _rows_of_group(gate_ref, i * tm, tm) * acc


def _matmul(x, w, w_idx, *, n_rows, tn, out_dtype, act=None):
    tm = ROW_TILE[n_rows]
    _, k, n = w.shape
    return pl.pallas_call(
        functools.partial(_mm_kernel, act=act),
        grid=(n // tn, n_rows // tm),
        in_specs=[
            pl.BlockSpec((tm, k), lambda j, i: (i, 0)),
            pl.BlockSpec((None, k, tn), lambda j, i: (w_idx, 0, j)),
        ],
        out_specs=pl.BlockSpec((tm, tn), lambda j, i: (i, j)),
        out_shape=jax.ShapeDtypeStruct((x.shape[0], n), out_dtype),
        scratch_shapes=[pltpu.VMEM((k, tn), BF16)],
        compiler_params=_params(2),
        name="matmul",
    )(x, w)


def _matmul_resid(x, w, w_idx, h, mod, layer, which_gate, *, n_rows, tn):
    _, k, n = w.shape
    w_tile_bytes = k * tn * 4
    tm = (ROW_TILE_WIDE_K if w_tile_bytes > 8 * 2**20 else ROW_TILE)[n_rows]
    w_buffers = pl.Buffered(1) if w_tile_bytes > 12 * 2**20 else None
    return pl.pallas_call(
        _mm_resid_kernel,
        grid=(n // tn, n_rows // tm),
        in_specs=[
            pl.BlockSpec((tm, k), lambda j, i: (i, 0)),
            pl.BlockSpec((None, k, tn), lambda j, i: (w_idx, 0, j), pipeline_mode=w_buffers),
            _mod_spec(layer, which_gate, tn, grid_rank=2),
            pl.BlockSpec((tm, tn), lambda j, i: (i, j)),
        ],
        out_specs=pl.BlockSpec((tm, tn), lambda j, i: (i, j)),
        out_shape=jax.ShapeDtypeStruct(h.shape, F32),
        scratch_shapes=[pltpu.VMEM((k, tn), BF16)],
        input_output_aliases={3: 0},
        compiler_params=_params(2),
        name="matmul_resid",
    )(x, w, mod, h)


HALO = BF16_ROWS


SEQ_STARTS = tuple(b * SEQ for b in range(BATCH)) + tuple(M_LAT + b * CTX_LEN for b in range(BATCH))
FIX_ROWS = BF16_ROWS


def _is_seq_edge(row):
    edge = row == M_ALL
    for s in SEQ_STARTS:
        edge = edge | (row == s)
    return edge


def _conv3(prev, cur, nxt, cw_ref, cb_ref):
    cw = cw_ref[...]
    y = cb_ref[...] + prev * cw[0:1]
    y = y + cur * cw[1:2]
    return y + nxt * cw[2:3]


def _silu_gate(g, v):
    half_g = 0.5 * g
    return (half_g * jnp.tanh(half_g) + half_g) * v


def _ffn_up_kernel(x_ref, xp_ref, xn_ref, wg_ref, wv_ref, cwg_ref, cwv_ref, cbg_ref, cbv_ref,
                   o_ref, wg_bf, wv_bf, xs_ref, zg_ref, zv_ref):
    i = pl.program_id(1)
    tm = x_ref.shape[0]
    row0 = i * tm

    @pl.when(i == 0)
    def _():
        wg_bf[...] = wg_ref[...].astype(BF16)
        wv_bf[...] = wv_ref[...].astype(BF16)

    no_rows = jnp.zeros_like(xp_ref)
    xs_ref[0:HALO, :] = jnp.where(_is_seq_edge(row0), no_rows, xp_ref[...])
    xs_ref[HALO:HALO + tm, :] = x_ref[...]
    xs_ref[HALO + tm:, :] = jnp.where(_is_seq_edge(row0 + tm), no_rows, xn_ref[...])
    xs = xs_ref[...]
    zg_ref[...] = jnp.dot(xs, wg_bf[...], preferred_element_type=F32)
    zv_ref[...] = jnp.dot(xs, wv_bf[...], preferred_element_type=F32)

    def conv(z_ref, cw_ref, cb_ref):
        return _conv3(z_ref[HALO - 1:HALO - 1 + tm, :], z_ref[HALO:HALO + tm, :],
                      z_ref[HALO + 1:HALO + 1 + tm, :], cw_ref, cb_ref)

    o_ref[...] = _silu_gate(conv(zg_ref, cwg_ref, cbg_ref), conv(zv_ref, cwv_ref, cbv_ref)).astype(o_ref.dtype)

    n_win = 2 * FIX_ROWS + 2 * SUBLANES
    w_row = lax.broadcasted_iota(jnp.int32, (n_win, 1), 0)
    start_row = SUBLANES + FIX_ROWS

    def fixed_conv(z_ref, cw_ref, cb_ref, win):
        z = z_ref[win, :]
        prev = jnp.where(w_row == start_row, 0.0, pltpu.roll(z, 1, 0))
        nxt = jnp.where(w_row == start_row - 1, 0.0, pltpu.roll(z, n_win - 1, 0))
        return _conv3(prev, z, nxt, cw_ref, cb_ref)[SUBLANES:SUBLANES + 2 * FIX_ROWS]

    for s in SEQ_STARTS[1:]:
        @pl.when((s > row0) & (s < row0 + tm))
        def _():
            win = pl.ds(pl.multiple_of(HALO + s - row0 - start_row, SUBLANES), n_win)
            rows = pl.ds(pl.multiple_of(s - row0 - FIX_ROWS, FIX_ROWS), 2 * FIX_ROWS)
            g = fixed_conv(zg_ref, cwg_ref, cbg_ref, win)
            v = fixed_conv(zv_ref, cwv_ref, cbv_ref, win)
            o_ref[rows, :] = _silu_gate(g, v).astype(o_ref.dtype)


def _ffn_up(n, w_up, conv_w, conv_b, layer, *, n_rows):
    tm, tn = ROW_TILE[n_rows], 512
    k = w_up.shape[1]
    nj = D_FF // tn
    halo_per_tile = tm // HALO
    last_halo = M_ALL // HALO - 1
    cb = conv_b.reshape(DEPTH, 1, 2 * D_FF)
    return pl.pallas_call(
        _ffn_up_kernel,
        grid=(nj, n_rows // tm),
        in_specs=[
            pl.BlockSpec((tm, k), lambda j, i: (i, 0)),
            pl.BlockSpec((HALO, k), lambda j, i: (jnp.maximum(i * halo_per_tile - 1, 0), 0)),
            pl.BlockSpec((HALO, k), lambda j, i: (jnp.minimum((i + 1) * halo_per_tile, last_halo), 0)),
            pl.BlockSpec((None, k, tn), lambda j, i: (layer, 0, j)),
            pl.BlockSpec((None, k, tn), lambda j, i: (layer, 0, j + nj)),
            pl.BlockSpec((None, 3, tn), lambda j, i: (layer, 0, j)),
            pl.BlockSpec((None, 3, tn), lambda j, i: (layer, 0, j + nj)),
            pl.BlockSpec((None, 1, tn), lambda j, i: (layer, 0, j)),
            pl.BlockSpec((None, 1, tn), lambda j, i: (layer, 0, j + nj)),
        ],
        out_specs=pl.BlockSpec((tm, tn), lambda j, i: (i, j)),
        out_shape=jax.ShapeDtypeStruct((M_ALL, D_FF), BF16),
        scratch_shapes=[
            pltpu.VMEM((k, tn), BF16),
            pltpu.VMEM((k, tn), BF16),
            pltpu.VMEM((tm + 2 * HALO, k), BF16),
            pltpu.VMEM((tm + 2 * HALO, tn), F32),
            pltpu.VMEM((tm + 2 * HALO, tn), F32),
        ],
        compiler_params=_params(2),
        name="ffn_up_conv_gate",
    )(n, n, n, w_up, w_up, conv_w, conv_w, cb, cb)


def _sgu_kernel(u_ref, v_ref, gv_ref, ws_ref, bs_ref, o_ref):
    tm = u_ref.shape[0]
    gw = A_WIDTH // A_GROUPS
    gv = gv_ref[...]
    for c in range(tm // CHUNK):
        rows = slice(c * CHUNK, (c + 1) * CHUNK)
        v = v_ref[rows, :].astype(F32)
        vn = v * lax.rsqrt(jnp.mean(v * v, axis=-1, keepdims=True) + EPS) * gv
        vn = vn.astype(BF16)
        for g in range(A_GROUPS):
            cols = slice(g * gw, (g + 1) * gw)
            s = jnp.dot(ws_ref[g].astype(BF16), vn[:, cols], preferred_element_type=F32)
            s = s + bs_ref[:, g:g + 1]
            o_ref[rows, cols] = (u_ref[rows, cols].astype(F32) * s).astype(o_ref.dtype)


def _sgu(z, g_v, w_s, b_s, j, *, n_rows):
    tm = 4 * CHUNK
    n_a = g_v.shape[0]
    return pl.pallas_call(
        _sgu_kernel,
        grid=(n_rows // tm,),
        in_specs=[
            pl.BlockSpec((tm, A_WIDTH), lambda i: (i, 0)),
            pl.BlockSpec((tm, A_WIDTH), lambda i: (i, 1)),
            pl.BlockSpec((None, 1, A_WIDTH), lambda i: (j, 0, 0)),
            pl.BlockSpec((None, A_GROUPS, CHUNK, CHUNK), lambda i: (j, 0, 0, 0)),
            pl.BlockSpec((None, CHUNK, A_GROUPS), lambda i: (j, 0, 0)),
        ],
        out_specs=pl.BlockSpec((tm, A_WIDTH), lambda i: (i, 0)),
        out_shape=jax.ShapeDtypeStruct((M_ALL, A_WIDTH), BF16),
        compiler_params=_params(1),
        name="sgu",
    )(z, z, g_v.reshape(n_a, 1, A_WIDTH), w_s, b_s.transpose(0, 2, 1))


NA_DR = 2 * NA_KH - 1
NA_DC = 2 * NA_KW - 1
NA_PAIR_TILES = NA_DR + 1
NA_QROWS = 4
NA_KROWS = NA_KH + NA_QROWS
NA_BLOCK_UNROLL = 7


def _na_bias_pairs(rpb_ref, head, pair_ref):
    shape = (GRID_W, 2 * GRID_W)
    q = lax.broadcasted_iota(jnp.int32, shape, 0)
    lane = lax.broadcasted_iota(jnp.int32, shape, 1)
    kcol = lane & (GRID_W - 1)
    upper = lane >= GRID_W
    dc = jnp.clip(kcol - q, -(NA_KW - 1), NA_KW - 1) + NA_KW - 1
    c0 = jnp.clip(q - NA_KW // 2, 0, GRID_W - NA_KW)
    ok = (kcol >= c0) & (kcol < c0 + NA_KW)
    for d in range(-1, NA_DR):
        acc = jnp.full(shape, NEG_INF, F32)
        for c in range(NA_DC):
            lo = rpb_ref[head, d * NA_DC + c] if d >= 0 else NEG_INF
            hi = rpb_ref[head, (d + 1) * NA_DC + c] if d + 1 < NA_DR else NEG_INF
            acc = jnp.where(dc == c, jnp.where(upper, hi, lo), acc)
        pair_ref[d + 1] = jnp.where(ok, acc * LOG2_E, NEG_INF)


def _na_window_start(m):
    return min(max(m * NA_QROWS - NA_KH // 2, 0), GRID_ROWS - NA_KROWS)


def _na_block_bias(pair_ref, m):
    upper = lax.broadcasted_iota(jnp.int32, (GRID_W, 2 * GRID_W), 1) >= GRID_W
    masked = jnp.full((GRID_W, 2 * GRID_W), NEG_INF, F32)
    ws = _na_window_start(m)
    rows = []
    for a in range(NA_QROWS):
        r = m * NA_QROWS + a
        r0 = min(max(r - NA_KH // 2, 0), GRID_ROWS - NA_KH)
        tiles = []
        for p in range(NA_KROWS // 2):
            key_row = ws + 2 * p
            in_lo = r0 <= key_row < r0 + NA_KH
            in_hi = r0 <= key_row + 1 < r0 + NA_KH
            if not (in_lo or in_hi):
                tiles.append(masked)
                continue
            tile = pair_ref[key_row - r + NA_KH]
            if not in_hi:
                tile = jnp.where(upper, NEG_INF, tile)
            elif not in_lo:
                tile = jnp.where(upper, tile, NEG_INF)
            tiles.append(tile)
        rows.append(jnp.concatenate(tiles, axis=1))
    return jnp.concatenate(rows, axis=0)


def _na_lat_kernel(rpb_ref, q_ref, k_ref, v_ref, kc_ref, vc_ref, o_ref, pair_ref, bias_ref):
    scale = NA_HEAD_DIM ** -0.5 * LOG2_E
    _na_bias_pairs(rpb_ref, pl.program_id(1), pair_ref)
    kc = kc_ref[...]
    vc = vc_ref[...]
    nt = (((1,), (1,)), ((), ()))
    n_q = NA_QROWS * GRID_W
    n_blocks = GRID_ROWS // NA_QROWS

    def attend(qrows, win, bias):
        q = q_ref[qrows, :]
        kw = k_ref[win, :]
        vw = v_ref[win, :]
        s = lax.dot_general(q, kw, nt, preferred_element_type=F32) * scale
        s = jnp.where(bias > 0.5 * NEG_INF, s + bias, NEG_INF)
        sc = lax.dot_general(q, kc, nt, preferred_element_type=F32) * scale
        mx = jnp.maximum(jnp.max(s, axis=-1, keepdims=True), jnp.max(sc, axis=-1, keepdims=True))
        p_w = jnp.exp2(s - mx)
        p_c = jnp.exp2(sc - mx)
        denom = jnp.sum(p_w, axis=-1, keepdims=True) + jnp.sum(p_c, axis=-1, keepdims=True)
        o = jnp.dot(p_w.astype(BF16), vw, preferred_element_type=F32)
        o = o + jnp.dot(p_c.astype(BF16), vc, preferred_element_type=F32)
        o_ref[qrows, :] = (o / denom).astype(o_ref.dtype)

    def edge_block(m):
        ws = _na_window_start(m)
        attend(slice(m * n_q, (m + 1) * n_q), slice(ws * GRID_W, (ws + NA_KROWS) * GRID_W),
               _na_block_bias(pair_ref, m))

    bias_ref[...] = _na_block_bias(pair_ref, 1)

    def interior_block(m, carry):
        qrows = pl.ds(pl.multiple_of(m * n_q, n_q), n_q)
        ws = m * NA_QROWS - NA_KH // 2
        win = pl.ds(pl.multiple_of(ws * GRID_W, GRID_W), NA_KROWS * GRID_W)
        attend(qrows, win, bias_ref[...])
        return carry

    edge_block(0)
    lax.fori_loop(1, n_blocks - 1, interior_block, 0, unroll=NA_BLOCK_UNROLL)
    edge_block(n_blocks - 1)


def _na_ctx_kernel(q_ref, k_ref, v_ref, o_ref):
    scale = NA_HEAD_DIM ** -0.5
    s = lax.dot_general(q_ref[...], k_ref[...], (((1,), (1,)), ((), ())),
                        preferred_element_type=F32) * scale
    m = jnp.max(s, axis=-1, keepdims=True)
    p = jnp.exp(s - m)
    denom = jnp.sum(p, axis=-1, keepdims=True)
    o = jnp.dot(p.astype(BF16), v_ref[...], preferred_element_type=F32)
    o_ref[...] = (o / denom).astype(o_ref.dtype)


def _na_attention(qkv, rpb):
    dh, nh = NA_HEAD_DIM, NA_HEADS
    ctx_blk0 = M_LAT // CTX_LEN
    lat = pl.pallas_call(
        _na_lat_kernel,
        grid=(BATCH, nh),
        in_specs=[
            pl.BlockSpec(memory_space=pltpu.SMEM),
            pl.BlockSpec((SEQ, dh), lambda b, h: (b, h)),
            pl.BlockSpec((SEQ, dh), lambda b, h: (b, nh + h)),
            pl.BlockSpec((SEQ, dh), lambda b, h: (b, 2 * nh + h)),
            pl.BlockSpec((CTX_LEN, dh), lambda b, h: (ctx_blk0 + b, nh + h)),
            pl.BlockSpec((CTX_LEN, dh), lambda b, h: (ctx_blk0 + b, 2 * nh + h)),
        ],
        out_specs=pl.BlockSpec((SEQ, dh), lambda b, h: (b, h)),
        out_shape=jax.ShapeDtypeStruct((M_ALL, D_MODEL), BF16),
        scratch_shapes=[
            pltpu.VMEM((NA_PAIR_TILES, GRID_W, 2 * GRID_W), F32),
            pltpu.VMEM((NA_QROWS * GRID_W, NA_KROWS * GRID_W), F32),
        ],
        compiler_params=_params(2),
        name="na_latent",
    )(rpb.reshape(nh, NA_DR * NA_DC), qkv, qkv, qkv, qkv, qkv)
    ctx = pl.pallas_call(
        _na_ctx_kernel,
        grid=(BATCH, nh),
        in_specs=[
            pl.BlockSpec((CTX_LEN, dh), lambda b, h: (ctx_blk0 + b, h)),
            pl.BlockSpec((CTX_LEN, dh), lambda b, h: (ctx_blk0 + b, nh + h)),
            pl.BlockSpec((CTX_LEN, dh), lambda b, h: (ctx_blk0 + b, 2 * nh + h)),
        ],
        out_specs=pl.BlockSpec((CTX_LEN, dh), lambda b, h: (b, h)),
        out_shape=jax.ShapeDtypeStruct((M_CTX, D_MODEL), BF16),
        compiler_params=_params(2),
        name="na_context",
    )(qkv, qkv, qkv)
    return lax.dynamic_update_slice(lat, ctx, (M_LAT, 0))


RNN_LEN = CTX_LEN + SEQ
SCAN_UNROLL = 8


def _shift_rows(x, offset, length):
    if offset == 0:
        return x
    t = lax.broadcasted_iota(jnp.int32, (length, 1), 0)
    rolled = pltpu.roll(x, (-offset) % length, 0)
    return jnp.where((t + offset >= 0) & (t + offset < length), rolled, 0.0)


def _rnn_conv(x, cw, cb):
    length = x.shape[0]
    left = RNN_CONV // 2
    y = cb
    for j in range(RNN_CONV):
        y = y + _shift_rows(x, j - left, length) * cw[j:j + 1]
    return y


def _block_scan(a, b, reverse):
    row = lax.broadcasted_iota(jnp.int32, a.shape, 0)
    for s in (1, 2, 4):
        if reverse:
            keep = row < SUBLANES - s
            shift = SUBLANES - s
        else:
            keep = row >= s
            shift = s
        a_sh = pltpu.roll(a, shift, 0)
        b_sh = pltpu.roll(b, shift, 0)
        b = jnp.where(keep, a * b_sh + b, b)
        a = jnp.where(keep, a * a_sh, a)
    return a, b


def _rglru_kernel(xc_ref, xl_ref, yc_ref, yl_ref, cw_ref, cb_ref, wg_ref, bg_ref, lam_ref,
                  oc_ref, ol_ref, af_ref, bf_ref, ar_ref, br_ref, hf_ref, hr_ref):
    cw = cw_ref[...]
    cb = cb_ref[...]
    xr_c = _rnn_conv(xc_ref[...].astype(F32), cw, cb)
    xr_l = _rnn_conv(xl_ref[...].astype(F32), cw, cb)

    def half_tanh_gate(xb, d, gate):
        w = (0.5 * wg_ref[2 * d + gate]).astype(BF16)
        return jnp.tanh(jnp.dot(xb, w, preferred_element_type=F32) + 0.5 * bg_ref[2 * d + gate])

    def gates(xr, d, a_ref, b_ref, rows):
        xb = xr.astype(BF16)
        half_xr = 0.5 * xr
        t_r = half_tanh_gate(xb, d, 0)
        t_i = half_tanh_gate(xb, d, 1)
        neg_lam = -lam_ref[d:d + 1, :]
        softplus = jnp.maximum(neg_lam, 0.0) + jnp.log1p(jnp.exp(-jnp.abs(neg_lam)))
        c = (-0.5 * RG_C * LOG2_E) * softplus
        a = jnp.exp2(c * t_r + c)
        a_ref[rows, :] = a
        y = 1.0 - a * a
        root = jnp.where(y > 0.0, y * lax.rsqrt(y), 0.0)
        b_ref[rows, :] = root * (half_xr * t_i + half_xr)

    ctx_f, lat_f = slice(0, CTX_LEN), slice(CTX_LEN, RNN_LEN)
    lat_r, ctx_r = slice(0, SEQ), slice(SEQ, RNN_LEN)
    gates(xr_c, 0, af_ref, bf_ref, ctx_f)
    gates(xr_l, 0, af_ref, bf_ref, lat_f)
    gates(xr_c, 1, ar_ref, br_ref, ctx_r)
    gates(xr_l, 1, ar_ref, br_ref, lat_r)

    nblk = RNN_LEN // SUBLANES
    last_row = slice(SUBLANES - 1, SUBLANES)
    first_row = slice(0, 1)
    blk_shape = (SUBLANES, RNN_HEAD_DIM)

    def block(n, carry):
        cf, cr = carry
        rows_f = pl.ds(pl.multiple_of(n * SUBLANES, SUBLANES), SUBLANES)
        rows_r = pl.ds(pl.multiple_of((nblk - 1 - n) * SUBLANES, SUBLANES), SUBLANES)
        a_f, b_f = _block_scan(af_ref[rows_f, :], bf_ref[rows_f, :], reverse=False)
        a_r, b_r = _block_scan(ar_ref[rows_r, :], br_ref[rows_r, :], reverse=True)
        hf_ref[rows_f, :] = a_f * cf + b_f
        hr_ref[rows_r, :] = a_r * cr + b_r
        cf = jnp.broadcast_to(a_f[last_row], blk_shape) * cf + jnp.broadcast_to(b_f[last_row], blk_shape)
        cr = jnp.broadcast_to(a_r[first_row], blk_shape) * cr + jnp.broadcast_to(b_r[first_row], blk_shape)
        return cf, cr

    zero = jnp.zeros(blk_shape, F32)
    lax.fori_loop(0, nblk, block, (zero, zero), unroll=SCAN_UNROLL)

    h_c = hf_ref[ctx_f, :] + hr_ref[ctx_r, :]
    h_l = hf_ref[lat_f, :] + hr_ref[lat_r, :]
    oc_ref[...] = (jax.nn.gelu(yc_ref[...].astype(F32)) * h_c).astype(oc_ref.dtype)
    ol_ref[...] = (jax.nn.gelu(yl_ref[...].astype(F32)) * h_l).astype(ol_ref.dtype)


def _rglru(xin, conv_w, conv_b, w_gate, b_gate, lam, j):
    dh, nh = RNN_HEAD_DIM, RNN_HEADS
    n_c = conv_w.shape[0]
    ctx_blk0 = M_LAT // CTX_LEN
    wg = w_gate.reshape(n_c, 4, nh, dh, dh)
    bg = b_gate.reshape(n_c, 4, nh, 1, dh)
    out_c, out_l = pl.pallas_call(
        _rglru_kernel,
        grid=(BATCH, nh),
        in_specs=[
            pl.BlockSpec((CTX_LEN, dh), lambda b, h: (ctx_blk0 + b, nh + h)),
            pl.BlockSpec((SEQ, dh), lambda b, h: (b, nh + h)),
            pl.BlockSpec((CTX_LEN, dh), lambda b, h: (ctx_blk0 + b, h)),
            pl.BlockSpec((SEQ, dh), lambda b, h: (b, h)),
            pl.BlockSpec((None, RNN_CONV, dh), lambda b, h: (j, 0, h)),
            pl.BlockSpec((None, 1, dh), lambda b, h: (j, 0, h)),
            pl.BlockSpec((None, 4, None, dh, dh), lambda b, h: (j, 0, h, 0, 0)),
            pl.BlockSpec((None, 4, None, 1, dh), lambda b, h: (j, 0, h, 0, 0)),
            pl.BlockSpec((None, 2, dh), lambda b, h: (j, 0, h)),
        ],
        out_specs=[
            pl.BlockSpec((CTX_LEN, dh), lambda b, h: (b, h)),
            pl.BlockSpec((SEQ, dh), lambda b, h: (b, h)),
        ],
        out_shape=[
            jax.ShapeDtypeStruct((M_CTX, RNN_WIDTH), BF16),
            jax.ShapeDtypeStruct((M_ALL, RNN_WIDTH), BF16),
        ],
        scratch_shapes=[pltpu.VMEM((RNN_LEN, dh), F32)] * 6,
        compiler_params=_params(2),
        name="rglru",
    )(xin, xin, xin, xin, conv_w, conv_b.reshape(n_c, 1, RNN_WIDTH), wg, bg, lam)
    return lax.dynamic_update_slice(out_l, out_c, (M_LAT, 0))


def kernel(x, c, ctx, c_ctx, ada_w, ada_b, norm_g, ffn_w_up, ffn_conv_w, ffn_conv_b, ffn_w_down,
           a_w_in, a_g_v, a_w_s, a_b_s, a_w_out, b_w_qkv, b_rpb, b_w_out,
           c_w_in, c_conv_w, c_conv_b, c_w_gate, c_b_gate, c_lam, c_w_out, final_g):
    h = jnp.concatenate([x.reshape(M_LAT, D_MODEL), ctx.reshape(M_CTX, D_MODEL)], axis=0)
    cvec = jnp.concatenate(
        [c, c_ctx[None], jnp.zeros((SUBLANES - BATCH - 1, D_MODEL), F32)], axis=0)
    mod = _ada_mod(cvec, ada_w, ada_b)

    for i in range(DEPTH):
        last = i == DEPTH - 1
        kind, j = i % N_MIXERS, i // N_MIXERS
        ffn_rows = M_LAT if last else M_ALL
        n_rows = M_LAT if (last and kind == 0) else M_ALL

        n = _norm_mod(h, norm_g, mod, i, 0, n_rows)
        if kind == 0:
            z = _matmul(n, a_w_in, j, n_rows=n_rows, tn=1024, out_dtype=BF16, act="gelu")
            y = _sgu(z, a_g_v, a_w_s, a_b_s, j, n_rows=n_rows)
            w_out = a_w_out
        elif kind == 1:
            qkv = _matmul(n, b_w_qkv, j, n_rows=n_rows, tn=1024, out_dtype=BF16)
            y = _na_attention(qkv, b_rpb[j])
            w_out = b_w_out
        else:
            xin = _matmul(n, c_w_in, j, n_rows=n_rows, tn=1024, out_dtype=F32)
            y = _rglru(xin, c_conv_w, c_conv_b, c_w_gate, c_b_gate, c_lam, j)
            w_out = c_w_out
        h = _matmul_resid(y, w_out, j, h, mod, i, 2, n_rows=n_rows, tn=1024)

        n = _norm_mod(h, norm_g, mod, i, 1, ffn_rows)
        a = _ffn_up(n, ffn_w_up, ffn_conv_w, ffn_conv_b, i, n_rows=ffn_rows)
        h = _matmul_resid(a, ffn_w_down, i, h, mod, i, 5, n_rows=ffn_rows, tn=512)

    return _final_norm(h, final_g).reshape(BATCH, SEQ, D_MODEL)
```

```python
import functools

import jax
import jax.numpy as jnp
from jax import lax
from jax.experimental import pallas as pl
from jax.experimental.pallas import tpu as pltpu

F32 = jnp.float32
BF16 = jnp.bfloat16

D_MODEL = 2048
BATCH = 2
SEQ = 4096
DEPTH = 4
GRID_W = 64
CTX_LEN = 256
N_MIXERS = 3
N_MOD = 6
EPS = 1e-6
NEG_INF = -1e30
LOG2_E = 1.4426950408889634
D_FF = 5632
CHUNK = 128
A_WIDTH = 2 * D_MODEL
A_GROUPS = 16
NA_HEADS = 16
NA_HEAD_DIM = D_MODEL // NA_HEADS
NA_KH = 8
NA_KW = 16
RNN_WIDTH = D_MODEL
RNN_HEADS = 16
RNN_HEAD_DIM = RNN_WIDTH // RNN_HEADS
RNN_CONV = 4
RG_C = 8.0

M_LAT = BATCH * SEQ
M_CTX = BATCH * CTX_LEN
M_ALL = M_LAT + M_CTX
GRID_ROWS = SEQ // GRID_W

VMEM_LIMIT_BYTES = 56 * 1024 * 1024
SUBLANES = 8
LANES = 128
BF16_ROWS = 16
MXU_WIDTH = 256

ROW_TILE = {M_ALL: M_ALL // 8, M_LAT: M_LAT // 8}
ROW_TILE_WIDE_K = {M_ALL: M_ALL // 16, M_LAT: M_LAT // 16}
NORM_ROW_TILE = 512
NORM_CHUNK = BF16_ROWS


def _params(n_axes, flags=None):
    return pltpu.CompilerParams(
        dimension_semantics=("arbitrary",) * n_axes, vmem_limit_bytes=VMEM_LIMIT_BYTES, flags=flags)


def _sigmoid(x):
    return 0.5 * jnp.tanh(0.5 * x) + 0.5


def _mod_spec(layer, which, tn=D_MODEL, grid_rank=1):
    if grid_rank == 1:
        return pl.BlockSpec((None, SUBLANES, tn), lambda i: (layer * N_MOD + which, 0, 0))
    return pl.BlockSpec((None, SUBLANES, tn), lambda j, i: (layer * N_MOD + which, 0, j))


def _rows_of_group(mod_ref, row0, tm):
    r = row0 + lax.broadcasted_iota(jnp.int32, (tm, 1), 0)
    m = mod_ref[...]
    out = m[BATCH:BATCH + 1]
    for b in reversed(range(BATCH)):
        out = jnp.where(r < (b + 1) * SEQ, m[b:b + 1], out)
    return out


def _ada_kernel(c_ref, w_ref, b_ref, o_ref):
    c = c_ref[...]
    s = (c * _sigmoid(c)).astype(BF16)
    o_ref[...] = jnp.dot(s, w_ref[...].astype(BF16), preferred_element_type=F32) + b_ref[...]


def _ada_mod(cvec, ada_w, ada_b):
    depth, d, n = ada_w.shape
    tn = 1024
    out = pl.pallas_call(
        _ada_kernel,
        grid=(depth, n // tn),
        in_specs=[
            pl.BlockSpec((SUBLANES, d), lambda l, j: (0, 0)),
            pl.BlockSpec((None, d, tn), lambda l, j: (l, 0, j)),
            pl.BlockSpec((None, 1, tn), lambda l, j: (l, 0, j)),
        ],
        out_specs=pl.BlockSpec((None, SUBLANES, tn), lambda l, j: (l, 0, j)),
        out_shape=jax.ShapeDtypeStruct((depth, SUBLANES, n), F32),
        compiler_params=_params(2),
        name="ada_mod",
    )(cvec, ada_w, ada_b.reshape(depth, 1, n))
    return out.reshape(depth, SUBLANES, N_MOD, d).transpose(0, 2, 1, 3).reshape(depth * N_MOD, SUBLANES, d)


def _norm_mod_kernel(h_ref, g_ref, shift_ref, scale_ref, o_ref):
    group = pl.ds(jnp.minimum(pl.program_id(0) * h_ref.shape[0] // SEQ, BATCH), 1)
    gain = g_ref[...] * (1 + scale_ref[group, :])
    shift = shift_ref[group, :]

    def chunk(r, carry):
        rows = pl.ds(pl.multiple_of(r * NORM_CHUNK, NORM_CHUNK), NORM_CHUNK)
        x = h_ref[rows, :]
        y = x * lax.rsqrt(jnp.mean(x * x, axis=-1, keepdims=True) + EPS)
        o_ref[rows, :] = (y * gain + shift).astype(o_ref.dtype)
        return carry

    lax.fori_loop(0, h_ref.shape[0] // NORM_CHUNK, chunk, 0, unroll=True)


def _norm_mod(h, norm_g, mod, layer, which_norm, n_rows):
    tm = NORM_ROW_TILE
    g = norm_g.reshape(DEPTH * 2, 1, D_MODEL)
    return pl.pallas_call(
        _norm_mod_kernel,
        grid=(n_rows // tm,),
        in_specs=[
            pl.BlockSpec((tm, D_MODEL), lambda i: (i, 0)),
            pl.BlockSpec((None, 1, D_MODEL), lambda i: (layer * 2 + which_norm, 0, 0)),
            _mod_spec(layer, 3 * which_norm),
            _mod_spec(layer, 3 * which_norm + 1),
        ],
        out_specs=pl.BlockSpec((tm, D_MODEL), lambda i: (i, 0)),
        out_shape=jax.ShapeDtypeStruct((M_ALL, D_MODEL), BF16),
        compiler_params=_params(1),
        name="norm_mod",
    )(h, g, mod, mod)


def _final_norm_kernel(h_ref, g_ref, o_ref):
    gain = g_ref[...]

    def chunk(r, carry):
        rows = pl.ds(pl.multiple_of(r * NORM_CHUNK, NORM_CHUNK), NORM_CHUNK)
        x = h_ref[rows, :]
        o_ref[rows, :] = x * lax.rsqrt(jnp.mean(x * x, axis=-1, keepdims=True) + EPS) * gain
        return carry

    lax.fori_loop(0, h_ref.shape[0] // NORM_CHUNK, chunk, 0, unroll=True)


def _final_norm(h, final_g):
    tm = NORM_ROW_TILE
    return pl.pallas_call(
        _final_norm_kernel,
        grid=(M_LAT // tm,),
        in_specs=[
            pl.BlockSpec((tm, D_MODEL), lambda i: (i, 0)),
            pl.BlockSpec((1, D_MODEL), lambda i: (0, 0)),
        ],
        out_specs=pl.BlockSpec((tm, D_MODEL), lambda i: (i, 0)),
        out_shape=jax.ShapeDtypeStruct((M_LAT, D_MODEL), F32),
        compiler_params=_params(1),
        name="final_norm",
    )(h, final_g.reshape(1, D_MODEL))


def _mm_kernel(x_ref, w_ref, o_ref, wbf_ref, *, act):
    @pl.when(pl.program_id(1) == 0)
    def _():
        wbf_ref[...] = w_ref[...].astype(BF16)

    acc = jnp.dot(x_ref[...], wbf_ref[...], preferred_element_type=F32)
    if act == "gelu":
        acc = jax.nn.gelu(acc)
    o_ref[...] = acc.astype(o_ref.dtype)


def _mm_resid_kernel(x_ref, w_ref, gate_ref, res_ref, o_ref, wbf_ref):
    i = pl.program_id(1)
    tm = x_ref.shape[0]

    @pl.when(i == 0)
    def _():
        wbf_ref[...] = w_ref[...].astype(BF16)

    acc = jnp.dot(x_ref[...], wbf_ref[...], preferred_element_type=F32)
    o_ref[...] = res_ref[...] + _rows_of_group(gate_ref, i * tm, tm) * acc


def _matmul(x, w, w_idx, *, n_rows, tn, out_dtype, act=None):
    tm = ROW_TILE[n_rows]
    _, k, n = w.shape
    return pl.pallas_call(
        functools.partial(_mm_kernel, act=act),
        grid=(n // tn, n_rows // tm),
        in_specs=[
            pl.BlockSpec((tm, k), lambda j, i: (i, 0)),
            pl.BlockSpec((None, k, tn), lambda j, i: (w_idx, 0, j)),
        ],
        out_specs=pl.BlockSpec((tm, tn), lambda j, i: (i, j)),
        out_shape=jax.ShapeDtypeStruct((x.shape[0], n), out_dtype),
        scratch_shapes=[pltpu.VMEM((k, tn), BF16)],
        compiler_params=_params(2),
        name="matmul",
    )(x, w)


def _matmul_resid(x, w, w_idx, h, mod, layer, which_gate, *, n_rows, tn):
    _, k, n = w.shape
    w_tile_bytes = k * tn * 4
    tm = (ROW_TILE_WIDE_K if w_tile_bytes > 8 * 2**20 else ROW_TILE)[n_rows]
    w_buffers = pl.Buffered(1) if w_tile_bytes > 12 * 2**20 else None
    return pl.pallas_call(
        _mm_resid_kernel,
        grid=(n // tn, n_rows // tm),
        in_specs=[
            pl.BlockSpec((tm, k), lambda j, i: (i, 0)),
            pl.BlockSpec((None, k, tn), lambda j, i: (w_idx, 0, j), pipeline_mode=w_buffers),
            _mod_spec(layer, which_gate, tn, grid_rank=2),
            pl.BlockSpec((tm, tn), lambda j, i: (i, j)),
        ],
        out_specs=pl.BlockSpec((tm, tn), lambda j, i: (i, j)),
        out_shape=jax.ShapeDtypeStruct(h.shape, F32),
        scratch_shapes=[pltpu.VMEM((k, tn), BF16)],
        input_output_aliases={3: 0},
        compiler_params=_params(2),
        name="matmul_resid",
    )(x, w, mod, h)


HALO = BF16_ROWS


SEQ_STARTS = tuple(b * SEQ for b in range(BATCH)) + tuple(M_LAT + b * CTX_LEN for b in range(BATCH))
FIX_ROWS = BF16_ROWS


def _is_seq_edge(row):
    edge = row == M_ALL
    for s in SEQ_STARTS:
        edge = edge | (row == s)
    return edge


def _conv3(prev, cur, nxt, cw_ref, cb_ref):
    cw = cw_ref[...]
    y = cb_ref[...] + prev * cw[0:1]
    y = y + cur * cw[1:2]
    return y + nxt * cw[2:3]


def _silu_gate(g, v):
    half_g = 0.5 * g
    return (half_g * jnp.tanh(half_g) + half_g) * v


def _ffn_up_kernel(x_ref, xp_ref, xn_ref, wg_ref, wv_ref, cwg_ref, cwv_ref, cbg_ref, cbv_ref,
                   o_ref, wg_bf, wv_bf, xs_ref, zg_ref, zv_ref):
    i = pl.program_id(1)
    tm = x_ref.shape[0]
    row0 = i * tm

    @pl.when(i == 0)
    def _():
        wg_bf[...] = wg_ref[...].astype(BF16)
        wv_bf[...] = wv_ref[...].astype(BF16)

    no_rows = jnp.zeros_like(xp_ref)
    xs_ref[0:HALO, :] = jnp.where(_is_seq_edge(row0), no_rows, xp_ref[...])
    xs_ref[HALO:HALO + tm, :] = x_ref[...]
    xs_ref[HALO + tm:, :] = jnp.where(_is_seq_edge(row0 + tm), no_rows, xn_ref[...])
    xs = xs_ref[...]
    zg_ref[...] = jnp.dot(xs, wg_bf[...], preferred_element_type=F32)
    zv_ref[...] = jnp.dot(xs, wv_bf[...], preferred_element_type=F32)

    def conv(z_ref, cw_ref, cb_ref):
        return _conv3(z_ref[HALO - 1:HALO - 1 + tm, :], z_ref[HALO:HALO + tm, :],
                      z_ref[HALO + 1:HALO + 1 + tm, :], cw_ref, cb_ref)

    o_ref[...] = _silu_gate(conv(zg_ref, cwg_ref, cbg_ref), conv(zv_ref, cwv_ref, cbv_ref)).astype(o_ref.dtype)

    n_win = 2 * FIX_ROWS + 2 * SUBLANES
    w_row = lax.broadcasted_iota(jnp.int32, (n_win, 1), 0)
    start_row = SUBLANES + FIX_ROWS

    def fixed_conv(z_ref, cw_ref, cb_ref, win):
        z = z_ref[win, :]
        prev = jnp.where(w_row == start_row, 0.0, pltpu.roll(z, 1, 0))
        nxt = jnp.where(w_row == start_row - 1, 0.0, pltpu.roll(z, n_win - 1, 0))
        return _conv3(prev, z, nxt, cw_ref, cb_ref)[SUBLANES:SUBLANES + 2 * FIX_ROWS]

    for s in SEQ_STARTS[1:]:
        @pl.when((s > row0) & (s < row0 + tm))
        def _():
            win = pl.ds(pl.multiple_of(HALO + s - row0 - start_row, SUBLANES), n_win)
            rows = pl.ds(pl.multiple_of(s - row0 - FIX_ROWS, FIX_ROWS), 2 * FIX_ROWS)
            g = fixed_conv(zg_ref, cwg_ref, cbg_ref, win)
            v = fixed_conv(zv_ref, cwv_ref, cbv_ref, win)
            o_ref[rows, :] = _silu_gate(g, v).astype(o_ref.dtype)


def _ffn_up(n, w_up, conv_w, conv_b, layer, *, n_rows):
    tm, tn = ROW_TILE[n_rows], 512
    k = w_up.shape[1]
    nj = D_FF // tn
    halo_per_tile = tm // HALO
    last_halo = M_ALL // HALO - 1
    cb = conv_b.reshape(DEPTH, 1, 2 * D_FF)
    return pl.pallas_call(
        _ffn_up_kernel,
        grid=(nj, n_rows // tm),
        in_specs=[
            pl.BlockSpec((tm, k), lambda j, i: (i, 0)),
            pl.BlockSpec((HALO, k), lambda j, i: (jnp.maximum(i * halo_per_tile - 1, 0), 0)),
            pl.BlockSpec((HALO, k), lambda j, i: (jnp.minimum((i + 1) * halo_per_tile, last_halo), 0)),
            pl.BlockSpec((None, k, tn), lambda j, i: (layer, 0, j)),
            pl.BlockSpec((None, k, tn), lambda j, i: (layer, 0, j + nj)),
            pl.BlockSpec((None, 3, tn), lambda j, i: (layer, 0, j)),
            pl.BlockSpec((None, 3, tn), lambda j, i: (layer, 0, j + nj)),
            pl.BlockSpec((None, 1, tn), lambda j, i: (layer, 0, j)),
            pl.BlockSpec((None, 1, tn), lambda j, i: (layer, 0, j + nj)),
        ],
        out_specs=pl.BlockSpec((tm, tn), lambda j, i: (i, j)),
        out_shape=jax.ShapeDtypeStruct((M_ALL, D_FF), BF16),
        scratch_shapes=[
            pltpu.VMEM((k, tn), BF16),
            pltpu.VMEM((k, tn), BF16),
            pltpu.VMEM((tm + 2 * HALO, k), BF16),
            pltpu.VMEM((tm + 2 * HALO, tn), F32),
            pltpu.VMEM((tm + 2 * HALO, tn), F32),
        ],
        compiler_params=_params(2),
        name="ffn_up_conv_gate",
    )(n, n, n, w_up, w_up, conv_w, conv_w, cb, cb)


def _sgu_kernel(u_ref, v_ref, gv_ref, ws_ref, bs_ref, o_ref):
    tm = u_ref.shape[0]
    gw = A_WIDTH // A_GROUPS
    gv = gv_ref[...]
    for c in range(tm // CHUNK):
        rows = slice(c * CHUNK, (c + 1) * CHUNK)
        v = v_ref[rows, :].astype(F32)
        vn = v * lax.rsqrt(jnp.mean(v * v, axis=-1, keepdims=True) + EPS) * gv
        vn = vn.astype(BF16)
        for g in range(A_GROUPS):
            cols = slice(g * gw, (g + 1) * gw)
            s = jnp.dot(ws_ref[g].astype(BF16), vn[:, cols], preferred_element_type=F32)
            s = s + bs_ref[:, g:g + 1]
            o_ref[rows, cols] = (u_ref[rows, cols].astype(F32) * s).astype(o_ref.dtype)


def _sgu(z, g_v, w_s, b_s, j, *, n_rows):
    tm = 4 * CHUNK
    n_a = g_v.shape[0]
    return pl.pallas_call(
        _sgu_kernel,
        grid=(n_rows // tm,),
        in_specs=[
            pl.BlockSpec((tm, A_WIDTH), lambda i: (i, 0)),
            pl.BlockSpec((tm, A_WIDTH), lambda i: (i, 1)),
            pl.BlockSpec((None, 1, A_WIDTH), lambda i: (j, 0, 0)),
            pl.BlockSpec((None, A_GROUPS, CHUNK, CHUNK), lambda i: (j, 0, 0, 0)),
            pl.BlockSpec((None, CHUNK, A_GROUPS), lambda i: (j, 0, 0)),
        ],
        out_specs=pl.BlockSpec((tm, A_WIDTH), lambda i: (i, 0)),
        out_shape=jax.ShapeDtypeStruct((M_ALL, A_WIDTH), BF16),
        compiler_params=_params(1),
        name="sgu",
    )(z, z, g_v.reshape(n_a, 1, A_WIDTH), w_s, b_s.transpose(0, 2, 1))


NA_DR = 2 * NA_KH - 1
NA_DC = 2 * NA_KW - 1
NA_PAIR_TILES = NA_DR + 1
NA_QROWS = 4
NA_KROWS = NA_KH + NA_QROWS
NA_BLOCK_UNROLL = 7


def _na_bias_pairs(rpb_ref, head, pair_ref):
    shape = (GRID_W, 2 * GRID_W)
    q = lax.broadcasted_iota(jnp.int32, shape, 0)
    lane = lax.broadcasted_iota(jnp.int32, shape, 1)
    kcol = lane & (GRID_W - 1)
    upper = lane >= GRID_W
    dc = jnp.clip(kcol - q, -(NA_KW - 1), NA_KW - 1) + NA_KW - 1
    c0 = jnp.clip(q - NA_KW // 2, 0, GRID_W - NA_KW)
    ok = (kcol >= c0) & (kcol < c0 + NA_KW)
    for d in range(-1, NA_DR):
        acc = jnp.full(shape, NEG_INF, F32)
        for c in range(NA_DC):
            lo = rpb_ref[head, d * NA_DC + c] if d >= 0 else NEG_INF
            hi = rpb_ref[head, (d + 1) * NA_DC + c] if d + 1 < NA_DR else NEG_INF
            acc = jnp.where(dc == c, jnp.where(upper, hi, lo), acc)
        pair_ref[d + 1] = jnp.where(ok, acc * LOG2_E, NEG_INF)


def _na_window_start(m):
    return min(max(m * NA_QROWS - NA_KH // 2, 0), GRID_ROWS - NA_KROWS)


def _na_block_bias(pair_ref, m):
    upper = lax.broadcasted_iota(jnp.int32, (GRID_W, 2 * GRID_W), 1) >= GRID_W
    masked = jnp.full((GRID_W, 2 * GRID_W), NEG_INF, F32)
    ws = _na_window_start(m)
    rows = []
    for a in range(NA_QROWS):
        r = m * NA_QROWS + a
        r0 = min(max(r - NA_KH // 2, 0), GRID_ROWS - NA_KH)
        tiles = []
        for p in range(NA_KROWS // 2):
            key_row = ws + 2 * p
            in_lo = r0 <= key_row < r0 + NA_KH
            in_hi = r0 <= key_row + 1 < r0 + NA_KH
            if not (in_lo or in_hi):
                tiles.append(masked)
                continue
            tile = pair_ref[key_row - r + NA_KH]
            if not in_hi:
                tile = jnp.where(upper, NEG_INF, tile)
            elif not in_lo:
                tile = jnp.where(upper, tile, NEG_INF)
            tiles.append(tile)
        rows.append(jnp.concatenate(tiles, axis=1))
    return jnp.concatenate(rows, axis=0)


def _na_lat_kernel(rpb_ref, q_ref, k_ref, v_ref, kc_ref, vc_ref, o_ref, pair_ref, bias_ref):
    scale = NA_HEAD_DIM ** -0.5 * LOG2_E

    @pl.when(pl.program_id(1) == 0)
    def _():
        _na_bias_pairs(rpb_ref, pl.program_id(0), pair_ref)
        bias_ref[...] = _na_block_bias(pair_ref, 1)

    kc = kc_ref[...]
    vc = vc_ref[...]
    nt = (((1,), (1,)), ((), ()))
    n_q = NA_QROWS * GRID_W
    n_blocks = GRID_ROWS // NA_QROWS

    def attend(qrows, win, bias):
        q = q_ref[qrows, :]
        kw = k_ref[win, :]
        vw = v_ref[win, :]
        s = lax.dot_general(q, kw, nt, preferred_element_type=F32) * scale
        s = jnp.where(bias > 0.5 * NEG_INF, s + bias, NEG_INF)
        sc = lax.dot_general(q, kc, nt, preferred_element_type=F32) * scale
        mx = jnp.maximum(jnp.max(s, axis=-1, keepdims=True), jnp.max(sc, axis=-1, keepdims=True))
        p_w = jnp.exp2(s - mx)
        p_c = jnp.exp2(sc - mx)
        denom = jnp.sum(p_w, axis=-1, keepdims=True) + jnp.sum(p_c, axis=-1, keepdims=True)
        o = jnp.dot(p_w.astype(BF16), vw, preferred_element_type=F32)
        o = o + jnp.dot(p_c.astype(BF16), vc, preferred_element_type=F32)
        o_ref[qrows, :] = (o / denom).astype(o_ref.dtype)

    def edge_block(m):
        ws = _na_window_start(m)
        attend(slice(m * n_q, (m + 1) * n_q), slice(ws * GRID_W, (ws + NA_KROWS) * GRID_W),
               _na_block_bias(pair_ref, m))

    def interior_block(m, carry):
        qrows = pl.ds(pl.multiple_of(m * n_q, n_q), n_q)
        ws = m * NA_QROWS - NA_KH // 2
        win = pl.ds(pl.multiple_of(ws * GRID_W, GRID_W), NA_KROWS * GRID_W)
        attend(qrows, win, bias_ref[...])
        return carry

    edge_block(0)
    lax.fori_loop(1, n_blocks - 1, interior_block, 0, unroll=NA_BLOCK_UNROLL)
    edge_block(n_blocks - 1)


def _na_ctx_kernel(q_ref, k_ref, v_ref, o_ref):
    scale = NA_HEAD_DIM ** -0.5
    s = lax.dot_general(q_ref[...], k_ref[...], (((1,), (1,)), ((), ())),
                        preferred_element_type=F32) * scale
    m = jnp.max(s, axis=-1, keepdims=True)
    p = jnp.exp(s - m)
    denom = jnp.sum(p, axis=-1, keepdims=True)
    o = jnp.dot(p.astype(BF16), v_ref[...], preferred_element_type=F32)
    o_ref[...] = (o / denom).astype(o_ref.dtype)


def _na_attention(qkv, rpb):
    dh, nh = NA_HEAD_DIM, NA_HEADS
    ctx_blk0 = M_LAT // CTX_LEN
    lat = pl.pallas_call(
        _na_lat_kernel,
        grid=(nh, BATCH),
        in_specs=[
            pl.BlockSpec(memory_space=pltpu.SMEM),
            pl.BlockSpec((SEQ, dh), lambda h, b: (b, h)),
            pl.BlockSpec((SEQ, dh), lambda h, b: (b, nh + h)),
            pl.BlockSpec((SEQ, dh), lambda h, b: (b, 2 * nh + h)),
            pl.BlockSpec((CTX_LEN, dh), lambda h, b: (ctx_blk0 + b, nh + h)),
            pl.BlockSpec((CTX_LEN, dh), lambda h, b: (ctx_blk0 + b, 2 * nh + h)),
        ],
        out_specs=pl.BlockSpec((SEQ, dh), lambda h, b: (b, h)),
        out_shape=jax.ShapeDtypeStruct((M_ALL, D_MODEL), BF16),
        scratch_shapes=[
            pltpu.VMEM((NA_PAIR_TILES, GRID_W, 2 * GRID_W), F32),
            pltpu.VMEM((NA_QROWS * GRID_W, NA_KROWS * GRID_W), F32),
        ],
        compiler_params=_params(2),
        name="na_latent",
    )(rpb.reshape(nh, NA_DR * NA_DC), qkv, qkv, qkv, qkv, qkv)
    ctx = pl.pallas_call(
        _na_ctx_kernel,
        grid=(BATCH, nh),
        in_specs=[
            pl.BlockSpec((CTX_LEN, dh), lambda b, h: (ctx_blk0 + b, h)),
            pl.BlockSpec((CTX_LEN, dh), lambda b, h: (ctx_blk0 + b, nh + h)),
            pl.BlockSpec((CTX_LEN, dh), lambda b, h: (ctx_blk0 + b, 2 * nh + h)),
        ],
        out_specs=pl.BlockSpec((CTX_LEN, dh), lambda b, h: (b, h)),
        out_shape=jax.ShapeDtypeStruct((M_CTX, D_MODEL), BF16),
        compiler_params=_params(2),
        name="na_context",
    )(qkv, qkv, qkv)
    return lax.dynamic_update_slice(lat, ctx, (M_LAT, 0))


RNN_LEN = CTX_LEN + SEQ
SCAN_UNROLL = 8


def _shift_rows(x, offset, length):
    if offset == 0:
        return x
    t = lax.broadcasted_iota(jnp.int32, (length, 1), 0)
    rolled = pltpu.roll(x, (-offset) % length, 0)
    return jnp.where((t + offset >= 0) & (t + offset < length), rolled, 0.0)


def _rnn_conv(x, cw, cb):
    length = x.shape[0]
    left = RNN_CONV // 2
    y = cb
    for j in range(RNN_CONV):
        y = y + _shift_rows(x, j - left, length) * cw[j:j + 1]
    return y


def _block_scan(a, b, reverse):
    row = lax.broadcasted_iota(jnp.int32, a.shape, 0)
    for s in (1, 2, 4):
        if reverse:
            keep = row < SUBLANES - s
            shift = SUBLANES - s
        else:
            keep = row >= s
            shift = s
        a_sh = pltpu.roll(a, shift, 0)
        b_sh = pltpu.roll(b, shift, 0)
        b = jnp.where(keep, a * b_sh + b, b)
        a = jnp.where(keep, a * a_sh, a)
    return a, b


def _rglru_kernel(xc_ref, xl_ref, yc_ref, yl_ref, cw_ref, cb_ref, wg_ref, bg_ref, lam_ref,
                  oc_ref, ol_ref, af_ref, bf_ref, ar_ref, br_ref, hf_ref, hr_ref):
    cw = cw_ref[...]
    cb = cb_ref[...]
    xr_c = _rnn_conv(xc_ref[...].astype(F32), cw, cb)
    xr_l = _rnn_conv(xl_ref[...].astype(F32), cw, cb)

    def half_tanh_gate(xb, d, gate):
        w = (0.5 * wg_ref[2 * d + gate]).astype(BF16)
        return jnp.tanh(jnp.dot(xb, w, preferred_element_type=F32) + 0.5 * bg_ref[2 * d + gate])

    def gates(xr, d, a_ref, b_ref, rows):
        xb = xr.astype(BF16)
        half_xr = 0.5 * xr
        t_r = half_tanh_gate(xb, d, 0)
        t_i = half_tanh_gate(xb, d, 1)
        neg_lam = -lam_ref[d:d + 1, :]
        softplus = jnp.maximum(neg_lam, 0.0) + jnp.log1p(jnp.exp(-jnp.abs(neg_lam)))
        c = (-0.5 * RG_C * LOG2_E) * softplus
        a = jnp.exp2(c * t_r + c)
        a_ref[rows, :] = a
        y = 1.0 - a * a
        root = jnp.where(y > 0.0, y * lax.rsqrt(y), 0.0)
        b_ref[rows, :] = root * (half_xr * t_i + half_xr)

    ctx_f, lat_f = slice(0, CTX_LEN), slice(CTX_LEN, RNN_LEN)
    lat_r, ctx_r = slice(0, SEQ), slice(SEQ, RNN_LEN)
    gates(xr_c, 0, af_ref, bf_ref, ctx_f)
    gates(xr_l, 0, af_ref, bf_ref, lat_f)
    gates(xr_c, 1, ar_ref, br_ref, ctx_r)
    gates(xr_l, 1, ar_ref, br_ref, lat_r)

    nblk = RNN_LEN // SUBLANES
    last_row = slice(SUBLANES - 1, SUBLANES)
    first_row = slice(0, 1)
    blk_shape = (SUBLANES, RNN_HEAD_DIM)

    def block(n, carry):
        cf, cr = carry
        rows_f = pl.ds(pl.multiple_of(n * SUBLANES, SUBLANES), SUBLANES)
        rows_r = pl.ds(pl.multiple_of((nblk - 1 - n) * SUBLANES, SUBLANES), SUBLANES)
        a_f, b_f = _block_scan(af_ref[rows_f, :], bf_ref[rows_f, :], reverse=False)
        a_r, b_r = _block_scan(ar_ref[rows_r, :], br_ref[rows_r, :], reverse=True)
        hf_ref[rows_f, :] = a_f * cf + b_f
        hr_ref[rows_r, :] = a_r * cr + b_r
        cf = jnp.broadcast_to(a_f[last_row], blk_shape) * cf + jnp.broadcast_to(b_f[last_row], blk_shape)
        cr = jnp.broadcast_to(a_r[first_row], blk_shape) * cr + jnp.broadcast_to(b_r[first_row], blk_shape)
        return cf, cr

    zero = jnp.zeros(blk_shape, F32)
    lax.fori_loop(0, nblk, block, (zero, zero), unroll=SCAN_UNROLL)

    h_c = hf_ref[ctx_f, :] + hr_ref[ctx_r, :]
    h_l = hf_ref[lat_f, :] + hr_ref[lat_r, :]
    oc_ref[...] = (jax.nn.gelu(yc_ref[...].astype(F32)) * h_c).astype(oc_ref.dtype)
    ol_ref[...] = (jax.nn.gelu(yl_ref[...].astype(F32)) * h_l).astype(ol_ref.dtype)


def _rglru(xin, conv_w, conv_b, w_gate, b_gate, lam, j):
    dh, nh = RNN_HEAD_DIM, RNN_HEADS
    n_c = conv_w.shape[0]
    ctx_blk0 = M_LAT // CTX_LEN
    wg = w_gate.reshape(n_c, 4, nh, dh, dh)
    bg = b_gate.reshape(n_c, 4, nh, 1, dh)
    out_c, out_l = pl.pallas_call(
        _rglru_kernel,
        grid=(BATCH, nh),
        in_specs=[
            pl.BlockSpec((CTX_LEN, dh), lambda b, h: (ctx_blk0 + b, nh + h)),
            pl.BlockSpec((SEQ, dh), lambda b, h: (b, nh + h)),
            pl.BlockSpec((CTX_LEN, dh), lambda b, h: (ctx_blk0 + b, h)),
            pl.BlockSpec((SEQ, dh), lambda b, h: (b, h)),
            pl.BlockSpec((None, RNN_CONV, dh), lambda b, h: (j, 0, h)),
            pl.BlockSpec((None, 1, dh), lambda b, h: (j, 0, h)),
            pl.BlockSpec((None, 4, None, dh, dh), lambda b, h: (j, 0, h, 0, 0)),
            pl.BlockSpec((None, 4, None, 1, dh), lambda b, h: (j, 0, h, 0, 0)),
            pl.BlockSpec((None, 2, dh), lambda b, h: (j, 0, h)),
        ],
        out_specs=[
            pl.BlockSpec((CTX_LEN, dh), lambda b, h: (b, h)),
            pl.BlockSpec((SEQ, dh), lambda b, h: (b, h)),
        ],
        out_shape=[
            jax.ShapeDtypeStruct((M_CTX, RNN_WIDTH), BF16),
            jax.ShapeDtypeStruct((M_ALL, RNN_WIDTH), BF16),
        ],
        scratch_shapes=[pltpu.VMEM((RNN_LEN, dh), F32)] * 6,
        compiler_params=_params(2),
        name="rglru",
    )(xin, xin, xin, xin, conv_w, conv_b.reshape(n_c, 1, RNN_WIDTH), wg, bg, lam)
    return lax.dynamic_update_slice(out_l, out_c, (M_LAT, 0))


def kernel(x, c, ctx, c_ctx, ada_w, ada_b, norm_g, ffn_w_up, ffn_conv_w, ffn_conv_b, ffn_w_down,
           a_w_in, a_g_v, a_w_s, a_b_s, a_w_out, b_w_qkv, b_rpb, b_w_out,
           c_w_in, c_conv_w, c_conv_b, c_w_gate, c_b_gate, c_lam, c_w_out, final_g):
    h = jnp.concatenate([x.reshape(M_LAT, D_MODEL), ctx.reshape(M_CTX, D_MODEL)], axis=0)
    cvec = jnp.concatenate(
        [c, c_ctx[None], jnp.zeros((SUBLANES - BATCH - 1, D_MODEL), F32)], axis=0)
    mod = _ada_mod(cvec, ada_w, ada_b)

    for i in range(DEPTH):
        last = i == DEPTH - 1
        kind, j = i % N_MIXERS, i // N_MIXERS
        ffn_rows = M_LAT if last else M_ALL
        n_rows = M_LAT if (last and kind == 0) else M_ALL

        n = _norm_mod(h, norm_g, mod, i, 0, n_rows)
        if kind == 0:
            z = _matmul(n, a_w_in, j, n_rows=n_rows, tn=1024, out_dtype=BF16, act="gelu")
            y = _sgu(z, a_g_v, a_w_s, a_b_s, j, n_rows=n_rows)
            w_out = a_w_out
        elif kind == 1:
            qkv = _matmul(n, b_w_qkv, j, n_rows=n_rows, tn=1024, out_dtype=BF16)
            y = _na_attention(qkv, b_rpb[j])
            w_out = b_w_out
        else:
            xin = _matmul(n, c_w_in, j, n_rows=n_rows, tn=1024, out_dtype=F32)
            y = _rglru(xin, c_conv_w, c_conv_b, c_w_gate, c_b_gate, c_lam, j)
            w_out = c_w_out
        h = _matmul_resid(y, w_out, j, h, mod, i, 2, n_rows=n_rows, tn=1024)

        n = _norm_mod(h, norm_g, mod, i, 1, ffn_rows)
        a = _ffn_up(n, ffn_w_up, ffn_conv_w, ffn_conv_b, i, n_rows=ffn_rows)
        h = _matmul_resid(a, ffn_w_down, i, h, mod, i, 5, n_rows=ffn_rows, tn=512)

    return _final_norm(h, final_g).reshape(BATCH, SEQ, D_MODEL)
```

```python
import functools

import jax
import jax.numpy as jnp
from jax import lax
from jax.experimental import pallas as pl
from jax.experimental.pallas import tpu as pltpu

F32 = jnp.float32
BF16 = jnp.bfloat16

D_MODEL = 2048
BATCH = 2
SEQ = 4096
DEPTH = 4
GRID_W = 64
CTX_LEN = 256
N_MIXERS = 3
N_MOD = 6
EPS = 1e-6
NEG_INF = -1e30
LOG2_E = 1.4426950408889634
D_FF = 5632
CHUNK = 128
A_WIDTH = 2 * D_MODEL
A_GROUPS = 16
NA_HEADS = 16
NA_HEAD_DIM = D_MODEL // NA_HEADS
NA_KH = 8
NA_KW = 16
RNN_WIDTH = D_MODEL
RNN_HEADS = 16
RNN_HEAD_DIM = RNN_WIDTH // RNN_HEADS
RNN_CONV = 4
RG_C = 8.0

M_LAT = BATCH * SEQ
M_CTX = BATCH * CTX_LEN
M_ALL = M_LAT + M_CTX
GRID_ROWS = SEQ // GRID_W

VMEM_LIMIT_BYTES = 56 * 1024 * 1024
SUBLANES = 8
LANES = 128
BF16_ROWS = 16
MXU_WIDTH = 256

ROW_TILE = {M_ALL: M_ALL // 8, M_LAT: M_LAT // 8}
ROW_TILE_WIDE_K = {M_ALL: M_ALL // 16, M_LAT: M_LAT // 16}
NORM_ROW_TILE = 512
NORM_CHUNK = BF16_ROWS


def _params(n_axes, flags=None):
    return pltpu.CompilerParams(
        dimension_semantics=("arbitrary",) * n_axes, vmem_limit_bytes=VMEM_LIMIT_BYTES, flags=flags)


def _sigmoid(x):
    return 0.5 * jnp.tanh(0.5 * x) + 0.5


def _mod_spec(layer, which, tn=D_MODEL, grid_rank=1):
    if grid_rank == 1:
        return pl.BlockSpec((None, SUBLANES, tn), lambda i: (layer * N_MOD + which, 0, 0))
    return pl.BlockSpec((None, SUBLANES, tn), lambda j, i: (layer * N_MOD + which, 0, j))


def _rows_of_group(mod_ref, row0, tm):
    r = row0 + lax.broadcasted_iota(jnp.int32, (tm, 1), 0)
    m = mod_ref[...]
    out = m[BATCH:BATCH + 1]
    for b in reversed(range(BATCH)):
        out = jnp.where(r < (b + 1) * SEQ, m[b:b + 1], out)
    return out


def _ada_kernel(c_ref, w_ref, b_ref, o_ref):
    c = c_ref[...]
    s = (c * _sigmoid(c)).astype(BF16)
    o_ref[...] = jnp.dot(s, w_ref[...].astype(BF16), preferred_element_type=F32) + b_ref[...]


def _ada_mod(cvec, ada_w, ada_b):
    depth, d, n = ada_w.shape
    tn = 1024
    out = pl.pallas_call(
        _ada_kernel,
        grid=(depth, n // tn),
        in_specs=[
            pl.BlockSpec((SUBLANES, d), lambda l, j: (0, 0)),
            pl.BlockSpec((None, d, tn), lambda l, j: (l, 0, j)),
            pl.BlockSpec((None, 1, tn), lambda l, j: (l, 0, j)),
        ],
        out_specs=pl.BlockSpec((None, SUBLANES, tn), lambda l, j: (l, 0, j)),
        out_shape=jax.ShapeDtypeStruct((depth, SUBLANES, n), F32),
        compiler_params=_params(2),
        name="ada_mod",
    )(cvec, ada_w, ada_b.reshape(depth, 1, n))
    return out.reshape(depth, SUBLANES, N_MOD, d).transpose(0, 2, 1, 3).reshape(depth * N_MOD, SUBLANES, d)


def _norm_mod_kernel(h_ref, g_ref, shift_ref, scale_ref, o_ref):
    group = pl.ds(jnp.minimum(pl.program_id(0) * h_ref.shape[0] // SEQ, BATCH), 1)
    gain = g_ref[...] * (1 + scale_ref[group, :])
    shift = shift_ref[group, :]

    def chunk(r, carry):
        rows = pl.ds(pl.multiple_of(r * NORM_CHUNK, NORM_CHUNK), NORM_CHUNK)
        x = h_ref[rows, :]
        y = x * lax.rsqrt(jnp.mean(x * x, axis=-1, keepdims=True) + EPS)
        o_ref[rows, :] = (y * gain + shift).astype(o_ref.dtype)
        return carry

    lax.fori_loop(0, h_ref.shape[0] // NORM_CHUNK, chunk, 0, unroll=True)


def _norm_mod(h, norm_g, mod, layer, which_norm, n_rows):
    tm = NORM_ROW_TILE
    g = norm_g.reshape(DEPTH * 2, 1, D_MODEL)
    return pl.pallas_call(
        _norm_mod_kernel,
        grid=(n_rows // tm,),
        in_specs=[
            pl.BlockSpec((tm, D_MODEL), lambda i: (i, 0)),
            pl.BlockSpec((None, 1, D_MODEL), lambda i: (layer * 2 + which_norm, 0, 0)),
            _mod_spec(layer, 3 * which_norm),
            _mod_spec(layer, 3 * which_norm + 1),
        ],
        out_specs=pl.BlockSpec((tm, D_MODEL), lambda i: (i, 0)),
        out_shape=jax.ShapeDtypeStruct((M_ALL, D_MODEL), BF16),
        compiler_params=_params(1),
        name="norm_mod",
    )(h, g, mod, mod)


def _final_norm_kernel(h_ref, g_ref, o_ref):
    gain = g_ref[...]

    def chunk(r, carry):
        rows = pl.ds(pl.multiple_of(r * NORM_CHUNK, NORM_CHUNK), NORM_CHUNK)
        x = h_ref[rows, :]
        o_ref[rows, :] = x * lax.rsqrt(jnp.mean(x * x, axis=-1, keepdims=True) + EPS) * gain
        return carry

    lax.fori_loop(0, h_ref.shape[0] // NORM_CHUNK, chunk, 0, unroll=True)


def _final_norm(h, final_g):
    tm = NORM_ROW_TILE
    return pl.pallas_call(
        _final_norm_kernel,
        grid=(M_LAT // tm,),
        in_specs=[
            pl.BlockSpec((tm, D_MODEL), lambda i: (i, 0)),
            pl.BlockSpec((1, D_MODEL), lambda i: (0, 0)),
        ],
        out_specs=pl.BlockSpec((tm, D_MODEL), lambda i: (i, 0)),
        out_shape=jax.ShapeDtypeStruct((M_LAT, D_MODEL), F32),
        compiler_params=_params(1),
        name="final_norm",
    )(h, final_g.reshape(1, D_MODEL))


def _mm_kernel(x_ref, w_ref, o_ref, wbf_ref, *, act):
    @pl.when(pl.program_id(1) == 0)
    def _():
        wbf_ref[...] = w_ref[...].astype(BF16)

    acc = jnp.dot(x_ref[...], wbf_ref[...], preferred_element_type=F32)
    if act == "gelu":
        acc = jax.nn.gelu(acc)
    o_ref[...] = acc.astype(o_ref.dtype)


def _mm_resid_kernel(x_ref, w_ref, gate_ref, res_ref, o_ref, wbf_ref):
    i = pl.program_id(1)
    tm = x_ref.shape[0]

    @pl.when(i == 0)
    def _():
        wbf_ref[...] = w_ref[...].astype(BF16)

    acc = jnp.dot(x_ref[...], wbf_ref[...], preferred_element_type=F32)
    o_ref[...] = res_ref[...] + _rows_of_group(gate_ref, i * tm, tm) * acc


def _matmul(x, w, w_idx, *, n_rows, tn, out_dtype, act=None):
    tm = ROW_TILE[n_rows]
    _, k, n = w.shape
    return pl.pallas_call(
        functools.partial(_mm_kernel, act=act),
        grid=(n // tn, n_rows // tm),
        in_specs=[
            pl.BlockSpec((tm, k), lambda j, i: (i, 0)),
            pl.BlockSpec((None, k, tn), lambda j, i: (w_idx, 0, j)),
        ],
        out_specs=pl.BlockSpec((tm, tn), lambda j, i: (i, j)),
        out_shape=jax.ShapeDtypeStruct((x.shape[0], n), out_dtype),
        scratch_shapes=[pltpu.VMEM((k, tn), BF16)],
        compiler_params=_params(2),
        name="matmul",
    )(x, w)


def _matmul_resid(x, w, w_idx, h, mod, layer, which_gate, *, n_rows, tn):
    _, k, n = w.shape
    w_tile_bytes = k * tn * 4
    tm = (ROW_TILE_WIDE_K if w_tile_bytes > 8 * 2**20 else ROW_TILE)[n_rows]
    w_buffers = pl.Buffered(1) if w_tile_bytes > 12 * 2**20 else None
    return pl.pallas_call(
        _mm_resid_kernel,
        grid=(n // tn, n_rows // tm),
        in_specs=[
            pl.BlockSpec((tm, k), lambda j, i: (i, 0)),
            pl.BlockSpec((None, k, tn), lambda j, i: (w_idx, 0, j), pipeline_mode=w_buffers),
            _mod_spec(layer, which_gate, tn, grid_rank=2),
            pl.BlockSpec((tm, tn), lambda j, i: (i, j)),
        ],
        out_specs=pl.BlockSpec((tm, tn), lambda j, i: (i, j)),
        out_shape=jax.ShapeDtypeStruct(h.shape, F32),
        scratch_shapes=[pltpu.VMEM((k, tn), BF16)],
        input_output_aliases={3: 0},
        compiler_params=_params(2),
        name="matmul_resid",
    )(x, w, mod, h)


HALO = BF16_ROWS


SEQ_STARTS = tuple(b * SEQ for b in range(BATCH)) + tuple(M_LAT + b * CTX_LEN for b in range(BATCH))
FIX_ROWS = BF16_ROWS


def _is_seq_edge(row):
    edge = row == M_ALL
    for s in SEQ_STARTS:
        edge = edge | (row == s)
    return edge


def _conv3(prev, cur, nxt, cw_ref, cb_ref):
    cw = cw_ref[...]
    y = cb_ref[...] + prev * cw[0:1]
    y = y + cur * cw[1:2]
    return y + nxt * cw[2:3]


def _silu_gate(g, v):
    half_g = 0.5 * g
    return (half_g * jnp.tanh(half_g) + half_g) * v


def _ffn_up_kernel(x_ref, xp_ref, xn_ref, wg_ref, wv_ref, cwg_ref, cwv_ref, cbg_ref, cbv_ref,
                   o_ref, wg_bf, wv_bf, xs_ref, zg_ref, zv_ref):
    i = pl.program_id(1)
    tm = x_ref.shape[0]
    row0 = i * tm

    @pl.when(i == 0)
    def _():
        wg_bf[...] = wg_ref[...].astype(BF16)
        wv_bf[...] = wv_ref[...].astype(BF16)

    no_rows = jnp.zeros_like(xp_ref)
    xs_ref[0:HALO, :] = jnp.where(_is_seq_edge(row0), no_rows, xp_ref[...])
    xs_ref[HALO:HALO + tm, :] = x_ref[...]
    xs_ref[HALO + tm:, :] = jnp.where(_is_seq_edge(row0 + tm), no_rows, xn_ref[...])
    xs = xs_ref[...]
    zg_ref[...] = jnp.dot(xs, wg_bf[...], preferred_element_type=F32)
    zv_ref[...] = jnp.dot(xs, wv_bf[...], preferred_element_type=F32)

    def conv(z_ref, cw_ref, cb_ref):
        return _conv3(z_ref[HALO - 1:HALO - 1 + tm, :], z_ref[HALO:HALO + tm, :],
                      z_ref[HALO + 1:HALO + 1 + tm, :], cw_ref, cb_ref)

    o_ref[...] = _silu_gate(conv(zg_ref, cwg_ref, cbg_ref), conv(zv_ref, cwv_ref, cbv_ref)).astype(o_ref.dtype)

    n_win = 2 * FIX_ROWS + 2 * SUBLANES
    w_row = lax.broadcasted_iota(jnp.int32, (n_win, 1), 0)
    start_row = SUBLANES + FIX_ROWS

    def fixed_conv(z_ref, cw_ref, cb_ref, win):
        z = z_ref[win, :]
        prev = jnp.where(w_row == start_row, 0.0, pltpu.roll(z, 1, 0))
        nxt = jnp.where(w_row == start_row - 1, 0.0, pltpu.roll(z, n_win - 1, 0))
        return _conv3(prev, z, nxt, cw_ref, cb_ref)[SUBLANES:SUBLANES + 2 * FIX_ROWS]

    for s in SEQ_STARTS[1:]:
        @pl.when((s > row0) & (s < row0 + tm))
        def _():
            win = pl.ds(pl.multiple_of(HALO + s - row0 - start_row, SUBLANES), n_win)
            rows = pl.ds(pl.multiple_of(s - row0 - FIX_ROWS, FIX_ROWS), 2 * FIX_ROWS)
            g = fixed_conv(zg_ref, cwg_ref, cbg_ref, win)
            v = fixed_conv(zv_ref, cwv_ref, cbv_ref, win)
            o_ref[rows, :] = _silu_gate(g, v).astype(o_ref.dtype)


def _ffn_up(n, w_up, conv_w, conv_b, layer, *, n_rows):
    tm, tn = ROW_TILE[n_rows], 512
    k = w_up.shape[1]
    nj = D_FF // tn
    halo_per_tile = tm // HALO
    last_halo = M_ALL // HALO - 1
    cb = conv_b.reshape(DEPTH, 1, 2 * D_FF)
    return pl.pallas_call(
        _ffn_up_kernel,
        grid=(nj, n_rows // tm),
        in_specs=[
            pl.BlockSpec((tm, k), lambda j, i: (i, 0)),
            pl.BlockSpec((HALO, k), lambda j, i: (jnp.maximum(i * halo_per_tile - 1, 0), 0)),
            pl.BlockSpec((HALO, k), lambda j, i: (jnp.minimum((i + 1) * halo_per_tile, last_halo), 0)),
            pl.BlockSpec((None, k, tn), lambda j, i: (layer, 0, j)),
            pl.BlockSpec((None, k, tn), lambda j, i: (layer, 0, j + nj)),
            pl.BlockSpec((None, 3, tn), lambda j, i: (layer, 0, j)),
            pl.BlockSpec((None, 3, tn), lambda j, i: (layer, 0, j + nj)),
            pl.BlockSpec((None, 1, tn), lambda j, i: (layer, 0, j)),
            pl.BlockSpec((None, 1, tn), lambda j, i: (layer, 0, j + nj)),
        ],
        out_specs=pl.BlockSpec((tm, tn), lambda j, i: (i, j)),
        out_shape=jax.ShapeDtypeStruct((M_ALL, D_FF), BF16),
        scratch_shapes=[
            pltpu.VMEM((k, tn), BF16),
            pltpu.VMEM((k, tn), BF16),
            pltpu.VMEM((tm + 2 * HALO, k), BF16),
            pltpu.VMEM((tm + 2 * HALO, tn), F32),
            pltpu.VMEM((tm + 2 * HALO, tn), F32),
        ],
        compiler_params=_params(2),
        name="ffn_up_conv_gate",
    )(n, n, n, w_up, w_up, conv_w, conv_w, cb, cb)


def _sgu_kernel(u_ref, v_ref, gv_ref, ws_ref, bs_ref, o_ref):
    tm = u_ref.shape[0]
    gw = A_WIDTH // A_GROUPS
    gv = gv_ref[...]
    for c in range(tm // CHUNK):
        rows = slice(c * CHUNK, (c + 1) * CHUNK)
        v = v_ref[rows, :].astype(F32)
        vn = v * lax.rsqrt(jnp.mean(v * v, axis=-1, keepdims=True) + EPS) * gv
        vn = vn.astype(BF16)
        for g in range(A_GROUPS):
            cols = slice(g * gw, (g + 1) * gw)
            s = jnp.dot(ws_ref[g].astype(BF16), vn[:, cols], preferred_element_type=F32)
            s = s + bs_ref[:, g:g + 1]
            o_ref[rows, cols] = (u_ref[rows, cols].astype(F32) * s).astype(o_ref.dtype)


def _sgu(z, g_v, w_s, b_s, j, *, n_rows):
    tm = 4 * CHUNK
    n_a = g_v.shape[0]
    return pl.pallas_call(
        _sgu_kernel,
        grid=(n_rows // tm,),
        in_specs=[
            pl.BlockSpec((tm, A_WIDTH), lambda i: (i, 0)),
            pl.BlockSpec((tm, A_WIDTH), lambda i: (i, 1)),
            pl.BlockSpec((None, 1, A_WIDTH), lambda i: (j, 0, 0)),
            pl.BlockSpec((None, A_GROUPS, CHUNK, CHUNK), lambda i: (j, 0, 0, 0)),
            pl.BlockSpec((None, CHUNK, A_GROUPS), lambda i: (j, 0, 0)),
        ],
        out_specs=pl.BlockSpec((tm, A_WIDTH), lambda i: (i, 0)),
        out_shape=jax.ShapeDtypeStruct((M_ALL, A_WIDTH), BF16),
        compiler_params=_params(1),
        name="sgu",
    )(z, z, g_v.reshape(n_a, 1, A_WIDTH), w_s, b_s.transpose(0, 2, 1))


NA_DR = 2 * NA_KH - 1
NA_DC = 2 * NA_KW - 1
NA_PAIR_TILES = NA_DR + 1
NA_QROWS = 4
NA_KROWS = NA_KH + NA_QROWS
NA_BLOCK_UNROLL = 7


def _na_bias_pairs(rpb_ref, head, pair_ref):
    shape = (GRID_W, 2 * GRID_W)
    q = lax.broadcasted_iota(jnp.int32, shape, 0)
    lane = lax.broadcasted_iota(jnp.int32, shape, 1)
    kcol = lane & (GRID_W - 1)
    upper = lane >= GRID_W
    dc = jnp.clip(kcol - q, -(NA_KW - 1), NA_KW - 1) + NA_KW - 1
    c0 = jnp.clip(q - NA_KW // 2, 0, GRID_W - NA_KW)
    ok = (kcol >= c0) & (kcol < c0 + NA_KW)
    for d in range(-1, NA_DR):
        acc = jnp.full(shape, NEG_INF, F32)
        for c in range(NA_DC):
            lo = rpb_ref[head, d * NA_DC + c] if d >= 0 else NEG_INF
            hi = rpb_ref[head, (d + 1) * NA_DC + c] if d + 1 < NA_DR else NEG_INF
            acc = jnp.where(dc == c, jnp.where(upper, hi, lo), acc)
        pair_ref[d + 1] = jnp.where(ok, acc * LOG2_E, NEG_INF)


def _na_window_start(m):
    return min(max(m * NA_QROWS - NA_KH // 2, 0), GRID_ROWS - NA_KROWS)


def _na_block_bias(pair_ref, m):
    upper = lax.broadcasted_iota(jnp.int32, (GRID_W, 2 * GRID_W), 1) >= GRID_W
    masked = jnp.full((GRID_W, 2 * GRID_W), NEG_INF, F32)
    ws = _na_window_start(m)
    rows = []
    for a in range(NA_QROWS):
        r = m * NA_QROWS + a
        r0 = min(max(r - NA_KH // 2, 0), GRID_ROWS - NA_KH)
        tiles = []
        for p in range(NA_KROWS // 2):
            key_row = ws + 2 * p
            in_lo = r0 <= key_row < r0 + NA_KH
            in_hi = r0 <= key_row + 1 < r0 + NA_KH
            if not (in_lo or in_hi):
                tiles.append(masked)
                continue
            tile = pair_ref[key_row - r + NA_KH]
            if not in_hi:
                tile = jnp.where(upper, NEG_INF, tile)
            elif not in_lo:
                tile = jnp.where(upper, tile, NEG_INF)
            tiles.append(tile)
        rows.append(jnp.concatenate(tiles, axis=1))
    return jnp.concatenate(rows, axis=0)


def _na_lat_kernel(rpb_ref, q_ref, k_ref, v_ref, kc_ref, vc_ref, o_ref, pair_ref, bias_ref):
    scale = NA_HEAD_DIM ** -0.5 * LOG2_E

    @pl.when(pl.program_id(1) == 0)
    def _():
        _na_bias_pairs(rpb_ref, pl.program_id(0), pair_ref)
        bias_ref[...] = _na_block_bias(pair_ref, 1)

    kc = kc_ref[...]
    vc = vc_ref[...]
    nt = (((1,), (1,)), ((), ()))
    n_q = NA_QROWS * GRID_W
    n_blocks = GRID_ROWS // NA_QROWS

    def attend(qrows, win, bias):
        q = q_ref[qrows, :]
        kw = k_ref[win, :]
        vw = v_ref[win, :]
        s = lax.dot_general(q, kw, nt, preferred_element_type=F32) * scale + bias
        sc = lax.dot_general(q, kc, nt, preferred_element_type=F32) * scale
        mx = jnp.maximum(jnp.max(s, axis=-1, keepdims=True), jnp.max(sc, axis=-1, keepdims=True))
        p_w = jnp.exp2(s - mx)
        p_c = jnp.exp2(sc - mx)
        denom = jnp.sum(p_w, axis=-1, keepdims=True) + jnp.sum(p_c, axis=-1, keepdims=True)
        o = jnp.dot(p_w.astype(BF16), vw, preferred_element_type=F32)
        o = o + jnp.dot(p_c.astype(BF16), vc, preferred_element_type=F32)
        o_ref[qrows, :] = (o / denom).astype(o_ref.dtype)

    def edge_block(m):
        ws = _na_window_start(m)
        attend(slice(m * n_q, (m + 1) * n_q), slice(ws * GRID_W, (ws + NA_KROWS) * GRID_W),
               _na_block_bias(pair_ref, m))

    def interior_block(m, carry):
        qrows = pl.ds(pl.multiple_of(m * n_q, n_q), n_q)
        ws = m * NA_QROWS - NA_KH // 2
        win = pl.ds(pl.multiple_of(ws * GRID_W, GRID_W), NA_KROWS * GRID_W)
        attend(qrows, win, bias_ref[...])
        return carry

    edge_block(0)
    lax.fori_loop(1, n_blocks - 1, interior_block, 0, unroll=NA_BLOCK_UNROLL)
    edge_block(n_blocks - 1)


def _na_ctx_kernel(q_ref, k_ref, v_ref, o_ref):
    scale = NA_HEAD_DIM ** -0.5
    s = lax.dot_general(q_ref[...], k_ref[...], (((1,), (1,)), ((), ())),
                        preferred_element_type=F32) * scale
    m = jnp.max(s, axis=-1, keepdims=True)
    p = jnp.exp(s - m)
    denom = jnp.sum(p, axis=-1, keepdims=True)
    o = jnp.dot(p.astype(BF16), v_ref[...], preferred_element_type=F32)
    o_ref[...] = (o / denom).astype(o_ref.dtype)


def _na_attention(qkv, rpb):
    dh, nh = NA_HEAD_DIM, NA_HEADS
    ctx_blk0 = M_LAT // CTX_LEN
    lat = pl.pallas_call(
        _na_lat_kernel,
        grid=(nh, BATCH),
        in_specs=[
            pl.BlockSpec(memory_space=pltpu.SMEM),
            pl.BlockSpec((SEQ, dh), lambda h, b: (b, h)),
            pl.BlockSpec((SEQ, dh), lambda h, b: (b, nh + h)),
            pl.BlockSpec((SEQ, dh), lambda h, b: (b, 2 * nh + h)),
            pl.BlockSpec((CTX_LEN, dh), lambda h, b: (ctx_blk0 + b, nh + h)),
            pl.BlockSpec((CTX_LEN, dh), lambda h, b: (ctx_blk0 + b, 2 * nh + h)),
        ],
        out_specs=pl.BlockSpec((SEQ, dh), lambda h, b: (b, h)),
        out_shape=jax.ShapeDtypeStruct((M_ALL, D_MODEL), BF16),
        scratch_shapes=[
            pltpu.VMEM((NA_PAIR_TILES, GRID_W, 2 * GRID_W), F32),
            pltpu.VMEM((NA_QROWS * GRID_W, NA_KROWS * GRID_W), F32),
        ],
        compiler_params=_params(2),
        name="na_latent",
    )(rpb.reshape(nh, NA_DR * NA_DC), qkv, qkv, qkv, qkv, qkv)
    ctx = pl.pallas_call(
        _na_ctx_kernel,
        grid=(BATCH, nh),
        in_specs=[
            pl.BlockSpec((CTX_LEN, dh), lambda b, h: (ctx_blk0 + b, h)),
            pl.BlockSpec((CTX_LEN, dh), lambda b, h: (ctx_blk0 + b, nh + h)),
            pl.BlockSpec((CTX_LEN, dh), lambda b, h: (ctx_blk0 + b, 2 * nh + h)),
        ],
        out_specs=pl.BlockSpec((CTX_LEN, dh), lambda b, h: (b, h)),
        out_shape=jax.ShapeDtypeStruct((M_CTX, D_MODEL), BF16),
        compiler_params=_params(2),
        name="na_context",
    )(qkv, qkv, qkv)
    return lax.dynamic_update_slice(lat, ctx, (M_LAT, 0))


RNN_LEN = CTX_LEN + SEQ
SCAN_UNROLL = 8


def _shift_rows(x, offset, length):
    if offset == 0:
        return x
    t = lax.broadcasted_iota(jnp.int32, (length, 1), 0)
    rolled = pltpu.roll(x, (-offset) % length, 0)
    return jnp.where((t + offset >= 0) & (t + offset < length), rolled, 0.0)


def _rnn_conv(x, cw, cb):
    length = x.shape[0]
    left = RNN_CONV // 2
    y = cb
    for j in range(RNN_CONV):
        y = y + _shift_rows(x, j - left, length) * cw[j:j + 1]
    return y


def _block_scan(a, b, reverse):
    row = lax.broadcasted_iota(jnp.int32, a.shape, 0)
    for s in (1, 2, 4):
        if reverse:
            keep = row < SUBLANES - s
            shift = SUBLANES - s
        else:
            keep = row >= s
            shift = s
        a_sh = pltpu.roll(a, shift, 0)
        b_sh = pltpu.roll(b, shift, 0)
        b = jnp.where(keep, a * b_sh + b, b)
        a = jnp.where(keep, a * a_sh, a)
    return a, b


def _rglru_kernel(xc_ref, xl_ref, yc_ref, yl_ref, cw_ref, cb_ref, wg_ref, bg_ref, lam_ref,
                  oc_ref, ol_ref, af_ref, bf_ref, ar_ref, br_ref, hf_ref, hr_ref):
    cw = cw_ref[...]
    cb = cb_ref[...]
    xr_c = _rnn_conv(xc_ref[...].astype(F32), cw, cb)
    xr_l = _rnn_conv(xl_ref[...].astype(F32), cw, cb)

    def half_tanh_gate(xb, d, gate):
        w = (0.5 * wg_ref[2 * d + gate]).astype(BF16)
        return jnp.tanh(jnp.dot(xb, w, preferred_element_type=F32) + 0.5 * bg_ref[2 * d + gate])

    def gates(xr, d, a_ref, b_ref, rows):
        xb = xr.astype(BF16)
        half_xr = 0.5 * xr
        t_r = half_tanh_gate(xb, d, 0)
        t_i = half_tanh_gate(xb, d, 1)
        neg_lam = -lam_ref[d:d + 1, :]
        softplus = jnp.maximum(neg_lam, 0.0) + jnp.log1p(jnp.exp(-jnp.abs(neg_lam)))
        c = (-0.5 * RG_C * LOG2_E) * softplus
        a = jnp.exp2(c * t_r + c)
        a_ref[rows, :] = a
        y = 1.0 - a * a
        root = jnp.where(y > 0.0, y * lax.rsqrt(y), 0.0)
        b_ref[rows, :] = root * (half_xr * t_i + half_xr)

    ctx_f, lat_f = slice(0, CTX_LEN), slice(CTX_LEN, RNN_LEN)
    lat_r, ctx_r = slice(0, SEQ), slice(SEQ, RNN_LEN)
    gates(xr_c, 0, af_ref, bf_ref, ctx_f)
    gates(xr_l, 0, af_ref, bf_ref, lat_f)
    gates(xr_c, 1, ar_ref, br_ref, ctx_r)
    gates(xr_l, 1, ar_ref, br_ref, lat_r)

    nblk = RNN_LEN // SUBLANES
    last_row = slice(SUBLANES - 1, SUBLANES)
    first_row = slice(0, 1)
    blk_shape = (SUBLANES, RNN_HEAD_DIM)

    def block(n, carry):
        cf, cr = carry
        rows_f = pl.ds(pl.multiple_of(n * SUBLANES, SUBLANES), SUBLANES)
        rows_r = pl.ds(pl.multiple_of((nblk - 1 - n) * SUBLANES, SUBLANES), SUBLANES)
        a_f, b_f = _block_scan(af_ref[rows_f, :], bf_ref[rows_f, :], reverse=False)
        a_r, b_r = _block_scan(ar_ref[rows_r, :], br_ref[rows_r, :], reverse=True)
        hf_ref[rows_f, :] = a_f * cf + b_f
        hr_ref[rows_r, :] = a_r * cr + b_r
        cf = jnp.broadcast_to(a_f[last_row], blk_shape) * cf + jnp.broadcast_to(b_f[last_row], blk_shape)
        cr = jnp.broadcast_to(a_r[first_row], blk_shape) * cr + jnp.broadcast_to(b_r[first_row], blk_shape)
        return cf, cr

    zero = jnp.zeros(blk_shape, F32)
    lax.fori_loop(0, nblk, block, (zero, zero), unroll=SCAN_UNROLL)

    h_c = hf_ref[ctx_f, :] + hr_ref[ctx_r, :]
    h_l = hf_ref[lat_f, :] + hr_ref[lat_r, :]
    oc_ref[...] = (jax.nn.gelu(yc_ref[...].astype(F32)) * h_c).astype(oc_ref.dtype)
    ol_ref[...] = (jax.nn.gelu(yl_ref[...].astype(F32)) * h_l).astype(ol_ref.dtype)


def _rglru(xin, conv_w, conv_b, w_gate, b_gate, lam, j):
    dh, nh = RNN_HEAD_DIM, RNN_HEADS
    n_c = conv_w.shape[0]
    ctx_blk0 = M_LAT // CTX_LEN
    wg = w_gate.reshape(n_c, 4, nh, dh, dh)
    bg = b_gate.reshape(n_c, 4, nh, 1, dh)
    out_c, out_l = pl.pallas_call(
        _rglru_kernel,
        grid=(BATCH, nh),
        in_specs=[
            pl.BlockSpec((CTX_LEN, dh), lambda b, h: (ctx_blk0 + b, nh + h)),
            pl.BlockSpec((SEQ, dh), lambda b, h: (b, nh + h)),
            pl.BlockSpec((CTX_LEN, dh), lambda b, h: (ctx_blk0 + b, h)),
            pl.BlockSpec((SEQ, dh), lambda b, h: (b, h)),
            pl.BlockSpec((None, RNN_CONV, dh), lambda b, h: (j, 0, h)),
            pl.BlockSpec((None, 1, dh), lambda b, h: (j, 0, h)),
            pl.BlockSpec((None, 4, None, dh, dh), lambda b, h: (j, 0, h, 0, 0)),
            pl.BlockSpec((None, 4, None, 1, dh), lambda b, h: (j, 0, h, 0, 0)),
            pl.BlockSpec((None, 2, dh), lambda b, h: (j, 0, h)),
        ],
        out_specs=[
            pl.BlockSpec((CTX_LEN, dh), lambda b, h: (b, h)),
            pl.BlockSpec((SEQ, dh), lambda b, h: (b, h)),
        ],
        out_shape=[
            jax.ShapeDtypeStruct((M_CTX, RNN_WIDTH), BF16),
            jax.ShapeDtypeStruct((M_ALL, RNN_WIDTH), BF16),
        ],
        scratch_shapes=[pltpu.VMEM((RNN_LEN, dh), F32)] * 6,
        compiler_params=_params(2),
        name="rglru",
    )(xin, xin, xin, xin, conv_w, conv_b.reshape(n_c, 1, RNN_WIDTH), wg, bg, lam)
    return lax.dynamic_update_slice(out_l, out_c, (M_LAT, 0))


def kernel(x, c, ctx, c_ctx, ada_w, ada_b, norm_g, ffn_w_up, ffn_conv_w, ffn_conv_b, ffn_w_down,
           a_w_in, a_g_v, a_w_s, a_b_s, a_w_out, b_w_qkv, b_rpb, b_w_out,
           c_w_in, c_conv_w, c_conv_b, c_w_gate, c_b_gate, c_lam, c_w_out, final_g):
    h = jnp.concatenate([x.reshape(M_LAT, D_MODEL), ctx.reshape(M_CTX, D_MODEL)], axis=0)
    cvec = jnp.concatenate(
        [c, c_ctx[None], jnp.zeros((SUBLANES - BATCH - 1, D_MODEL), F32)], axis=0)
    mod = _ada_mod(cvec, ada_w, ada_b)

    for i in range(DEPTH):
        last = i == DEPTH - 1
        kind, j = i % N_MIXERS, i // N_MIXERS
        ffn_rows = M_LAT if last else M_ALL
        n_rows = M_LAT if (last and kind == 0) else M_ALL

        n = _norm_mod(h, norm_g, mod, i, 0, n_rows)
        if kind == 0:
            z = _matmul(n, a_w_in, j, n_rows=n_rows, tn=1024, out_dtype=BF16, act="gelu")
            y = _sgu(z, a_g_v, a_w_s, a_b_s, j, n_rows=n_rows)
            w_out = a_w_out
        elif kind == 1:
            qkv = _matmul(n, b_w_qkv, j, n_rows=n_rows, tn=1024, out_dtype=BF16)
            y = _na_attention(qkv, b_rpb[j])
            w_out = b_w_out
        else:
            xin = _matmul(n, c_w_in, j, n_rows=n_rows, tn=1024, out_dtype=F32)
            y = _rglru(xin, c_conv_w, c_conv_b, c_w_gate, c_b_gate, c_lam, j)
            w_out = c_w_out
        h = _matmul_resid(y, w_out, j, h, mod, i, 2, n_rows=n_rows, tn=1024)

        n = _norm_mod(h, norm_g, mod, i, 1, ffn_rows)
        a = _ffn_up(n, ffn_w_up, ffn_conv_w, ffn_conv_b, i, n_rows=ffn_rows)
        h = _matmul_resid(a, ffn_w_down, i, h, mod, i, 5, n_rows=ffn_rows, tn=512)

    return _final_norm(h, final_g).reshape(BATCH, SEQ, D_MODEL)
```

```python
import functools

import jax
import jax.numpy as jnp
from jax import lax
from jax.experimental import pallas as pl
from jax.experimental.pallas import tpu as pltpu

F32 = jnp.float32
BF16 = jnp.bfloat16

D_MODEL = 2048
BATCH = 2
SEQ = 4096
DEPTH = 4
GRID_W = 64
CTX_LEN = 256
N_MIXERS = 3
N_MOD = 6
EPS = 1e-6
NEG_INF = -1e30
LOG2_E = 1.4426950408889634
D_FF = 5632
CHUNK = 128
A_WIDTH = 2 * D_MODEL
A_GROUPS = 16
NA_HEADS = 16
NA_HEAD_DIM = D_MODEL // NA_HEADS
NA_KH = 8
NA_KW = 16
RNN_WIDTH = D_MODEL
RNN_HEADS = 16
RNN_HEAD_DIM = RNN_WIDTH // RNN_HEADS
RNN_CONV = 4
RG_C = 8.0

M_LAT = BATCH * SEQ
M_CTX = BATCH * CTX_LEN
M_ALL = M_LAT + M_CTX
GRID_ROWS = SEQ // GRID_W

VMEM_LIMIT_BYTES = 56 * 1024 * 1024
SUBLANES = 8
LANES = 128
BF16_ROWS = 16

ROW_TILE = {M_ALL: M_ALL // 8, M_LAT: M_LAT // 8}
ROW_TILE_WIDE_K = {M_ALL: M_ALL // 16, M_LAT: M_LAT // 16}
NORM_ROW_TILE = 512
NORM_CHUNK = BF16_ROWS


def _params(n_axes):
    return pltpu.CompilerParams(
        dimension_semantics=("arbitrary",) * n_axes, vmem_limit_bytes=VMEM_LIMIT_BYTES)


def _sigmoid(x):
    return 0.5 * jnp.tanh(0.5 * x) + 0.5


def _mod_spec(layer, which, tn=D_MODEL, grid_rank=1):
    if grid_rank == 1:
        return pl.BlockSpec((None, SUBLANES, tn), lambda i: (layer * N_MOD + which, 0, 0))
    return pl.BlockSpec((None, SUBLANES, tn), lambda j, i: (layer * N_MOD + which, 0, j))


def _rows_of_group(mod_ref, row0, tm):
    r = row0 + lax.broadcasted_iota(jnp.int32, (tm, 1), 0)
    m = mod_ref[...]
    out = m[BATCH:BATCH + 1]
    for b in reversed(range(BATCH)):
        out = jnp.where(r < (b + 1) * SEQ, m[b:b + 1], out)
    return out


def _split_stream_specs(tm, tn, row_of, col_of):
    assert M_LAT % tm == 0 and tm == M_CTX
    last_lat = M_LAT // tm - 1
    return [
        pl.BlockSpec((tm, tn), lambda *ids: (jnp.minimum(row_of(*ids), last_lat), col_of(*ids))),
        pl.BlockSpec((tm, tn), lambda *ids: (0, col_of(*ids))),
    ]


def _stream_tile(h_refs, i, tm, rows=slice(None)):
    if len(h_refs) == 1:
        return h_refs[0][rows, :]
    lat_ref, ctx_ref = h_refs
    return jnp.where(i < M_LAT // tm, lat_ref[rows, :], ctx_ref[rows, :])


def _ada_kernel(c_ref, w_ref, b_ref, o_ref):
    c = c_ref[...]
    s = (c * _sigmoid(c)).astype(BF16)
    o_ref[...] = jnp.dot(s, w_ref[...].astype(BF16), preferred_element_type=F32) + b_ref[...]


def _ada_mod(cvec, ada_w, ada_b):
    depth, d, n = ada_w.shape
    tn = 1024
    out = pl.pallas_call(
        _ada_kernel,
        grid=(depth, n // tn),
        in_specs=[
            pl.BlockSpec((SUBLANES, d), lambda l, j: (0, 0)),
            pl.BlockSpec((None, d, tn), lambda l, j: (l, 0, j)),
            pl.BlockSpec((None, 1, tn), lambda l, j: (l, 0, j)),
        ],
        out_specs=pl.BlockSpec((None, SUBLANES, tn), lambda l, j: (l, 0, j)),
        out_shape=jax.ShapeDtypeStruct((depth, SUBLANES, n), F32),
        compiler_params=_params(2),
        name="ada_mod",
    )(cvec, ada_w, ada_b.reshape(depth, 1, n))
    return out.reshape(depth, SUBLANES, N_MOD, d).transpose(0, 2, 1, 3).reshape(depth * N_MOD, SUBLANES, d)


def _norm_mod_kernel(*refs):
    *h_refs, g_ref, shift_ref, scale_ref, o_ref = refs
    i = pl.program_id(0)
    tm = o_ref.shape[0]
    group = pl.ds(jnp.minimum(i * tm // SEQ, BATCH), 1)
    gain = g_ref[...] * (1 + scale_ref[group, :])
    shift = shift_ref[group, :]

    def chunk(r, carry):
        rows = pl.ds(pl.multiple_of(r * NORM_CHUNK, NORM_CHUNK), NORM_CHUNK)
        x = _stream_tile(h_refs, i, tm, rows)
        y = x * lax.rsqrt(jnp.mean(x * x, axis=-1, keepdims=True) + EPS)
        o_ref[rows, :] = (y * gain + shift).astype(o_ref.dtype)
        return carry

    lax.fori_loop(0, tm // NORM_CHUNK, chunk, 0, unroll=True)


def _norm_mod(h, norm_g, mod, layer, which_norm, n_rows):
    tm = NORM_ROW_TILE
    g = norm_g.reshape(DEPTH * 2, 1, D_MODEL)
    h_arrays = h if isinstance(h, tuple) else (h,)
    h_specs = (_split_stream_specs(tm, D_MODEL, lambda i: i, lambda i: 0) if isinstance(h, tuple)
               else [pl.BlockSpec((tm, D_MODEL), lambda i: (i, 0))])
    return pl.pallas_call(
        _norm_mod_kernel,
        grid=(n_rows // tm,),
        in_specs=[
            *h_specs,
            pl.BlockSpec((None, 1, D_MODEL), lambda i: (layer * 2 + which_norm, 0, 0)),
            _mod_spec(layer, 3 * which_norm),
            _mod_spec(layer, 3 * which_norm + 1),
        ],
        out_specs=pl.BlockSpec((tm, D_MODEL), lambda i: (i, 0)),
        out_shape=jax.ShapeDtypeStruct((M_ALL, D_MODEL), BF16),
        compiler_params=_params(1),
        name="norm_mod",
    )(*h_arrays, g, mod, mod)


def _final_norm_kernel(h_ref, g_ref, o_ref):
    gain = g_ref[...]

    def chunk(r, carry):
        rows = pl.ds(pl.multiple_of(r * NORM_CHUNK, NORM_CHUNK), NORM_CHUNK)
        x = h_ref[rows, :]
        o_ref[rows, :] = x * lax.rsqrt(jnp.mean(x * x, axis=-1, keepdims=True) + EPS) * gain
        return carry

    lax.fori_loop(0, h_ref.shape[0] // NORM_CHUNK, chunk, 0, unroll=True)


def _final_norm(h, final_g):
    tm = NORM_ROW_TILE
    return pl.pallas_call(
        _final_norm_kernel,
        grid=(M_LAT // tm,),
        in_specs=[
            pl.BlockSpec((tm, D_MODEL), lambda i: (i, 0)),
            pl.BlockSpec((1, D_MODEL), lambda i: (0, 0)),
        ],
        out_specs=pl.BlockSpec((tm, D_MODEL), lambda i: (i, 0)),
        out_shape=jax.ShapeDtypeStruct((M_LAT, D_MODEL), F32),
        compiler_params=_params(1),
        name="final_norm",
    )(h, final_g.reshape(1, D_MODEL))


def _mm_kernel(x_ref, w_ref, o_ref, wbf_ref, *, act):
    @pl.when(pl.program_id(1) == 0)
    def _():
        wbf_ref[...] = w_ref[...].astype(BF16)

    acc = jnp.dot(x_ref[...], wbf_ref[...], preferred_element_type=F32)
    if act == "gelu":
        acc = jax.nn.gelu(acc)
    o_ref[...] = acc.astype(o_ref.dtype)


def _mm_resid_kernel(x_ref, w_ref, gate_ref, *refs):
    *res_refs, o_ref, wbf_ref = refs
    i = pl.program_id(1)
    tm = x_ref.shape[0]

    @pl.when(i == 0)
    def _():
        wbf_ref[...] = w_ref[...].astype(BF16)

    acc = jnp.dot(x_ref[...], wbf_ref[...], preferred_element_type=F32)
    o_ref[...] = _stream_tile(res_refs, i, tm) + _rows_of_group(gate_ref, i * tm, tm) * acc


def _matmul(x, w, w_idx, *, n_rows, tn, out_dtype, act=None):
    tm = ROW_TILE[n_rows]
    _, k, n = w.shape
    return pl.pallas_call(
        functools.partial(_mm_kernel, act=act),
        grid=(n // tn, n_rows // tm),
        in_specs=[
            pl.BlockSpec((tm, k), lambda j, i: (i, 0)),
            pl.BlockSpec((None, k, tn), lambda j, i: (w_idx, 0, j)),
        ],
        out_specs=pl.BlockSpec((tm, tn), lambda j, i: (i, j)),
        out_shape=jax.ShapeDtypeStruct((x.shape[0], n), out_dtype),
        scratch_shapes=[pltpu.VMEM((k, tn), BF16)],
        compiler_params=_params(2),
        name="matmul",
    )(x, w)


def _matmul_resid(x, w, w_idx, h, mod, layer, which_gate, *, n_rows, tn):
    _, k, n = w.shape
    w_tile_bytes = k * tn * 4
    w_buffers = pl.Buffered(1) if w_tile_bytes > 12 * 2**20 else None
    if isinstance(h, tuple):
        tm = M_CTX
        res_arrays, aliases = h, {}
        res_specs = _split_stream_specs(tm, tn, lambda j, i: i, lambda j, i: j)
    else:
        tm = (ROW_TILE_WIDE_K if w_tile_bytes > 8 * 2**20 else ROW_TILE)[n_rows]
        res_arrays, aliases = (h,), {3: 0}
        res_specs = [pl.BlockSpec((tm, tn), lambda j, i: (i, j))]
    return pl.pallas_call(
        _mm_resid_kernel,
        grid=(n // tn, n_rows // tm),
        in_specs=[
            pl.BlockSpec((tm, k), lambda j, i: (i, 0)),
            pl.BlockSpec((None, k, tn), lambda j, i: (w_idx, 0, j), pipeline_mode=w_buffers),
            _mod_spec(layer, which_gate, tn, grid_rank=2),
            *res_specs,
        ],
        out_specs=pl.BlockSpec((tm, tn), lambda j, i: (i, j)),
        out_shape=jax.ShapeDtypeStruct((M_ALL, n), F32),
        scratch_shapes=[pltpu.VMEM((k, tn), BF16)],
        input_output_aliases=aliases,
        compiler_params=_params(2),
        name="matmul_resid",
    )(x, w, mod, *res_arrays)


HALO = BF16_ROWS


SEQ_STARTS = tuple(b * SEQ for b in range(BATCH)) + tuple(M_LAT + b * CTX_LEN for b in range(BATCH))
FIX_ROWS = BF16_ROWS


def _is_seq_edge(row):
    edge = row == M_ALL
    for s in SEQ_STARTS:
        edge = edge | (row == s)
    return edge


def _conv3(prev, cur, nxt, cw_ref, cb_ref):
    cw = cw_ref[...]
    y = cb_ref[...] + prev * cw[0:1]
    y = y + cur * cw[1:2]
    return y + nxt * cw[2:3]


def _silu_gate(g, v):
    half_g = 0.5 * g
    return (half_g * jnp.tanh(half_g) + half_g) * v


def _ffn_up_kernel(x_ref, xp_ref, xn_ref, wg_ref, wv_ref, cwg_ref, cwv_ref, cbg_ref, cbv_ref,
                   o_ref, wg_bf, wv_bf, xs_ref, zg_ref, zv_ref):
    i = pl.program_id(1)
    tm = x_ref.shape[0]
    row0 = i * tm

    @pl.when(i == 0)
    def _():
        wg_bf[...] = wg_ref[...].astype(BF16)
        wv_bf[...] = wv_ref[...].astype(BF16)

    no_rows = jnp.zeros_like(xp_ref)
    xs_ref[0:HALO, :] = jnp.where(_is_seq_edge(row0), no_rows, xp_ref[...])
    xs_ref[HALO:HALO + tm, :] = x_ref[...]
    xs_ref[HALO + tm:, :] = jnp.where(_is_seq_edge(row0 + tm), no_rows, xn_ref[...])
    xs = xs_ref[...]
    zg_ref[...] = jnp.dot(xs, wg_bf[...], preferred_element_type=F32)
    zv_ref[...] = jnp.dot(xs, wv_bf[...], preferred_element_type=F32)

    def conv(z_ref, cw_ref, cb_ref):
        return _conv3(z_ref[HALO - 1:HALO - 1 + tm, :], z_ref[HALO:HALO + tm, :],
                      z_ref[HALO + 1:HALO + 1 + tm, :], cw_ref, cb_ref)

    o_ref[...] = _silu_gate(conv(zg_ref, cwg_ref, cbg_ref), conv(zv_ref, cwv_ref, cbv_ref)).astype(o_ref.dtype)

    n_win = 2 * FIX_ROWS + 2 * SUBLANES
    w_row = lax.broadcasted_iota(jnp.int32, (n_win, 1), 0)
    start_row = SUBLANES + FIX_ROWS

    def fixed_conv(z_ref, cw_ref, cb_ref, win):
        z = z_ref[win, :]
        prev = jnp.where(w_row == start_row, 0.0, pltpu.roll(z, 1, 0))
        nxt = jnp.where(w_row == start_row - 1, 0.0, pltpu.roll(z, n_win - 1, 0))
        return _conv3(prev, z, nxt, cw_ref, cb_ref)[SUBLANES:SUBLANES + 2 * FIX_ROWS]

    for s in SEQ_STARTS[1:]:
        @pl.when((s > row0) & (s < row0 + tm))
        def _():
            win = pl.ds(pl.multiple_of(HALO + s - row0 - start_row, SUBLANES), n_win)
            rows = pl.ds(pl.multiple_of(s - row0 - FIX_ROWS, FIX_ROWS), 2 * FIX_ROWS)
            g = fixed_conv(zg_ref, cwg_ref, cbg_ref, win)
            v = fixed_conv(zv_ref, cwv_ref, cbv_ref, win)
            o_ref[rows, :] = _silu_gate(g, v).astype(o_ref.dtype)


def _ffn_up(n, w_up, conv_w, conv_b, layer, *, n_rows):
    tm, tn = ROW_TILE[n_rows], 512
    k = w_up.shape[1]
    nj = D_FF // tn
    halo_per_tile = tm // HALO
    last_halo = M_ALL // HALO - 1
    cb = conv_b.reshape(DEPTH, 1, 2 * D_FF)
    return pl.pallas_call(
        _ffn_up_kernel,
        grid=(nj, n_rows // tm),
        in_specs=[
            pl.BlockSpec((tm, k), lambda j, i: (i, 0)),
            pl.BlockSpec((HALO, k), lambda j, i: (jnp.maximum(i * halo_per_tile - 1, 0), 0)),
            pl.BlockSpec((HALO, k), lambda j, i: (jnp.minimum((i + 1) * halo_per_tile, last_halo), 0)),
            pl.BlockSpec((None, k, tn), lambda j, i: (layer, 0, j)),
            pl.BlockSpec((None, k, tn), lambda j, i: (layer, 0, j + nj)),
            pl.BlockSpec((None, 3, tn), lambda j, i: (layer, 0, j)),
            pl.BlockSpec((None, 3, tn), lambda j, i: (layer, 0, j + nj)),
            pl.BlockSpec((None, 1, tn), lambda j, i: (layer, 0, j)),
            pl.BlockSpec((None, 1, tn), lambda j, i: (layer, 0, j + nj)),
        ],
        out_specs=pl.BlockSpec((tm, tn), lambda j, i: (i, j)),
        out_shape=jax.ShapeDtypeStruct((M_ALL, D_FF), BF16),
        scratch_shapes=[
            pltpu.VMEM((k, tn), BF16),
            pltpu.VMEM((k, tn), BF16),
            pltpu.VMEM((tm + 2 * HALO, k), BF16),
            pltpu.VMEM((tm + 2 * HALO, tn), F32),
            pltpu.VMEM((tm + 2 * HALO, tn), F32),
        ],
        compiler_params=_params(2),
        name="ffn_up_conv_gate",
    )(n, n, n, w_up, w_up, conv_w, conv_w, cb, cb)


def _sgu_kernel(u_ref, v_ref, gv_ref, ws_ref, bs_ref, o_ref):
    tm = u_ref.shape[0]
    gw = A_WIDTH // A_GROUPS
    gv = gv_ref[...]
    for c in range(tm // CHUNK):
        rows = slice(c * CHUNK, (c + 1) * CHUNK)
        v = v_ref[rows, :].astype(F32)
        vn = v * lax.rsqrt(jnp.mean(v * v, axis=-1, keepdims=True) + EPS) * gv
        vn = vn.astype(BF16)
        for g in range(A_GROUPS):
            cols = slice(g * gw, (g + 1) * gw)
            s = jnp.dot(ws_ref[g].astype(BF16), vn[:, cols], preferred_element_type=F32)
            s = s + bs_ref[:, g:g + 1]
            o_ref[rows, cols] = (u_ref[rows, cols].astype(F32) * s).astype(o_ref.dtype)


def _sgu(z, g_v, w_s, b_s, j, *, n_rows):
    tm = 4 * CHUNK
    n_a = g_v.shape[0]
    return pl.pallas_call(
        _sgu_kernel,
        grid=(n_rows // tm,),
        in_specs=[
            pl.BlockSpec((tm, A_WIDTH), lambda i: (i, 0)),
            pl.BlockSpec((tm, A_WIDTH), lambda i: (i, 1)),
            pl.BlockSpec((None, 1, A_WIDTH), lambda i: (j, 0, 0)),
            pl.BlockSpec((None, A_GROUPS, CHUNK, CHUNK), lambda i: (j, 0, 0, 0)),
            pl.BlockSpec((None, CHUNK, A_GROUPS), lambda i: (j, 0, 0)),
        ],
        out_specs=pl.BlockSpec((tm, A_WIDTH), lambda i: (i, 0)),
        out_shape=jax.ShapeDtypeStruct((M_ALL, A_WIDTH), BF16),
        compiler_params=_params(1),
        name="sgu",
    )(z, z, g_v.reshape(n_a, 1, A_WIDTH), w_s, b_s.transpose(0, 2, 1))


NA_DR = 2 * NA_KH - 1
NA_DC = 2 * NA_KW - 1
NA_PAIR_TILES = NA_DR + 1
NA_QROWS = 4
NA_KROWS = NA_KH + NA_QROWS
NA_BLOCK_UNROLL = 7


def _na_bias_pairs(rpb_ref, head, pair_ref):
    shape = (GRID_W, 2 * GRID_W)
    q = lax.broadcasted_iota(jnp.int32, shape, 0)
    lane = lax.broadcasted_iota(jnp.int32, shape, 1)
    kcol = lane & (GRID_W - 1)
    upper = lane >= GRID_W
    dc = jnp.clip(kcol - q, -(NA_KW - 1), NA_KW - 1) + NA_KW - 1
    c0 = jnp.clip(q - NA_KW // 2, 0, GRID_W - NA_KW)
    ok = (kcol >= c0) & (kcol < c0 + NA_KW)
    for d in range(-1, NA_DR):
        acc = jnp.full(shape, NEG_INF, F32)
        for c in range(NA_DC):
            lo = rpb_ref[head, d * NA_DC + c] if d >= 0 else NEG_INF
            hi = rpb_ref[head, (d + 1) * NA_DC + c] if d + 1 < NA_DR else NEG_INF
            acc = jnp.where(dc == c, jnp.where(upper, hi, lo), acc)
        pair_ref[d + 1] = jnp.where(ok, acc * LOG2_E, NEG_INF)


def _na_window_start(m):
    return min(max(m * NA_QROWS - NA_KH // 2, 0), GRID_ROWS - NA_KROWS)


def _na_block_bias(pair_ref, m):
    upper = lax.broadcasted_iota(jnp.int32, (GRID_W, 2 * GRID_W), 1) >= GRID_W
    masked = jnp.full((GRID_W, 2 * GRID_W), NEG_INF, F32)
    ws = _na_window_start(m)
    rows = []
    for a in range(NA_QROWS):
        r = m * NA_QROWS + a
        r0 = min(max(r - NA_KH // 2, 0), GRID_ROWS - NA_KH)
        tiles = []
        for p in range(NA_KROWS // 2):
            key_row = ws + 2 * p
            in_lo = r0 <= key_row < r0 + NA_KH
            in_hi = r0 <= key_row + 1 < r0 + NA_KH
            if not (in_lo or in_hi):
                tiles.append(masked)
                continue
            tile = pair_ref[key_row - r + NA_KH]
            if not in_hi:
                tile = jnp.where(upper, NEG_INF, tile)
            elif not in_lo:
                tile = jnp.where(upper, tile, NEG_INF)
            tiles.append(tile)
        rows.append(jnp.concatenate(tiles, axis=1))
    return jnp.concatenate(rows, axis=0)


def _na_lat_kernel(rpb_ref, q_ref, k_ref, v_ref, kc_ref, vc_ref, o_ref, pair_ref, bias_ref):
    scale = NA_HEAD_DIM ** -0.5 * LOG2_E

    @pl.when(pl.program_id(1) == 0)
    def _():
        _na_bias_pairs(rpb_ref, pl.program_id(0), pair_ref)
        bias_ref[...] = _na_block_bias(pair_ref, 1)

    kc = kc_ref[...]
    vc = vc_ref[...]
    nt = (((1,), (1,)), ((), ()))
    n_q = NA_QROWS * GRID_W
    n_blocks = GRID_ROWS // NA_QROWS

    def attend(qrows, win, bias):
        q = q_ref[qrows, :]
        kw = k_ref[win, :]
        vw = v_ref[win, :]
        s = lax.dot_general(q, kw, nt, preferred_element_type=F32) * scale + bias
        sc = lax.dot_general(q, kc, nt, preferred_element_type=F32) * scale
        mx = jnp.maximum(jnp.max(s, axis=-1, keepdims=True), jnp.max(sc, axis=-1, keepdims=True))
        p_w = jnp.exp2(s - mx)
        p_c = jnp.exp2(sc - mx)
        denom = jnp.sum(p_w, axis=-1, keepdims=True) + jnp.sum(p_c, axis=-1, keepdims=True)
        o = jnp.dot(p_w.astype(BF16), vw, preferred_element_type=F32)
        o = o + jnp.dot(p_c.astype(BF16), vc, preferred_element_type=F32)
        o_ref[qrows, :] = (o / denom).astype(o_ref.dtype)

    def edge_block(m):
        ws = _na_window_start(m)
        attend(slice(m * n_q, (m + 1) * n_q), slice(ws * GRID_W, (ws + NA_KROWS) * GRID_W),
               _na_block_bias(pair_ref, m))

    def interior_block(m, carry):
        qrows = pl.ds(pl.multiple_of(m * n_q, n_q), n_q)
        ws = m * NA_QROWS - NA_KH // 2
        win = pl.ds(pl.multiple_of(ws * GRID_W, GRID_W), NA_KROWS * GRID_W)
        attend(qrows, win, bias_ref[...])
        return carry

    edge_block(0)
    lax.fori_loop(1, n_blocks - 1, interior_block, 0, unroll=NA_BLOCK_UNROLL)
    edge_block(n_blocks - 1)


def _na_ctx_kernel(q_ref, k_ref, v_ref, o_ref):
    scale = NA_HEAD_DIM ** -0.5
    s = lax.dot_general(q_ref[...], k_ref[...], (((1,), (1,)), ((), ())),
                        preferred_element_type=F32) * scale
    m = jnp.max(s, axis=-1, keepdims=True)
    p = jnp.exp(s - m)
    denom = jnp.sum(p, axis=-1, keepdims=True)
    o = jnp.dot(p.astype(BF16), v_ref[...], preferred_element_type=F32)
    o_ref[...] = (o / denom).astype(o_ref.dtype)


def _na_attention(qkv, rpb):
    dh, nh = NA_HEAD_DIM, NA_HEADS
    ctx_blk0 = M_LAT // CTX_LEN
    lat = pl.pallas_call(
        _na_lat_kernel,
        grid=(nh, BATCH),
        in_specs=[
            pl.BlockSpec(memory_space=pltpu.SMEM),
            pl.BlockSpec((SEQ, dh), lambda h, b: (b, h)),
            pl.BlockSpec((SEQ, dh), lambda h, b: (b, nh + h)),
            pl.BlockSpec((SEQ, dh), lambda h, b: (b, 2 * nh + h)),
            pl.BlockSpec((CTX_LEN, dh), lambda h, b: (ctx_blk0 + b, nh + h)),
            pl.BlockSpec((CTX_LEN, dh), lambda h, b: (ctx_blk0 + b, 2 * nh + h)),
        ],
        out_specs=pl.BlockSpec((SEQ, dh), lambda h, b: (b, h)),
        out_shape=jax.ShapeDtypeStruct((M_ALL, D_MODEL), BF16),
        scratch_shapes=[
            pltpu.VMEM((NA_PAIR_TILES, GRID_W, 2 * GRID_W), F32),
            pltpu.VMEM((NA_QROWS * GRID_W, NA_KROWS * GRID_W), F32),
        ],
        compiler_params=_params(2),
        name="na_latent",
    )(rpb.reshape(nh, NA_DR * NA_DC), qkv, qkv, qkv, qkv, qkv)
    ctx = pl.pallas_call(
        _na_ctx_kernel,
        grid=(BATCH, nh),
        in_specs=[
            pl.BlockSpec((CTX_LEN, dh), lambda b, h: (ctx_blk0 + b, h)),
            pl.BlockSpec((CTX_LEN, dh), lambda b, h: (ctx_blk0 + b, nh + h)),
            pl.BlockSpec((CTX_LEN, dh), lambda b, h: (ctx_blk0 + b, 2 * nh + h)),
        ],
        out_specs=pl.BlockSpec((CTX_LEN, dh), lambda b, h: (b, h)),
        out_shape=jax.ShapeDtypeStruct((M_CTX, D_MODEL), BF16),
        compiler_params=_params(2),
        name="na_context",
    )(qkv, qkv, qkv)
    return lax.dynamic_update_slice(lat, ctx, (M_LAT, 0))


RNN_LEN = CTX_LEN + SEQ
SCAN_UNROLL = 8


def _shift_rows(x, offset, length):
    if offset == 0:
        return x
    t = lax.broadcasted_iota(jnp.int32, (length, 1), 0)
    rolled = pltpu.roll(x, (-offset) % length, 0)
    return jnp.where((t + offset >= 0) & (t + offset < length), rolled, 0.0)


def _rnn_conv(x, cw, cb):
    length = x.shape[0]
    left = RNN_CONV // 2
    y = cb
    for j in range(RNN_CONV):
        y = y + _shift_rows(x, j - left, length) * cw[j:j + 1]
    return y


def _block_scan(a, b, reverse):
    row = lax.broadcasted_iota(jnp.int32, a.shape, 0)
    for s in (1, 2, 4):
        if reverse:
            keep = row < SUBLANES - s
            shift = SUBLANES - s
        else:
            keep = row >= s
            shift = s
        a_sh = pltpu.roll(a, shift, 0)
        b_sh = pltpu.roll(b, shift, 0)
        b = jnp.where(keep, a * b_sh + b, b)
        a = jnp.where(keep, a * a_sh, a)
    return a, b


def _rglru_kernel(xc_ref, xl_ref, yc_ref, yl_ref, cw_ref, cb_ref, wg_ref, bg_ref, lam_ref,
                  oc_ref, ol_ref, af_ref, bf_ref, ar_ref, br_ref, hf_ref, hr_ref):
    cw = cw_ref[...]
    cb = cb_ref[...]
    xr_c = _rnn_conv(xc_ref[...].astype(F32), cw, cb)
    xr_l = _rnn_conv(xl_ref[...].astype(F32), cw, cb)

    def half_tanh_gate(xb, d, gate):
        w = (0.5 * wg_ref[2 * d + gate]).astype(BF16)
        return jnp.tanh(jnp.dot(xb, w, preferred_element_type=F32) + 0.5 * bg_ref[2 * d + gate])

    def gates(xr, d, a_ref, b_ref, rows):
        xb = xr.astype(BF16)
        half_xr = 0.5 * xr
        t_r = half_tanh_gate(xb, d, 0)
        t_i = half_tanh_gate(xb, d, 1)
        neg_lam = -lam_ref[d:d + 1, :]
        softplus = jnp.maximum(neg_lam, 0.0) + jnp.log1p(jnp.exp(-jnp.abs(neg_lam)))
        c = (-0.5 * RG_C * LOG2_E) * softplus
        a = jnp.exp2(c * t_r + c)
        a_ref[rows, :] = a
        y = 1.0 - a * a
        root = jnp.where(y > 0.0, y * lax.rsqrt(y), 0.0)
        b_ref[rows, :] = root * (half_xr * t_i + half_xr)

    ctx_f, lat_f = slice(0, CTX_LEN), slice(CTX_LEN, RNN_LEN)
    lat_r, ctx_r = slice(0, SEQ), slice(SEQ, RNN_LEN)
    gates(xr_c, 0, af_ref, bf_ref, ctx_f)
    gates(xr_l, 0, af_ref, bf_ref, lat_f)
    gates(xr_c, 1, ar_ref, br_ref, ctx_r)
    gates(xr_l, 1, ar_ref, br_ref, lat_r)

    nblk = RNN_LEN // SUBLANES
    last_row = slice(SUBLANES - 1, SUBLANES)
    first_row = slice(0, 1)
    blk_shape = (SUBLANES, RNN_HEAD_DIM)

    def block(n, carry):
        cf, cr = carry
        rows_f = pl.ds(pl.multiple_of(n * SUBLANES, SUBLANES), SUBLANES)
        rows_r = pl.ds(pl.multiple_of((nblk - 1 - n) * SUBLANES, SUBLANES), SUBLANES)
        a_f, b_f = _block_scan(af_ref[rows_f, :], bf_ref[rows_f, :], reverse=False)
        a_r, b_r = _block_scan(ar_ref[rows_r, :], br_ref[rows_r, :], reverse=True)
        hf_ref[rows_f, :] = a_f * cf + b_f
        hr_ref[rows_r, :] = a_r * cr + b_r
        cf = jnp.broadcast_to(a_f[last_row], blk_shape) * cf + jnp.broadcast_to(b_f[last_row], blk_shape)
        cr = jnp.broadcast_to(a_r[first_row], blk_shape) * cr + jnp.broadcast_to(b_r[first_row], blk_shape)
        return cf, cr

    zero = jnp.zeros(blk_shape, F32)
    lax.fori_loop(0, nblk, block, (zero, zero), unroll=SCAN_UNROLL)

    h_c = hf_ref[ctx_f, :] + hr_ref[ctx_r, :]
    h_l = hf_ref[lat_f, :] + hr_ref[lat_r, :]
    oc_ref[...] = (jax.nn.gelu(yc_ref[...].astype(F32)) * h_c).astype(oc_ref.dtype)
    ol_ref[...] = (jax.nn.gelu(yl_ref[...].astype(F32)) * h_l).astype(ol_ref.dtype)


def _rglru(xin, conv_w, conv_b, w_gate, b_gate, lam, j):
    dh, nh = RNN_HEAD_DIM, RNN_HEADS
    n_c = conv_w.shape[0]
    ctx_blk0 = M_LAT // CTX_LEN
    wg = w_gate.reshape(n_c, 4, nh, dh, dh)
    bg = b_gate.reshape(n_c, 4, nh, 1, dh)
    out_c, out_l = pl.pallas_call(
        _rglru_kernel,
        grid=(BATCH, nh),
        in_specs=[
            pl.BlockSpec((CTX_LEN, dh), lambda b, h: (ctx_blk0 + b, nh + h)),
            pl.BlockSpec((SEQ, dh), lambda b, h: (b, nh + h)),
            pl.BlockSpec((CTX_LEN, dh), lambda b, h: (ctx_blk0 + b, h)),
            pl.BlockSpec((SEQ, dh), lambda b, h: (b, h)),
            pl.BlockSpec((None, RNN_CONV, dh), lambda b, h: (j, 0, h)),
            pl.BlockSpec((None, 1, dh), lambda b, h: (j, 0, h)),
            pl.BlockSpec((None, 4, None, dh, dh), lambda b, h: (j, 0, h, 0, 0)),
            pl.BlockSpec((None, 4, None, 1, dh), lambda b, h: (j, 0, h, 0, 0)),
            pl.BlockSpec((None, 2, dh), lambda b, h: (j, 0, h)),
        ],
        out_specs=[
            pl.BlockSpec((CTX_LEN, dh), lambda b, h: (b, h)),
            pl.BlockSpec((SEQ, dh), lambda b, h: (b, h)),
        ],
        out_shape=[
            jax.ShapeDtypeStruct((M_CTX, RNN_WIDTH), BF16),
            jax.ShapeDtypeStruct((M_ALL, RNN_WIDTH), BF16),
        ],
        scratch_shapes=[pltpu.VMEM((RNN_LEN, dh), F32)] * 6,
        compiler_params=_params(2),
        name="rglru",
    )(xin, xin, xin, xin, conv_w, conv_b.reshape(n_c, 1, RNN_WIDTH), wg, bg, lam)
    return lax.dynamic_update_slice(out_l, out_c, (M_LAT, 0))


def kernel(x, c, ctx, c_ctx, ada_w, ada_b, norm_g, ffn_w_up, ffn_conv_w, ffn_conv_b, ffn_w_down,
           a_w_in, a_g_v, a_w_s, a_b_s, a_w_out, b_w_qkv, b_rpb, b_w_out,
           c_w_in, c_conv_w, c_conv_b, c_w_gate, c_b_gate, c_lam, c_w_out, final_g):
    h = (x.reshape(M_LAT, D_MODEL), ctx.reshape(M_CTX, D_MODEL))
    cvec = jnp.concatenate(
        [c, c_ctx[None], jnp.zeros((SUBLANES - BATCH - 1, D_MODEL), F32)], axis=0)
    mod = _ada_mod(cvec, ada_w, ada_b)

    for i in range(DEPTH):
        last = i == DEPTH - 1
        kind, j = i % N_MIXERS, i // N_MIXERS
        ffn_rows = M_LAT if last else M_ALL
        n_rows = M_LAT if (last and kind == 0) else M_ALL

        n = _norm_mod(h, norm_g, mod, i, 0, n_rows)
        if kind == 0:
            z = _matmul(n, a_w_in, j, n_rows=n_rows, tn=1024, out_dtype=BF16, act="gelu")
            y = _sgu(z, a_g_v, a_w_s, a_b_s, j, n_rows=n_rows)
            w_out = a_w_out
        elif kind == 1:
            qkv = _matmul(n, b_w_qkv, j, n_rows=n_rows, tn=1024, out_dtype=BF16)
            y = _na_attention(qkv, b_rpb[j])
            w_out = b_w_out
        else:
            xin = _matmul(n, c_w_in, j, n_rows=n_rows, tn=1024, out_dtype=F32)
            y = _rglru(xin, c_conv_w, c_conv_b, c_w_gate, c_b_gate, c_lam, j)
            w_out = c_w_out
        h = _matmul_resid(y, w_out, j, h, mod, i, 2, n_rows=n_rows, tn=1024)

        n = _norm_mod(h, norm_g, mod, i, 1, ffn_rows)
        a = _ffn_up(n, ffn_w_up, ffn_conv_w, ffn_conv_b, i, n_rows=ffn_rows)
        h = _matmul_resid(a, ffn_w_down, i, h, mod, i, 5, n_rows=ffn_rows, tn=512)

    return _final_norm(h, final_g).reshape(BATCH, SEQ, D_MODEL)
```

```python
import functools

import jax
import jax.numpy as jnp
from jax import lax
from jax.experimental import pallas as pl
from jax.experimental.pallas import tpu as pltpu

F32 = jnp.float32
BF16 = jnp.bfloat16

D_MODEL = 2048
BATCH = 2
SEQ = 4096
DEPTH = 4
GRID_W = 64
CTX_LEN = 256
N_MIXERS = 3
N_MOD = 6
EPS = 1e-6
NEG_INF = -1e30
LOG2_E = 1.4426950408889634
D_FF = 5632
FFN_CONV = 3
CHUNK = 128
A_WIDTH = 2 * D_MODEL
A_GROUPS = 16
NA_HEADS = 16
NA_HEAD_DIM = D_MODEL // NA_HEADS
NA_KH = 8
NA_KW = 16
RNN_WIDTH = D_MODEL
RNN_HEADS = 16
RNN_HEAD_DIM = RNN_WIDTH // RNN_HEADS
RNN_CONV = 4
RG_C = 8.0

M_LAT = BATCH * SEQ
M_CTX = BATCH * CTX_LEN
M_ALL = M_LAT + M_CTX
GRID_ROWS = SEQ // GRID_W

VMEM_LIMIT_BYTES = 56 * 1024 * 1024
SUBLANES = 8
LANES = 128
BF16_ROWS = 16

ROW_TILE = {M_ALL: M_ALL // 8, M_LAT: M_LAT // 8}
ROW_TILE_WIDE_K = {M_ALL: M_ALL // 16, M_LAT: M_LAT // 16}
NORM_CHUNK = BF16_ROWS


def _params(n_axes):
    return pltpu.CompilerParams(
        dimension_semantics=("arbitrary",) * n_axes, vmem_limit_bytes=VMEM_LIMIT_BYTES)


def _sigmoid(x):
    return 0.5 * jnp.tanh(0.5 * x) + 0.5


def _mod_spec(layer, which, tn=D_MODEL, grid_rank=1):
    if grid_rank == 1:
        return pl.BlockSpec((None, SUBLANES, tn), lambda i: (layer * N_MOD + which, 0, 0))
    return pl.BlockSpec((None, SUBLANES, tn), lambda j, i: (layer * N_MOD + which, 0, j))


def _rows_of_group(mod_ref, row0, tm):
    r = row0 + lax.broadcasted_iota(jnp.int32, (tm, 1), 0)
    m = mod_ref[...]
    out = m[BATCH:BATCH + 1]
    for b in reversed(range(BATCH)):
        out = jnp.where(r < (b + 1) * SEQ, m[b:b + 1], out)
    return out


def _split_stream_specs(tm, tn, row_of, col_of):
    assert M_LAT % tm == 0 and tm == M_CTX
    last_lat = M_LAT // tm - 1
    return [
        pl.BlockSpec((tm, tn), lambda *ids: (jnp.minimum(row_of(*ids), last_lat), col_of(*ids))),
        pl.BlockSpec((tm, tn), lambda *ids: (0, col_of(*ids))),
    ]


def _stream_tile(h_refs, i, tm, rows=slice(None)):
    if len(h_refs) == 1:
        return h_refs[0][rows, :]
    lat_ref, ctx_ref = h_refs
    return jnp.where(i < M_LAT // tm, lat_ref[rows, :], ctx_ref[rows, :])


def _ada_kernel(c_ref, w_ref, b_ref, o_ref):
    c = c_ref[...]
    s = (c * _sigmoid(c)).astype(BF16)
    o_ref[...] = jnp.dot(s, w_ref[...].astype(BF16), preferred_element_type=F32) + b_ref[...]


def _ada_mod(cvec, ada_w, ada_b):
    depth, d, n = ada_w.shape
    tn = 1024
    out = pl.pallas_call(
        _ada_kernel,
        grid=(depth, n // tn),
        in_specs=[
            pl.BlockSpec((SUBLANES, d), lambda l, j: (0, 0)),
            pl.BlockSpec((None, d, tn), lambda l, j: (l, 0, j)),
            pl.BlockSpec((None, 1, tn), lambda l, j: (l, 0, j)),
        ],
        out_specs=pl.BlockSpec((None, SUBLANES, tn), lambda l, j: (l, 0, j)),
        out_shape=jax.ShapeDtypeStruct((depth, SUBLANES, n), F32),
        compiler_params=_params(2),
        name="ada_mod",
    )(cvec, ada_w, ada_b.reshape(depth, 1, n))
    return out.reshape(depth, SUBLANES, N_MOD, d).transpose(0, 2, 1, 3).reshape(depth * N_MOD, SUBLANES, d)


def _norm_mod_kernel(*refs):
    *h_refs, g_ref, shift_ref, scale_ref, o_ref, gain_ref = refs
    i = pl.program_id(0)
    tm = o_ref.shape[0]
    gain_ref[...] = g_ref[...] * (1 + scale_ref[...])

    for r0 in range(0, tm, NORM_CHUNK):
        rows = slice(r0, r0 + NORM_CHUNK)
        group = pl.ds(jnp.minimum((i * tm + r0) // SEQ, BATCH), 1)
        x = _stream_tile(h_refs, i, tm, rows)
        y = x * lax.rsqrt(jnp.mean(x * x, axis=-1, keepdims=True) + EPS)
        o_ref[rows, :] = (y * gain_ref[group, :] + shift_ref[group, :]).astype(o_ref.dtype)


def _norm_mod(h, norm_g, mod, layer, which_norm, n_rows):
    g = norm_g.reshape(DEPTH * 2, 1, D_MODEL)
    if isinstance(h, tuple):
        tm = M_CTX
        h_arrays, h_specs = h, _split_stream_specs(tm, D_MODEL, lambda i: i, lambda i: 0)
    else:
        tm = ROW_TILE[n_rows]
        h_arrays, h_specs = (h,), [pl.BlockSpec((tm, D_MODEL), lambda i: (i, 0))]
    return pl.pallas_call(
        _norm_mod_kernel,
        grid=(n_rows // tm,),
        in_specs=[
            *h_specs,
            pl.BlockSpec((None, 1, D_MODEL), lambda i: (layer * 2 + which_norm, 0, 0)),
            _mod_spec(layer, 3 * which_norm),
            _mod_spec(layer, 3 * which_norm + 1),
        ],
        out_specs=pl.BlockSpec((tm, D_MODEL), lambda i: (i, 0)),
        out_shape=jax.ShapeDtypeStruct((M_ALL, D_MODEL), BF16),
        scratch_shapes=[pltpu.VMEM((SUBLANES, D_MODEL), F32)],
        compiler_params=_params(1),
        name="norm_mod",
    )(*h_arrays, g, mod, mod)


def _final_norm_kernel(h_ref, g_ref, o_ref):
    gain = g_ref[...]

    def chunk(r, carry):
        rows = pl.ds(pl.multiple_of(r * NORM_CHUNK, NORM_CHUNK), NORM_CHUNK)
        x = h_ref[rows, :]
        o_ref[rows, :] = x * lax.rsqrt(jnp.mean(x * x, axis=-1, keepdims=True) + EPS) * gain
        return carry

    lax.fori_loop(0, h_ref.shape[0] // NORM_CHUNK, chunk, 0, unroll=True)


def _final_norm(h, final_g):
    tm = ROW_TILE[M_LAT]
    return pl.pallas_call(
        _final_norm_kernel,
        grid=(M_LAT // tm,),
        in_specs=[
            pl.BlockSpec((tm, D_MODEL), lambda i: (i, 0)),
            pl.BlockSpec((1, D_MODEL), lambda i: (0, 0)),
        ],
        out_specs=pl.BlockSpec((tm, D_MODEL), lambda i: (i, 0)),
        out_shape=jax.ShapeDtypeStruct((M_LAT, D_MODEL), F32),
        compiler_params=_params(1),
        name="final_norm",
    )(h, final_g.reshape(1, D_MODEL))


def _mm_kernel(x_ref, w_ref, o_ref, wbf_ref, *, act):
    @pl.when(pl.program_id(1) == 0)
    def _():
        wbf_ref[...] = w_ref[...].astype(BF16)

    acc = jnp.dot(x_ref[...], wbf_ref[...], preferred_element_type=F32)
    if act == "gelu":
        acc = jax.nn.gelu(acc)
    o_ref[...] = acc.astype(o_ref.dtype)


def _mm_resid_kernel(x_ref, w_ref, gate_ref, *refs):
    *res_refs, o_ref, wbf_ref = refs
    i = pl.program_id(1)
    tm = x_ref.shape[0]

    @pl.when(i == 0)
    def _():
        wbf_ref[...] = w_ref[...].astype(BF16)

    acc = jnp.dot(x_ref[...], wbf_ref[...], preferred_element_type=F32)
    o_ref[...] = _stream_tile(res_refs, i, tm) + _rows_of_group(gate_ref, i * tm, tm) * acc


def _matmul(x, w, w_idx, *, n_rows, tn, out_dtype, act=None):
    tm = ROW_TILE[n_rows]
    _, k, n = w.shape
    return pl.pallas_call(
        functools.partial(_mm_kernel, act=act),
        grid=(n // tn, n_rows // tm),
        in_specs=[
            pl.BlockSpec((tm, k), lambda j, i: (i, 0)),
            pl.BlockSpec((None, k, tn), lambda j, i: (w_idx, 0, j)),
        ],
        out_specs=pl.BlockSpec((tm, tn), lambda j, i: (i, j)),
        out_shape=jax.ShapeDtypeStruct((x.shape[0], n), out_dtype),
        scratch_shapes=[pltpu.VMEM((k, tn), BF16)],
        compiler_params=_params(2),
        name="matmul",
    )(x, w)


def _matmul_resid(x, w, w_idx, h, mod, layer, which_gate, *, n_rows, tn):
    _, k, n = w.shape
    w_tile_bytes = k * tn * 4
    w_buffers = pl.Buffered(1) if w_tile_bytes > 12 * 2**20 else None
    if isinstance(h, tuple):
        tm = M_CTX
        res_arrays, aliases = h, {}
        res_specs = _split_stream_specs(tm, tn, lambda j, i: i, lambda j, i: j)
    else:
        tm = (ROW_TILE_WIDE_K if w_tile_bytes > 8 * 2**20 else ROW_TILE)[n_rows]
        res_arrays, aliases = (h,), {3: 0}
        res_specs = [pl.BlockSpec((tm, tn), lambda j, i: (i, j))]
    return pl.pallas_call(
        _mm_resid_kernel,
        grid=(n // tn, n_rows // tm),
        in_specs=[
            pl.BlockSpec((tm, k), lambda j, i: (i, 0)),
            pl.BlockSpec((None, k, tn), lambda j, i: (w_idx, 0, j), pipeline_mode=w_buffers),
            _mod_spec(layer, which_gate, tn, grid_rank=2),
            *res_specs,
        ],
        out_specs=pl.BlockSpec((tm, tn), lambda j, i: (i, j)),
        out_shape=jax.ShapeDtypeStruct((M_ALL, n), F32),
        scratch_shapes=[pltpu.VMEM((k, tn), BF16)],
        input_output_aliases=aliases,
        compiler_params=_params(2),
        name="matmul_resid",
    )(x, w, mod, *res_arrays)


HALO = BF16_ROWS


SEQ_STARTS = tuple(b * SEQ for b in range(BATCH)) + tuple(M_LAT + b * CTX_LEN for b in range(BATCH))
FIX_ROWS = BF16_ROWS


def _is_seq_edge(row):
    edge = row == M_ALL
    for s in SEQ_STARTS:
        edge = edge | (row == s)
    return edge


def _conv3(prev, cur, nxt, conv_ref):
    p = conv_ref[...]
    y = p[3:4] + prev * p[0:1]
    y = y + cur * p[1:2]
    return y + nxt * p[2:3]


def _silu_gate(g, v):
    half_g = 0.5 * g
    return (half_g * jnp.tanh(half_g) + half_g) * v


def _ffn_up_kernel(x_ref, xp_ref, xn_ref, wg_ref, wv_ref, cg_ref, cv_ref,
                   o_ref, wg_bf, wv_bf, xs_ref, zg_ref, zv_ref):
    i = pl.program_id(1)
    tm = x_ref.shape[0]
    row0 = i * tm

    @pl.when(i == 0)
    def _():
        wg_bf[...] = wg_ref[...].astype(BF16)
        wv_bf[...] = wv_ref[...].astype(BF16)

    no_rows = jnp.zeros_like(xp_ref)
    xs_ref[0:HALO, :] = jnp.where(_is_seq_edge(row0), no_rows, xp_ref[...])
    xs_ref[HALO:HALO + tm, :] = x_ref[...]
    xs_ref[HALO + tm:, :] = jnp.where(_is_seq_edge(row0 + tm), no_rows, xn_ref[...])
    xs = xs_ref[...]
    zg_ref[...] = jnp.dot(xs, wg_bf[...], preferred_element_type=F32)
    zv_ref[...] = jnp.dot(xs, wv_bf[...], preferred_element_type=F32)

    def conv(z_ref, conv_ref):
        return _conv3(z_ref[HALO - 1:HALO - 1 + tm, :], z_ref[HALO:HALO + tm, :],
                      z_ref[HALO + 1:HALO + 1 + tm, :], conv_ref)

    o_ref[...] = _silu_gate(conv(zg_ref, cg_ref), conv(zv_ref, cv_ref)).astype(o_ref.dtype)

    n_win = 2 * FIX_ROWS + 2 * SUBLANES
    w_row = lax.broadcasted_iota(jnp.int32, (n_win, 1), 0)
    start_row = SUBLANES + FIX_ROWS

    def fixed_conv(z_ref, conv_ref, win):
        z = z_ref[win, :]
        prev = jnp.where(w_row == start_row, 0.0, pltpu.roll(z, 1, 0))
        nxt = jnp.where(w_row == start_row - 1, 0.0, pltpu.roll(z, n_win - 1, 0))
        return _conv3(prev, z, nxt, conv_ref)[SUBLANES:SUBLANES + 2 * FIX_ROWS]

    for s in SEQ_STARTS[1:]:
        @pl.when((s > row0) & (s < row0 + tm))
        def _():
            win = pl.ds(pl.multiple_of(HALO + s - row0 - start_row, SUBLANES), n_win)
            rows = pl.ds(pl.multiple_of(s - row0 - FIX_ROWS, FIX_ROWS), 2 * FIX_ROWS)
            g = fixed_conv(zg_ref, cg_ref, win)
            v = fixed_conv(zv_ref, cv_ref, win)
            o_ref[rows, :] = _silu_gate(g, v).astype(o_ref.dtype)


def _ffn_up(n, w_up, conv_w, conv_b, layer, *, n_rows):
    tm, tn = ROW_TILE[n_rows], 512
    k = w_up.shape[1]
    nj = D_FF // tn
    halo_per_tile = tm // HALO
    last_halo = M_ALL // HALO - 1
    conv_p = jnp.concatenate([conv_w, conv_b.reshape(DEPTH, 1, 2 * D_FF)], axis=1)
    return pl.pallas_call(
        _ffn_up_kernel,
        grid=(nj, n_rows // tm),
        in_specs=[
            pl.BlockSpec((tm, k), lambda j, i: (i, 0)),
            pl.BlockSpec((HALO, k), lambda j, i: (jnp.maximum(i * halo_per_tile - 1, 0), 0)),
            pl.BlockSpec((HALO, k), lambda j, i: (jnp.minimum((i + 1) * halo_per_tile, last_halo), 0)),
            pl.BlockSpec((None, k, tn), lambda j, i: (layer, 0, j)),
            pl.BlockSpec((None, k, tn), lambda j, i: (layer, 0, j + nj)),
            pl.BlockSpec((None, FFN_CONV + 1, tn), lambda j, i: (layer, 0, j)),
            pl.BlockSpec((None, FFN_CONV + 1, tn), lambda j, i: (layer, 0, j + nj)),
        ],
        out_specs=pl.BlockSpec((tm, tn), lambda j, i: (i, j)),
        out_shape=jax.ShapeDtypeStruct((M_ALL, D_FF), BF16),
        scratch_shapes=[
            pltpu.VMEM((k, tn), BF16),
            pltpu.VMEM((k, tn), BF16),
            pltpu.VMEM((tm + 2 * HALO, k), BF16),
            pltpu.VMEM((tm + 2 * HALO, tn), F32),
            pltpu.VMEM((tm + 2 * HALO, tn), F32),
        ],
        compiler_params=_params(2),
        name="ffn_up_conv_gate",
    )(n, n, n, w_up, w_up, conv_p, conv_p)


def _sgu_kernel(u_ref, v_ref, gv_ref, ws_ref, bs_ref, o_ref):
    tm = u_ref.shape[0]
    gw = A_WIDTH // A_GROUPS
    gv = gv_ref[...]
    for c in range(tm // CHUNK):
        rows = slice(c * CHUNK, (c + 1) * CHUNK)
        v = v_ref[rows, :].astype(F32)
        vn = v * lax.rsqrt(jnp.mean(v * v, axis=-1, keepdims=True) + EPS) * gv
        vn = vn.astype(BF16)
        for g in range(A_GROUPS):
            cols = slice(g * gw, (g + 1) * gw)
            s = jnp.dot(ws_ref[g].astype(BF16), vn[:, cols], preferred_element_type=F32)
            s = s + bs_ref[:, g:g + 1]
            o_ref[rows, cols] = (u_ref[rows, cols].astype(F32) * s).astype(o_ref.dtype)


def _sgu(z, g_v, w_s, b_s, j, *, n_rows):
    tm = 4 * CHUNK
    n_a = g_v.shape[0]
    return pl.pallas_call(
        _sgu_kernel,
        grid=(n_rows // tm,),
        in_specs=[
            pl.BlockSpec((tm, A_WIDTH), lambda i: (i, 0)),
            pl.BlockSpec((tm, A_WIDTH), lambda i: (i, 1)),
            pl.BlockSpec((None, 1, A_WIDTH), lambda i: (j, 0, 0)),
            pl.BlockSpec((None, A_GROUPS, CHUNK, CHUNK), lambda i: (j, 0, 0, 0)),
            pl.BlockSpec((None, CHUNK, A_GROUPS), lambda i: (j, 0, 0)),
        ],
        out_specs=pl.BlockSpec((tm, A_WIDTH), lambda i: (i, 0)),
        out_shape=jax.ShapeDtypeStruct((M_ALL, A_WIDTH), BF16),
        compiler_params=_params(1),
        name="sgu",
    )(z, z, g_v.reshape(n_a, 1, A_WIDTH), w_s, b_s.transpose(0, 2, 1))


NA_DR = 2 * NA_KH - 1
NA_DC = 2 * NA_KW - 1
NA_PAIR_TILES = NA_DR + 1
NA_QROWS = 4
NA_KROWS = NA_KH + NA_QROWS
NA_BLOCK_UNROLL = 14


def _na_bias_pairs(rpb_ref, head, pair_ref):
    shape = (GRID_W, 2 * GRID_W)
    q = lax.broadcasted_iota(jnp.int32, shape, 0)
    lane = lax.broadcasted_iota(jnp.int32, shape, 1)
    kcol = lane & (GRID_W - 1)
    upper = lane >= GRID_W
    dc = jnp.clip(kcol - q, -(NA_KW - 1), NA_KW - 1) + NA_KW - 1
    c0 = jnp.clip(q - NA_KW // 2, 0, GRID_W - NA_KW)
    ok = (kcol >= c0) & (kcol < c0 + NA_KW)
    for d in range(-1, NA_DR):
        acc = jnp.full(shape, NEG_INF, F32)
        for c in range(NA_DC):
            lo = rpb_ref[head, d * NA_DC + c] if d >= 0 else NEG_INF
            hi = rpb_ref[head, (d + 1) * NA_DC + c] if d + 1 < NA_DR else NEG_INF
            acc = jnp.where(dc == c, jnp.where(upper, hi, lo), acc)
        pair_ref[d + 1] = jnp.where(ok, acc * LOG2_E, NEG_INF)


def _na_window_start(m):
    return min(max(m * NA_QROWS - NA_KH // 2, 0), GRID_ROWS - NA_KROWS)


def _na_block_bias(pair_ref, m):
    upper = lax.broadcasted_iota(jnp.int32, (GRID_W, 2 * GRID_W), 1) >= GRID_W
    masked = jnp.full((GRID_W, 2 * GRID_W), NEG_INF, F32)
    ws = _na_window_start(m)
    rows = []
    for a in range(NA_QROWS):
        r = m * NA_QROWS + a
        r0 = min(max(r - NA_KH // 2, 0), GRID_ROWS - NA_KH)
        tiles = []
        for p in range(NA_KROWS // 2):
            key_row = ws + 2 * p
            in_lo = r0 <= key_row < r0 + NA_KH
            in_hi = r0 <= key_row + 1 < r0 + NA_KH
            if not (in_lo or in_hi):
                tiles.append(masked)
                continue
            tile = pair_ref[key_row - r + NA_KH]
            if not in_hi:
                tile = jnp.where(upper, NEG_INF, tile)
            elif not in_lo:
                tile = jnp.where(upper, tile, NEG_INF)
            tiles.append(tile)
        rows.append(jnp.concatenate(tiles, axis=1))
    return jnp.concatenate(rows, axis=0)


def _na_lat_kernel(rpb_ref, q_ref, k_ref, v_ref, kc_ref, vc_ref, o_ref, pair_ref, bias_ref):
    scale = NA_HEAD_DIM ** -0.5 * LOG2_E

    @pl.when(pl.program_id(1) == 0)
    def _():
        _na_bias_pairs(rpb_ref, pl.program_id(0), pair_ref)
        bias_ref[...] = _na_block_bias(pair_ref, 1)

    kc = kc_ref[...]
    vc = vc_ref[...]
    nt = (((1,), (1,)), ((), ()))
    n_q = NA_QROWS * GRID_W
    n_blocks = GRID_ROWS // NA_QROWS

    def attend(qrows, win, bias):
        q = q_ref[qrows, :]
        kw = k_ref[win, :]
        vw = v_ref[win, :]
        s = lax.dot_general(q, kw, nt, preferred_element_type=F32) * scale + bias
        sc = lax.dot_general(q, kc, nt, preferred_element_type=F32) * scale
        mx = jnp.maximum(jnp.max(s, axis=-1, keepdims=True), jnp.max(sc, axis=-1, keepdims=True))
        p_w = jnp.exp2(s - mx)
        p_c = jnp.exp2(sc - mx)
        denom = jnp.sum(p_w, axis=-1, keepdims=True) + jnp.sum(p_c, axis=-1, keepdims=True)
        o = jnp.dot(p_w.astype(BF16), vw, preferred_element_type=F32)
        o = o + jnp.dot(p_c.astype(BF16), vc, preferred_element_type=F32)
        o_ref[qrows, :] = (o / denom).astype(o_ref.dtype)

    def edge_block(m):
        ws = _na_window_start(m)
        attend(slice(m * n_q, (m + 1) * n_q), slice(ws * GRID_W, (ws + NA_KROWS) * GRID_W),
               _na_block_bias(pair_ref, m))

    def interior_block(m, carry):
        qrows = pl.ds(pl.multiple_of(m * n_q, n_q), n_q)
        ws = m * NA_QROWS - NA_KH // 2
        win = pl.ds(pl.multiple_of(ws * GRID_W, GRID_W), NA_KROWS * GRID_W)
        attend(qrows, win, bias_ref[...])
        return carry

    edge_block(0)
    lax.fori_loop(1, n_blocks - 1, interior_block, 0, unroll=NA_BLOCK_UNROLL)
    edge_block(n_blocks - 1)


def _na_ctx_kernel(q_ref, k_ref, v_ref, o_ref):
    scale = NA_HEAD_DIM ** -0.5
    s = lax.dot_general(q_ref[...], k_ref[...], (((1,), (1,)), ((), ())),
                        preferred_element_type=F32) * scale
    m = jnp.max(s, axis=-1, keepdims=True)
    p = jnp.exp(s - m)
    denom = jnp.sum(p, axis=-1, keepdims=True)
    o = jnp.dot(p.astype(BF16), v_ref[...], preferred_element_type=F32)
    o_ref[...] = (o / denom).astype(o_ref.dtype)


def _na_attention(qkv, rpb):
    dh, nh = NA_HEAD_DIM, NA_HEADS
    ctx_blk0 = M_LAT // CTX_LEN
    lat = pl.pallas_call(
        _na_lat_kernel,
        grid=(nh, BATCH),
        in_specs=[
            pl.BlockSpec(memory_space=pltpu.SMEM),
            pl.BlockSpec((SEQ, dh), lambda h, b: (b, h)),
            pl.BlockSpec((SEQ, dh), lambda h, b: (b, nh + h)),
            pl.BlockSpec((SEQ, dh), lambda h, b: (b, 2 * nh + h)),
            pl.BlockSpec((CTX_LEN, dh), lambda h, b: (ctx_blk0 + b, nh + h)),
            pl.BlockSpec((CTX_LEN, dh), lambda h, b: (ctx_blk0 + b, 2 * nh + h)),
        ],
        out_specs=pl.BlockSpec((SEQ, dh), lambda h, b: (b, h)),
        out_shape=jax.ShapeDtypeStruct((M_ALL, D_MODEL), BF16),
        scratch_shapes=[
            pltpu.VMEM((NA_PAIR_TILES, GRID_W, 2 * GRID_W), F32),
            pltpu.VMEM((NA_QROWS * GRID_W, NA_KROWS * GRID_W), F32),
        ],
        compiler_params=_params(2),
        name="na_latent",
    )(rpb.reshape(nh, NA_DR * NA_DC), qkv, qkv, qkv, qkv, qkv)
    ctx = pl.pallas_call(
        _na_ctx_kernel,
        grid=(BATCH, nh),
        in_specs=[
            pl.BlockSpec((CTX_LEN, dh), lambda b, h: (ctx_blk0 + b, h)),
            pl.BlockSpec((CTX_LEN, dh), lambda b, h: (ctx_blk0 + b, nh + h)),
            pl.BlockSpec((CTX_LEN, dh), lambda b, h: (ctx_blk0 + b, 2 * nh + h)),
        ],
        out_specs=pl.BlockSpec((CTX_LEN, dh), lambda b, h: (b, h)),
        out_shape=jax.ShapeDtypeStruct((M_CTX, D_MODEL), BF16),
        compiler_params=_params(2),
        name="na_context",
    )(qkv, qkv, qkv)
    return lax.dynamic_update_slice(lat, ctx, (M_LAT, 0))


RNN_LEN = CTX_LEN + SEQ
SCAN_UNROLL = 8


def _shift_rows(x, offset, length):
    if offset == 0:
        return x
    t = lax.broadcasted_iota(jnp.int32, (length, 1), 0)
    rolled = pltpu.roll(x, (-offset) % length, 0)
    return jnp.where((t + offset >= 0) & (t + offset < length), rolled, 0.0)


def _rnn_conv(x, cw, cb):
    length = x.shape[0]
    left = RNN_CONV // 2
    y = cb
    for j in range(RNN_CONV):
        y = y + _shift_rows(x, j - left, length) * cw[j:j + 1]
    return y


def _block_scan(a, b, reverse):
    row = lax.broadcasted_iota(jnp.int32, a.shape, 0)
    for s in (1, 2, 4):
        if reverse:
            keep = row < SUBLANES - s
            shift = SUBLANES - s
        else:
            keep = row >= s
            shift = s
        a_sh = pltpu.roll(a, shift, 0)
        b_sh = pltpu.roll(b, shift, 0)
        b = jnp.where(keep, a * b_sh + b, b)
        a = jnp.where(keep, a * a_sh, a)
    return a, b


def _rglru_kernel(xc_ref, xl_ref, yc_ref, yl_ref, cw_ref, cb_ref, wg_ref, bg_ref, lam_ref,
                  oc_ref, ol_ref, af_ref, bf_ref, ar_ref, br_ref, hf_ref, hr_ref):
    cw = cw_ref[...]
    cb = cb_ref[...]
    xr_c = _rnn_conv(xc_ref[...].astype(F32), cw, cb)
    xr_l = _rnn_conv(xl_ref[...].astype(F32), cw, cb)

    def half_tanh_gate(xb, d, gate):
        w = (0.5 * wg_ref[2 * d + gate]).astype(BF16)
        return jnp.tanh(jnp.dot(xb, w, preferred_element_type=F32) + 0.5 * bg_ref[2 * d + gate])

    def gates(xr, d, a_ref, b_ref, rows):
        xb = xr.astype(BF16)
        half_xr = 0.5 * xr
        t_r = half_tanh_gate(xb, d, 0)
        t_i = half_tanh_gate(xb, d, 1)
        neg_lam = -lam_ref[d:d + 1, :]
        softplus = jnp.maximum(neg_lam, 0.0) + jnp.log1p(jnp.exp(-jnp.abs(neg_lam)))
        c = (-0.5 * RG_C * LOG2_E) * softplus
        a = jnp.exp2(c * t_r + c)
        a_ref[rows, :] = a
        y = 1.0 - a * a
        root = jnp.where(y > 0.0, y * lax.rsqrt(y), 0.0)
        b_ref[rows, :] = root * (half_xr * t_i + half_xr)

    ctx_f, lat_f = slice(0, CTX_LEN), slice(CTX_LEN, RNN_LEN)
    lat_r, ctx_r = slice(0, SEQ), slice(SEQ, RNN_LEN)
    gates(xr_c, 0, af_ref, bf_ref, ctx_f)
    gates(xr_l, 0, af_ref, bf_ref, lat_f)
    gates(xr_c, 1, ar_ref, br_ref, ctx_r)
    gates(xr_l, 1, ar_ref, br_ref, lat_r)

    nblk = RNN_LEN // SUBLANES
    last_row = slice(SUBLANES - 1, SUBLANES)
    first_row = slice(0, 1)
    blk_shape = (SUBLANES, RNN_HEAD_DIM)

    def block(n, carry):
        cf, cr = carry
        rows_f = pl.ds(pl.multiple_of(n * SUBLANES, SUBLANES), SUBLANES)
        rows_r = pl.ds(pl.multiple_of((nblk - 1 - n) * SUBLANES, SUBLANES), SUBLANES)
        a_f, b_f = _block_scan(af_ref[rows_f, :], bf_ref[rows_f, :], reverse=False)
        a_r, b_r = _block_scan(ar_ref[rows_r, :], br_ref[rows_r, :], reverse=True)
        hf_ref[rows_f, :] = a_f * cf + b_f
        hr_ref[rows_r, :] = a_r * cr + b_r
        cf = jnp.broadcast_to(a_f[last_row], blk_shape) * cf + jnp.broadcast_to(b_f[last_row], blk_shape)
        cr = jnp.broadcast_to(a_r[first_row], blk_shape) * cr + jnp.broadcast_to(b_r[first_row], blk_shape)
        return cf, cr

    zero = jnp.zeros(blk_shape, F32)
    lax.fori_loop(0, nblk, block, (zero, zero), unroll=SCAN_UNROLL)

    h_c = hf_ref[ctx_f, :] + hr_ref[ctx_r, :]
    h_l = hf_ref[lat_f, :] + hr_ref[lat_r, :]
    oc_ref[...] = (jax.nn.gelu(yc_ref[...].astype(F32)) * h_c).astype(oc_ref.dtype)
    ol_ref[...] = (jax.nn.gelu(yl_ref[...].astype(F32)) * h_l).astype(ol_ref.dtype)


def _rglru(xin, conv_w, conv_b, w_gate, b_gate, lam, j):
    dh, nh = RNN_HEAD_DIM, RNN_HEADS
    n_c = conv_w.shape[0]
    ctx_blk0 = M_LAT // CTX_LEN
    wg = w_gate.reshape(n_c, 4, nh, dh, dh)
    bg = b_gate.reshape(n_c, 4, nh, 1, dh)
    out_c, out_l = pl.pallas_call(
        _rglru_kernel,
        grid=(BATCH, nh),
        in_specs=[
            pl.BlockSpec((CTX_LEN, dh), lambda b, h: (ctx_blk0 + b, nh + h)),
            pl.BlockSpec((SEQ, dh), lambda b, h: (b, nh + h)),
            pl.BlockSpec((CTX_LEN, dh), lambda b, h: (ctx_blk0 + b, h)),
            pl.BlockSpec((SEQ, dh), lambda b, h: (b, h)),
            pl.BlockSpec((None, RNN_CONV, dh), lambda b, h: (j, 0, h)),
            pl.BlockSpec((None, 1, dh), lambda b, h: (j, 0, h)),
            pl.BlockSpec((None, 4, None, dh, dh), lambda b, h: (j, 0, h, 0, 0)),
            pl.BlockSpec((None, 4, None, 1, dh), lambda b, h: (j, 0, h, 0, 0)),
            pl.BlockSpec((None, 2, dh), lambda b, h: (j, 0, h)),
        ],
        out_specs=[
            pl.BlockSpec((CTX_LEN, dh), lambda b, h: (b, h)),
            pl.BlockSpec((SEQ, dh), lambda b, h: (b, h)),
        ],
        out_shape=[
            jax.ShapeDtypeStruct((M_CTX, RNN_WIDTH), BF16),
            jax.ShapeDtypeStruct((M_ALL, RNN_WIDTH), BF16),
        ],
        scratch_shapes=[pltpu.VMEM((RNN_LEN, dh), F32)] * 6,
        compiler_params=_params(2),
        name="rglru",
    )(xin, xin, xin, xin, conv_w, conv_b.reshape(n_c, 1, RNN_WIDTH), wg, bg, lam)
    return lax.dynamic_update_slice(out_l, out_c, (M_LAT, 0))


def kernel(x, c, ctx, c_ctx, ada_w, ada_b, norm_g, ffn_w_up, ffn_conv_w, ffn_conv_b, ffn_w_down,
           a_w_in, a_g_v, a_w_s, a_b_s, a_w_out, b_w_qkv, b_rpb, b_w_out,
           c_w_in, c_conv_w, c_conv_b, c_w_gate, c_b_gate, c_lam, c_w_out, final_g):
    h = (x.reshape(M_LAT, D_MODEL), ctx.reshape(M_CTX, D_MODEL))
    cvec = jnp.concatenate(
        [c, c_ctx[None], jnp.zeros((SUBLANES - BATCH - 1, D_MODEL), F32)], axis=0)
    mod = _ada_mod(cvec, ada_w, ada_b)

    for i in range(DEPTH):
        last = i == DEPTH - 1
        kind, j = i % N_MIXERS, i // N_MIXERS
        ffn_rows = M_LAT if last else M_ALL
        n_rows = M_LAT if (last and kind == 0) else M_ALL

        n = _norm_mod(h, norm_g, mod, i, 0, n_rows)
        if kind == 0:
            z = _matmul(n, a_w_in, j, n_rows=n_rows, tn=1024, out_dtype=BF16, act="gelu")
            y = _sgu(z, a_g_v, a_w_s, a_b_s, j, n_rows=n_rows)
            w_out = a_w_out
        elif kind == 1:
            qkv = _matmul(n, b_w_qkv, j, n_rows=n_rows, tn=1024, out_dtype=BF16)
            y = _na_attention(qkv, b_rpb[j])
            w_out = b_w_out
        else:
            xin = _matmul(n, c_w_in, j, n_rows=n_rows, tn=1024, out_dtype=F32)
            y = _rglru(xin, c_conv_w, c_conv_b, c_w_gate, c_b_gate, c_lam, j)
            w_out = c_w_out
        h = _matmul_resid(y, w_out, j, h, mod, i, 2, n_rows=n_rows, tn=1024)

        n = _norm_mod(h, norm_g, mod, i, 1, ffn_rows)
        a = _ffn_up(n, ffn_w_up, ffn_conv_w, ffn_conv_b, i, n_rows=ffn_rows)
        h = _matmul_resid(a, ffn_w_down, i, h, mod, i, 5, n_rows=ffn_rows, tn=512)

    return _final_norm(h, final_g).reshape(BATCH, SEQ, D_MODEL)
```

```python
import functools

import jax
import jax.numpy as jnp
from jax import lax
from jax.experimental import pallas as pl
from jax.experimental.pallas import tpu as pltpu

F32 = jnp.float32
BF16 = jnp.bfloat16

D_MODEL = 2048
BATCH = 2
SEQ = 4096
DEPTH = 4
GRID_W = 64
CTX_LEN = 256
N_MIXERS = 3
N_MOD = 6
EPS = 1e-6
NEG_INF = -1e30
LOG2_E = 1.4426950408889634
D_FF = 5632
CHUNK = 128
A_WIDTH = 2 * D_MODEL
A_GROUPS = 16
NA_HEADS = 16
NA_HEAD_DIM = D_MODEL // NA_HEADS
NA_KH = 8
NA_KW = 16
RNN_WIDTH = D_MODEL
RNN_HEADS = 16
RNN_HEAD_DIM = RNN_WIDTH // RNN_HEADS
RNN_CONV = 4
RG_C = 8.0

M_LAT = BATCH * SEQ
M_CTX = BATCH * CTX_LEN
M_ALL = M_LAT + M_CTX
GRID_ROWS = SEQ // GRID_W

VMEM_LIMIT_BYTES = 56 * 1024 * 1024
SUBLANES = 8
LANES = 128
BF16_ROWS = 16

ROW_TILE = {M_ALL: M_ALL // 8, M_LAT: M_LAT // 8}
ROW_TILE_WIDE_K = {M_ALL: M_ALL // 16, M_LAT: M_LAT // 16}
NORM_ROW_TILE = 512
NORM_CHUNK = BF16_ROWS


def _params(n_axes):
    return pltpu.CompilerParams(
        dimension_semantics=("arbitrary",) * n_axes, vmem_limit_bytes=VMEM_LIMIT_BYTES)


def _sigmoid(x):
    return 0.5 * jnp.tanh(0.5 * x) + 0.5


def _mod_spec(layer, which, tn=D_MODEL, grid_rank=1):
    if grid_rank == 1:
        return pl.BlockSpec((None, SUBLANES, tn), lambda i: (layer * N_MOD + which, 0, 0))
    return pl.BlockSpec((None, SUBLANES, tn), lambda j, i: (layer * N_MOD + which, 0, j))


def _rows_of_group(mod_ref, row0, tm):
    r = row0 + lax.broadcasted_iota(jnp.int32, (tm, 1), 0)
    m = mod_ref[...]
    out = m[BATCH:BATCH + 1]
    for b in reversed(range(BATCH)):
        out = jnp.where(r < (b + 1) * SEQ, m[b:b + 1], out)
    return out


def _split_stream_specs(tm, tn, row_of, col_of):
    assert M_LAT % tm == 0 and tm == M_CTX
    last_lat = M_LAT // tm - 1
    return [
        pl.BlockSpec((tm, tn), lambda *ids: (jnp.minimum(row_of(*ids), last_lat), col_of(*ids))),
        pl.BlockSpec((tm, tn), lambda *ids: (0, col_of(*ids))),
    ]


def _stream_tile(h_refs, i, tm, rows=slice(None)):
    if len(h_refs) == 1:
        return h_refs[0][rows, :]
    lat_ref, ctx_ref = h_refs
    return jnp.where(i < M_LAT // tm, lat_ref[rows, :], ctx_ref[rows, :])


def _ada_kernel(c_ref, w_ref, b_ref, o_ref):
    c = c_ref[...]
    s = (c * _sigmoid(c)).astype(BF16)
    o_ref[...] = jnp.dot(s, w_ref[...].astype(BF16), preferred_element_type=F32) + b_ref[...]


def _ada_mod(cvec, ada_w, ada_b):
    depth, d, n = ada_w.shape
    tn = 2048
    out = pl.pallas_call(
        _ada_kernel,
        grid=(depth, n // tn),
        in_specs=[
            pl.BlockSpec((SUBLANES, d), lambda l, j: (0, 0)),
            pl.BlockSpec((None, d, tn), lambda l, j: (l, 0, j)),
            pl.BlockSpec((None, 1, tn), lambda l, j: (l, 0, j)),
        ],
        out_specs=pl.BlockSpec((None, SUBLANES, tn), lambda l, j: (l, 0, j)),
        out_shape=jax.ShapeDtypeStruct((depth, SUBLANES, n), F32),
        compiler_params=_params(2),
        name="ada_mod",
    )(cvec, ada_w, ada_b.reshape(depth, 1, n))
    return out.reshape(depth, SUBLANES, N_MOD, d).transpose(0, 2, 1, 3).reshape(depth * N_MOD, SUBLANES, d)


def _norm_mod_kernel(*refs):
    *h_refs, g_ref, shift_ref, scale_ref, o_ref = refs
    i = pl.program_id(0)
    tm = o_ref.shape[0]
    group = pl.ds(jnp.minimum(i * tm // SEQ, BATCH), 1)
    gain = g_ref[...] * (1 + scale_ref[group, :])
    shift = shift_ref[group, :]

    def chunk(r, carry):
        rows = pl.ds(pl.multiple_of(r * NORM_CHUNK, NORM_CHUNK), NORM_CHUNK)
        x = _stream_tile(h_refs, i, tm, rows)
        y = x * lax.rsqrt(jnp.mean(x * x, axis=-1, keepdims=True) + EPS)
        o_ref[rows, :] = (y * gain + shift).astype(o_ref.dtype)
        return carry

    lax.fori_loop(0, tm // NORM_CHUNK, chunk, 0, unroll=True)


def _norm_mod(h, norm_g, mod, layer, which_norm, n_rows):
    tm = NORM_ROW_TILE
    g = norm_g.reshape(DEPTH * 2, 1, D_MODEL)
    h_arrays = h if isinstance(h, tuple) else (h,)
    h_specs = (_split_stream_specs(tm, D_MODEL, lambda i: i, lambda i: 0) if isinstance(h, tuple)
               else [pl.BlockSpec((tm, D_MODEL), lambda i: (i, 0))])
    return pl.pallas_call(
        _norm_mod_kernel,
        grid=(n_rows // tm,),
        in_specs=[
            *h_specs,
            pl.BlockSpec((None, 1, D_MODEL), lambda i: (layer * 2 + which_norm, 0, 0)),
            _mod_spec(layer, 3 * which_norm),
            _mod_spec(layer, 3 * which_norm + 1),
        ],
        out_specs=pl.BlockSpec((tm, D_MODEL), lambda i: (i, 0)),
        out_shape=jax.ShapeDtypeStruct((M_ALL, D_MODEL), BF16),
        compiler_params=_params(1),
        name="norm_mod",
    )(*h_arrays, g, mod, mod)


def _final_norm_kernel(h_ref, g_ref, o_ref):
    gain = g_ref[...]

    def chunk(r, carry):
        rows = pl.ds(pl.multiple_of(r * NORM_CHUNK, NORM_CHUNK), NORM_CHUNK)
        x = h_ref[rows, :]
        o_ref[rows, :] = x * lax.rsqrt(jnp.mean(x * x, axis=-1, keepdims=True) + EPS) * gain
        return carry

    lax.fori_loop(0, h_ref.shape[0] // NORM_CHUNK, chunk, 0, unroll=True)


def _final_norm(h, final_g):
    tm = NORM_ROW_TILE
    return pl.pallas_call(
        _final_norm_kernel,
        grid=(M_LAT // tm,),
        in_specs=[
            pl.BlockSpec((tm, D_MODEL), lambda i: (i, 0)),
            pl.BlockSpec((1, D_MODEL), lambda i: (0, 0)),
        ],
        out_specs=pl.BlockSpec((tm, D_MODEL), lambda i: (i, 0)),
        out_shape=jax.ShapeDtypeStruct((M_LAT, D_MODEL), F32),
        compiler_params=_params(1),
        name="final_norm",
    )(h, final_g.reshape(1, D_MODEL))


def _mm_kernel(x_ref, w_ref, o_ref, wbf_ref, *, act):
    @pl.when(pl.program_id(1) == 0)
    def _():
        wbf_ref[...] = w_ref[...].astype(BF16)

    acc = jnp.dot(x_ref[...], wbf_ref[...], preferred_element_type=F32)
    if act == "gelu":
        acc = jax.nn.gelu(acc)
    o_ref[...] = acc.astype(o_ref.dtype)


def _mm_resid_kernel(x_ref, w_ref, gate_ref, *refs):
    *res_refs, o_ref, wbf_ref = refs
    i = pl.program_id(1)
    tm = x_ref.shape[0]

    @pl.when(i == 0)
    def _():
        wbf_ref[...] = w_ref[...].astype(BF16)

    acc = jnp.dot(x_ref[...], wbf_ref[...], preferred_element_type=F32)
    o_ref[...] = _stream_tile(res_refs, i, tm) + _rows_of_group(gate_ref, i * tm, tm) * acc


def _matmul(x, w, w_idx, *, n_rows, tn, out_dtype, act=None):
    tm = ROW_TILE[n_rows]
    _, k, n = w.shape
    return pl.pallas_call(
        functools.partial(_mm_kernel, act=act),
        grid=(n // tn, n_rows // tm),
        in_specs=[
            pl.BlockSpec((tm, k), lambda j, i: (i, 0)),
            pl.BlockSpec((None, k, tn), lambda j, i: (w_idx, 0, j)),
        ],
        out_specs=pl.BlockSpec((tm, tn), lambda j, i: (i, j)),
        out_shape=jax.ShapeDtypeStruct((x.shape[0], n), out_dtype),
        scratch_shapes=[pltpu.VMEM((k, tn), BF16)],
        compiler_params=_params(2),
        name="matmul",
    )(x, w)


def _matmul_resid(x, w, w_idx, h, mod, layer, which_gate, *, n_rows, tn):
    _, k, n = w.shape
    w_tile_bytes = k * tn * 4
    w_buffers = pl.Buffered(1) if w_tile_bytes > 12 * 2**20 else None
    if isinstance(h, tuple):
        tm = M_CTX
        res_arrays, aliases = h, {}
        res_specs = _split_stream_specs(tm, tn, lambda j, i: i, lambda j, i: j)
    else:
        tm = (ROW_TILE_WIDE_K if w_tile_bytes > 8 * 2**20 else ROW_TILE)[n_rows]
        res_arrays, aliases = (h,), {3: 0}
        res_specs = [pl.BlockSpec((tm, tn), lambda j, i: (i, j))]
    return pl.pallas_call(
        _mm_resid_kernel,
        grid=(n // tn, n_rows // tm),
        in_specs=[
            pl.BlockSpec((tm, k), lambda j, i: (i, 0)),
            pl.BlockSpec((None, k, tn), lambda j, i: (w_idx, 0, j), pipeline_mode=w_buffers),
            _mod_spec(layer, which_gate, tn, grid_rank=2),
            *res_specs,
        ],
        out_specs=pl.BlockSpec((tm, tn), lambda j, i: (i, j)),
        out_shape=jax.ShapeDtypeStruct((M_ALL, n), F32),
        scratch_shapes=[pltpu.VMEM((k, tn), BF16)],
        input_output_aliases=aliases,
        compiler_params=_params(2),
        name="matmul_resid",
    )(x, w, mod, *res_arrays)


HALO = BF16_ROWS


SEQ_STARTS = tuple(b * SEQ for b in range(BATCH)) + tuple(M_LAT + b * CTX_LEN for b in range(BATCH))
FIX_ROWS = BF16_ROWS


def _is_seq_edge(row):
    edge = row == M_ALL
    for s in SEQ_STARTS:
        edge = edge | (row == s)
    return edge


def _conv3(prev, cur, nxt, cw_ref, cb_ref):
    cw = cw_ref[...]
    y = cb_ref[...] + prev * cw[0:1]
    y = y + cur * cw[1:2]
    return y + nxt * cw[2:3]


def _silu_gate(g, v):
    half_g = 0.5 * g
    return (half_g * jnp.tanh(half_g) + half_g) * v


def _ffn_up_kernel(x_ref, xp_ref, xn_ref, wg_ref, wv_ref, cwg_ref, cwv_ref, cbg_ref, cbv_ref,
                   o_ref, wg_bf, wv_bf, xs_ref, zg_ref, zv_ref):
    i = pl.program_id(1)
    tm = x_ref.shape[0]
    row0 = i * tm

    @pl.when(i == 0)
    def _():
        wg_bf[...] = wg_ref[...].astype(BF16)
        wv_bf[...] = wv_ref[...].astype(BF16)

    no_rows = jnp.zeros_like(xp_ref)
    xs_ref[0:HALO, :] = jnp.where(_is_seq_edge(row0), no_rows, xp_ref[...])
    xs_ref[HALO:HALO + tm, :] = x_ref[...]
    xs_ref[HALO + tm:, :] = jnp.where(_is_seq_edge(row0 + tm), no_rows, xn_ref[...])
    xs = xs_ref[...]
    zg_ref[...] = jnp.dot(xs, wg_bf[...], preferred_element_type=F32)
    zv_ref[...] = jnp.dot(xs, wv_bf[...], preferred_element_type=F32)

    def conv(z_ref, cw_ref, cb_ref):
        return _conv3(z_ref[HALO - 1:HALO - 1 + tm, :], z_ref[HALO:HALO + tm, :],
                      z_ref[HALO + 1:HALO + 1 + tm, :], cw_ref, cb_ref)

    o_ref[...] = _silu_gate(conv(zg_ref, cwg_ref, cbg_ref), conv(zv_ref, cwv_ref, cbv_ref)).astype(o_ref.dtype)

    n_win = 2 * FIX_ROWS + 2 * SUBLANES
    w_row = lax.broadcasted_iota(jnp.int32, (n_win, 1), 0)
    start_row = SUBLANES + FIX_ROWS

    def fixed_conv(z_ref, cw_ref, cb_ref, win):
        z = z_ref[win, :]
        prev = jnp.where(w_row == start_row, 0.0, pltpu.roll(z, 1, 0))
        nxt = jnp.where(w_row == start_row - 1, 0.0, pltpu.roll(z, n_win - 1, 0))
        return _conv3(prev, z, nxt, cw_ref, cb_ref)[SUBLANES:SUBLANES + 2 * FIX_ROWS]

    for s in SEQ_STARTS[1:]:
        @pl.when((s > row0) & (s < row0 + tm))
        def _():
            win = pl.ds(pl.multiple_of(HALO + s - row0 - start_row, SUBLANES), n_win)
            rows = pl.ds(pl.multiple_of(s - row0 - FIX_ROWS, FIX_ROWS), 2 * FIX_ROWS)
            g = fixed_conv(zg_ref, cwg_ref, cbg_ref, win)
            v = fixed_conv(zv_ref, cwv_ref, cbv_ref, win)
            o_ref[rows, :] = _silu_gate(g, v).astype(o_ref.dtype)


def _ffn_up(n, w_up, conv_w, conv_b, layer, *, n_rows):
    tm, tn = ROW_TILE[n_rows], 512
    k = w_up.shape[1]
    nj = D_FF // tn
    halo_per_tile = tm // HALO
    last_halo = M_ALL // HALO - 1
    cb = conv_b.reshape(DEPTH, 1, 2 * D_FF)
    return pl.pallas_call(
        _ffn_up_kernel,
        grid=(nj, n_rows // tm),
        in_specs=[
            pl.BlockSpec((tm, k), lambda j, i: (i, 0)),
            pl.BlockSpec((HALO, k), lambda j, i: (jnp.maximum(i * halo_per_tile - 1, 0), 0)),
            pl.BlockSpec((HALO, k), lambda j, i: (jnp.minimum((i + 1) * halo_per_tile, last_halo), 0)),
            pl.BlockSpec((None, k, tn), lambda j, i: (layer, 0, j)),
            pl.BlockSpec((None, k, tn), lambda j, i: (layer, 0, j + nj)),
            pl.BlockSpec((None, 3, tn), lambda j, i: (layer, 0, j)),
            pl.BlockSpec((None, 3, tn), lambda j, i: (layer, 0, j + nj)),
            pl.BlockSpec((None, 1, tn), lambda j, i: (layer, 0, j)),
            pl.BlockSpec((None, 1, tn), lambda j, i: (layer, 0, j + nj)),
        ],
        out_specs=pl.BlockSpec((tm, tn), lambda j, i: (i, j)),
        out_shape=jax.ShapeDtypeStruct((M_ALL, D_FF), BF16),
        scratch_shapes=[
            pltpu.VMEM((k, tn), BF16),
            pltpu.VMEM((k, tn), BF16),
            pltpu.VMEM((tm + 2 * HALO, k), BF16),
            pltpu.VMEM((tm + 2 * HALO, tn), F32),
            pltpu.VMEM((tm + 2 * HALO, tn), F32),
        ],
        compiler_params=_params(2),
        name="ffn_up_conv_gate",
    )(n, n, n, w_up, w_up, conv_w, conv_w, cb, cb)


def _sgu_kernel(u_ref, v_ref, gv_ref, ws_ref, bs_ref, o_ref):
    tm = u_ref.shape[0]
    gw = A_WIDTH // A_GROUPS
    gv = gv_ref[...]
    for c in range(tm // CHUNK):
        rows = slice(c * CHUNK, (c + 1) * CHUNK)
        v = v_ref[rows, :].astype(F32)
        vn = v * lax.rsqrt(jnp.mean(v * v, axis=-1, keepdims=True) + EPS) * gv
        vn = vn.astype(BF16)
        for g in range(A_GROUPS):
            cols = slice(g * gw, (g + 1) * gw)
            s = jnp.dot(ws_ref[g].astype(BF16), vn[:, cols], preferred_element_type=F32)
            s = s + bs_ref[:, g:g + 1]
            o_ref[rows, cols] = (u_ref[rows, cols].astype(F32) * s).astype(o_ref.dtype)


def _sgu(z, g_v, w_s, b_s, j, *, n_rows):
    tm = 4 * CHUNK
    n_a = g_v.shape[0]
    return pl.pallas_call(
        _sgu_kernel,
        grid=(n_rows // tm,),
        in_specs=[
            pl.BlockSpec((tm, A_WIDTH), lambda i: (i, 0)),
            pl.BlockSpec((tm, A_WIDTH), lambda i: (i, 1)),
            pl.BlockSpec((None, 1, A_WIDTH), lambda i: (j, 0, 0)),
            pl.BlockSpec((None, A_GROUPS, CHUNK, CHUNK), lambda i: (j, 0, 0, 0)),
            pl.BlockSpec((None, CHUNK, A_GROUPS), lambda i: (j, 0, 0)),
        ],
        out_specs=pl.BlockSpec((tm, A_WIDTH), lambda i: (i, 0)),
        out_shape=jax.ShapeDtypeStruct((M_ALL, A_WIDTH), BF16),
        compiler_params=_params(1),
        name="sgu",
    )(z, z, g_v.reshape(n_a, 1, A_WIDTH), w_s, b_s.transpose(0, 2, 1))


NA_DR = 2 * NA_KH - 1
NA_DC = 2 * NA_KW - 1
NA_PAIR_TILES = NA_DR + 1
NA_QROWS = 4
NA_KROWS = NA_KH + NA_QROWS
NA_BLOCK_UNROLL = 14


def _na_bias_pairs(rpb_ref, head, pair_ref):
    shape = (GRID_W, 2 * GRID_W)
    q = lax.broadcasted_iota(jnp.int32, shape, 0)
    lane = lax.broadcasted_iota(jnp.int32, shape, 1)
    kcol = lane & (GRID_W - 1)
    upper = lane >= GRID_W
    dc = jnp.clip(kcol - q, -(NA_KW - 1), NA_KW - 1) + NA_KW - 1
    c0 = jnp.clip(q - NA_KW // 2, 0, GRID_W - NA_KW)
    ok = (kcol >= c0) & (kcol < c0 + NA_KW)
    for d in range(-1, NA_DR):
        acc = jnp.full(shape, NEG_INF, F32)
        for c in range(NA_DC):
            lo = rpb_ref[head, d * NA_DC + c] if d >= 0 else NEG_INF
            hi = rpb_ref[head, (d + 1) * NA_DC + c] if d + 1 < NA_DR else NEG_INF
            acc = jnp.where(dc == c, jnp.where(upper, hi, lo), acc)
        pair_ref[d + 1] = jnp.where(ok, acc * LOG2_E, NEG_INF)


def _na_window_start(m):
    return min(max(m * NA_QROWS - NA_KH // 2, 0), GRID_ROWS - NA_KROWS)


def _na_block_bias(pair_ref, m):
    upper = lax.broadcasted_iota(jnp.int32, (GRID_W, 2 * GRID_W), 1) >= GRID_W
    masked = jnp.full((GRID_W, 2 * GRID_W), NEG_INF, F32)
    ws = _na_window_start(m)
    rows = []
    for a in range(NA_QROWS):
        r = m * NA_QROWS + a
        r0 = min(max(r - NA_KH // 2, 0), GRID_ROWS - NA_KH)
        tiles = []
        for p in range(NA_KROWS // 2):
            key_row = ws + 2 * p
            in_lo = r0 <= key_row < r0 + NA_KH
            in_hi = r0 <= key_row + 1 < r0 + NA_KH
            if not (in_lo or in_hi):
                tiles.append(masked)
                continue
            tile = pair_ref[key_row - r + NA_KH]
            if not in_hi:
                tile = jnp.where(upper, NEG_INF, tile)
            elif not in_lo:
                tile = jnp.where(upper, tile, NEG_INF)
            tiles.append(tile)
        rows.append(jnp.concatenate(tiles, axis=1))
    return jnp.concatenate(rows, axis=0)


def _na_lat_kernel(rpb_ref, q_ref, k_ref, v_ref, kc_ref, vc_ref, o_ref, pair_ref, bias_ref):
    scale = NA_HEAD_DIM ** -0.5 * LOG2_E

    @pl.when(pl.program_id(1) == 0)
    def _():
        _na_bias_pairs(rpb_ref, pl.program_id(0), pair_ref)
        bias_ref[...] = _na_block_bias(pair_ref, 1)

    kc = kc_ref[...]
    vc = vc_ref[...]
    nt = (((1,), (1,)), ((), ()))
    n_q = NA_QROWS * GRID_W
    n_blocks = GRID_ROWS // NA_QROWS

    def attend(qrows, win, bias):
        q = q_ref[qrows, :]
        kw = k_ref[win, :]
        vw = v_ref[win, :]
        s =lax.dot_general(q, kw, nt, preferred_element_type=F32) * scale + bias
        sc = lax.dot_general(q, kc, nt, preferred_element_type=F32) * scale
        mx = jnp.maximum(jnp.max(s, axis=-1, keepdims=True), jnp.max(sc, axis=-1, keepdims=True))
        p_w = jnp.exp2(s - mx)
        p_c = jnp.exp2(sc - mx)
        denom = jnp.sum(p_w, axis=-1, keepdims=True) + jnp.sum(p_c, axis=-1, keepdims=True)
        o = jnp.dot(p_w.astype(BF16), vw, preferred_element_type=F32)
        o = o + jnp.dot(p_c.astype(BF16), vc, preferred_element_type=F32)
        o_ref[qrows, :] = (o / denom).astype(o_ref.dtype)

    def edge_block(m):
        ws = _na_window_start(m)
        attend(slice(m * n_q, (m + 1) * n_q), slice(ws * GRID_W, (ws + NA_KROWS) * GRID_W),
               _na_block_bias(pair_ref, m))

    def interior_block(m, carry):
        qrows = pl.ds(pl.multiple_of(m * n_q, n_q), n_q)
        ws = m * NA_QROWS - NA_KH // 2
        win = pl.ds(pl.multiple_of(ws * GRID_W, GRID_W), NA_KROWS * GRID_W)
        attend(qrows, win, bias_ref[...])
        return carry

    edge_block(0)
    lax.fori_loop(1, n_blocks - 1, interior_block, 0, unroll=NA_BLOCK_UNROLL)
    edge_block(n_blocks - 1)


def _na_ctx_kernel(q_ref, k_ref, v_ref, o_ref):
    scale = NA_HEAD_DIM ** -0.5
    for h in range(NA_HEADS):
        cols = slice(h * NA_HEAD_DIM, (h + 1) * NA_HEAD_DIM)
        s = lax.dot_general(q_ref[:, cols], k_ref[:, cols], (((1,), (1,)), ((), ())),
                            preferred_element_type=F32) * scale
        m = jnp.max(s, axis=-1, keepdims=True)
        p = jnp.exp(s - m)
        denom = jnp.sum(p, axis=-1, keepdims=True)
        o = jnp.dot(p.astype(BF16), v_ref[:, cols], preferred_element_type=F32)
        o_ref[:, cols] = (o / denom).astype(o_ref.dtype)


def _na_attention(qkv, rpb):
    dh, nh = NA_HEAD_DIM, NA_HEADS
    ctx_blk0 = M_LAT // CTX_LEN
    lat = pl.pallas_call(
        _na_lat_kernel,
        grid=(nh, BATCH),
        in_specs=[
            pl.BlockSpec(memory_space=pltpu.SMEM),
            pl.BlockSpec((SEQ, dh), lambda h, b: (b, h)),
            pl.BlockSpec((SEQ, dh), lambda h, b: (b, nh + h)),
            pl.BlockSpec((SEQ, dh), lambda h, b: (b, 2 * nh + h)),
            pl.BlockSpec((CTX_LEN, dh), lambda h, b: (ctx_blk0 + b, nh + h)),
            pl.BlockSpec((CTX_LEN, dh), lambda h, b: (ctx_blk0 + b, 2 * nh + h)),
        ],
        out_specs=pl.BlockSpec((SEQ, dh), lambda h, b: (b, h)),
        out_shape=jax.ShapeDtypeStruct((M_ALL, D_MODEL), BF16),
        scratch_shapes=[
            pltpu.VMEM((NA_PAIR_TILES, GRID_W, 2 * GRID_W), F32),
            pltpu.VMEM((NA_QROWS * GRID_W, NA_KROWS * GRID_W), F32),
        ],
        compiler_params=_params(2),
        name="na_latent",
    )(rpb.reshape(nh, NA_DR * NA_DC), qkv, qkv, qkv, qkv, qkv)
    ctx = pl.pallas_call(
        _na_ctx_kernel,
        grid=(BATCH,),
        in_specs=[
            pl.BlockSpec((CTX_LEN, D_MODEL), lambda b: (ctx_blk0 + b, 0)),
            pl.BlockSpec((CTX_LEN, D_MODEL), lambda b: (ctx_blk0 + b, 1)),
            pl.BlockSpec((CTX_LEN, D_MODEL), lambda b: (ctx_blk0 + b, 2)),
        ],
        out_specs=pl.BlockSpec((CTX_LEN, D_MODEL), lambda b: (b, 0)),
        out_shape=jax.ShapeDtypeStruct((M_CTX, D_MODEL), BF16),
        compiler_params=_params(1),
        name="na_context",
    )(qkv, qkv, qkv)
    return lax.dynamic_update_slice(lat, ctx, (M_LAT, 0))


RNN_LEN = CTX_LEN + SEQ
SCAN_UNROLL = 8


def _shift_rows(x, offset, length):
    if offset == 0:
        return x
    t = lax.broadcasted_iota(jnp.int32, (length, 1), 0)
    rolled = pltpu.roll(x, (-offset) % length, 0)
    return jnp.where((t + offset >= 0) & (t + offset < length), rolled, 0.0)


def _rnn_conv(x, cw, cb):
    length = x.shape[0]
    left = RNN_CONV // 2
    y = cb
    for j in range(RNN_CONV):
        y = y + _shift_rows(x, j - left, length) * cw[j:j + 1]
    return y


def _block_scan(a, b, reverse):
    row = lax.broadcasted_iota(jnp.int32, a.shape, 0)
    for s in (1, 2, 4):
        if reverse:
            keep = row < SUBLANES - s
            shift = SUBLANES - s
        else:
            keep = row >= s
            shift = s
        a_sh = pltpu.roll(a, shift, 0)
        b_sh = pltpu.roll(b, shift, 0)
        b = jnp.where(keep, a * b_sh + b, b)
        a = jnp.where(keep, a * a_sh, a)
    return a, b


def _rglru_kernel(xc_ref, xl_ref, yc_ref, yl_ref, cw_ref, cb_ref, wg_ref, bg_ref, lam_ref,
                  oc_ref, ol_ref, af_ref, bf_ref, ar_ref, br_ref, hf_ref, hr_ref):
    cw = cw_ref[...]
    cb = cb_ref[...]
    xr_c = _rnn_conv(xc_ref[...].astype(F32), cw, cb)
    xr_l = _rnn_conv(xl_ref[...].astype(F32), cw, cb)

    def half_tanh_gate(xb, d, gate):
        w = (0.5 * wg_ref[2 * d + gate]).astype(BF16)
        return jnp.tanh(jnp.dot(xb, w, preferred_element_type=F32) + 0.5 * bg_ref[2 * d + gate])

    def gates(xr, d, a_ref, b_ref, rows):
        xb = xr.astype(BF16)
        half_xr = 0.5 * xr
        t_r = half_tanh_gate(xb, d, 0)
        t_i = half_tanh_gate(xb, d, 1)
        neg_lam = -lam_ref[d:d + 1, :]
        softplus = jnp.maximum(neg_lam, 0.0) + jnp.log1p(jnp.exp(-jnp.abs(neg_lam)))
        c = (-0.5 * RG_C * LOG2_E) * softplus
        a = jnp.exp2(c * t_r + c)
        a_ref[rows, :] = a
        y = 1.0 - a * a
        root = jnp.where(y > 0.0, y * lax.rsqrt(y), 0.0)
        b_ref[rows, :] = root * (half_xr * t_i + half_xr)

    ctx_f, lat_f = slice(0, CTX_LEN), slice(CTX_LEN, RNN_LEN)
    lat_r, ctx_r = slice(0, SEQ), slice(SEQ, RNN_LEN)
    gates(xr_c, 0, af_ref, bf_ref, ctx_f)
    gates(xr_l, 0, af_ref, bf_ref, lat_f)
    gates(xr_c, 1, ar_ref, br_ref, ctx_r)
    gates(xr_l, 1, ar_ref, br_ref, lat_r)

    nblk = RNN_LEN // SUBLANES
    last_row = slice(SUBLANES - 1, SUBLANES)
    first_row = slice(0, 1)
    blk_shape = (SUBLANES, RNN_HEAD_DIM)

    def block(n, carry):
        cf, cr = carry
        rows_f = pl.ds(pl.multiple_of(n * SUBLANES, SUBLANES), SUBLANES)
        rows_r = pl.ds(pl.multiple_of((nblk - 1 - n) * SUBLANES, SUBLANES), SUBLANES)
        a_f, b_f = _block_scan(af_ref[rows_f, :], bf_ref[rows_f, :], reverse=False)
        a_r, b_r = _block_scan(ar_ref[rows_r, :], br_ref[rows_r, :], reverse=True)
        hf_ref[rows_f, :] = a_f * cf + b_f
        hr_ref[rows_r, :] = a_r * cr + b_r
        cf = jnp.broadcast_to(a_f[last_row], blk_shape) * cf + jnp.broadcast_to(b_f[last_row], blk_shape)
        cr = jnp.broadcast_to(a_r[first_row], blk_shape) * cr + jnp.broadcast_to(b_r[first_row], blk_shape)
        return cf, cr

    zero = jnp.zeros(blk_shape, F32)
    lax.fori_loop(0, nblk, block, (zero, zero), unroll=SCAN_UNROLL)

    h_c = hf_ref[ctx_f, :] + hr_ref[ctx_r, :]
    h_l = hf_ref[lat_f, :] + hr_ref[lat_r, :]
    oc_ref[...] = (jax.nn.gelu(yc_ref[...].astype(F32)) * h_c).astype(oc_ref.dtype)
    ol_ref[...] = (jax.nn.gelu(yl_ref[...].astype(F32)) * h_l).astype(ol_ref.dtype)


def _rglru(xin, conv_w, conv_b, w_gate, b_gate, lam, j):
    dh, nh = RNN_HEAD_DIM, RNN_HEADS
    n_c = conv_w.shape[0]
    ctx_blk0 = M_LAT // CTX_LEN
    wg = w_gate.reshape(n_c, 4, nh, dh, dh)
    bg = b_gate.reshape(n_c, 4, nh, 1, dh)
    out_c, out_l = pl.pallas_call(
        _rglru_kernel,
        grid=(BATCH, nh),
        in_specs=[
            pl.BlockSpec((CTX_LEN, dh), lambda b, h: (ctx_blk0 + b, nh + h)),
            pl.BlockSpec((SEQ, dh), lambda b, h: (b, nh + h)),
            pl.BlockSpec((CTX_LEN, dh), lambda b, h: (ctx_blk0 + b, h)),
            pl.BlockSpec((SEQ, dh), lambda b, h: (b, h)),
            pl.BlockSpec((None, RNN_CONV, dh), lambda b, h: (j, 0, h)),
            pl.BlockSpec((None, 1, dh), lambda b, h: (j, 0, h)),
            pl.BlockSpec((None, 4, None, dh, dh), lambda b, h: (j, 0, h, 0, 0)),
            pl.BlockSpec((None, 4, None, 1, dh), lambda b, h: (j, 0, h, 0, 0)),
            pl.BlockSpec((None, 2, dh), lambda b, h: (j, 0, h)),
        ],
        out_specs=[
            pl.BlockSpec((CTX_LEN, dh), lambda b, h: (b, h)),
            pl.BlockSpec((SEQ, dh), lambda b, h: (b, h)),
        ],
        out_shape=[
            jax.ShapeDtypeStruct((M_CTX, RNN_WIDTH), BF16),
            jax.ShapeDtypeStruct((M_ALL, RNN_WIDTH), BF16),
        ],
        scratch_shapes=[pltpu.VMEM((RNN_LEN, dh), F32)] * 6,
        compiler_params=_params(2),
        name="rglru",
    )(xin, xin, xin, xin, conv_w, conv_b.reshape(n_c, 1, RNN_WIDTH), wg, bg, lam)
    return lax.dynamic_update_slice(out_l, out_c, (M_LAT, 0))


def kernel(x, c, ctx, c_ctx, ada_w, ada_b, norm_g, ffn_w_up, ffn_conv_w, ffn_conv_b, ffn_w_down,
           a_w_in, a_g_v, a_w_s, a_b_s, a_w_out, b_w_qkv, b_rpb, b_w_out,
           c_w_in, c_conv_w, c_conv_b, c_w_gate, c_b_gate, c_lam, c_w_out, final_g):
    h = (x.reshape(M_LAT, D_MODEL), ctx.reshape(M_CTX, D_MODEL))
    cvec = jnp.concatenate(
        [c, c_ctx[None], jnp.zeros((SUBLANES - BATCH - 1, D_MODEL), F32)], axis=0)
    mod = _ada_mod(cvec, ada_w, ada_b)

    for i in range(DEPTH):
        last = i == DEPTH - 1
        kind, j = i % N_MIXERS, i // N_MIXERS
        ffn_rows = M_LAT if last else M_ALL
        n_rows = M_LAT if (last and kind == 0) else M_ALL

        n = _norm_mod(h, norm_g, mod, i, 0, n_rows)
        if kind == 0:
            z = _matmul(n, a_w_in, j, n_rows=n_rows, tn=1024, out_dtype=BF16, act="gelu")
            y = _sgu(z, a_g_v, a_w_s, a_b_s, j, n_rows=n_rows)
            w_out = a_w_out
        elif kind == 1:
            qkv = _matmul(n, b_w_qkv, j, n_rows=n_rows, tn=1024, out_dtype=BF16)
            y = _na_attention(qkv, b_rpb[j])
            w_out = b_w_out
        else:
            xin = _matmul(n, c_w_in, j, n_rows=n_rows, tn=1024, out_dtype=F32)
            y = _rglru(xin, c_conv_w, c_conv_b, c_w_gate, c_b_gate, c_lam, j)
            w_out = c_w_out
        h = _matmul_resid(y, w_out, j, h, mod, i, 2, n_rows=n_rows, tn=1024)

        n = _norm_mod(h, norm_g, mod, i, 1, ffn_rows)
        a = _ffn_up(n, ffn_w_up, ffn_conv_w, ffn_conv_b, i, n_rows=ffn_rows)
        h = _matmul_resid(a, ffn_w_down, i, h, mod, i, 5, n_rows=ffn_rows, tn=512)

    return _final_norm(h, final_g).reshape(BATCH, SEQ, D_MODEL)
```

```python
import functools

import jax
import jax.numpy as jnp
from jax import lax
from jax.experimental import pallas as pl
from jax.experimental.pallas import tpu as pltpu

F32 = jnp.float32
BF16 = jnp.bfloat16

D_MODEL = 2048
BATCH = 2
SEQ = 4096
DEPTH = 4
GRID_W = 64
CTX_LEN = 256
N_MIXERS = 3
N_MOD = 6
EPS = 1e-6
NEG_INF = -1e30
LOG2_E = 1.4426950408889634
D_FF = 5632
CHUNK = 128
A_WIDTH = 2 * D_MODEL
A_GROUPS = 16
NA_HEADS = 16
NA_HEAD_DIM = D_MODEL // NA_HEADS
NA_KH = 8
NA_KW = 16
RNN_WIDTH = D_MODEL
RNN_HEADS = 16
RNN_HEAD_DIM = RNN_WIDTH // RNN_HEADS
RNN_CONV = 4
RG_C = 8.0

M_LAT = BATCH * SEQ
M_CTX = BATCH * CTX_LEN
M_ALL = M_LAT + M_CTX
GRID_ROWS = SEQ // GRID_W

VMEM_LIMIT_BYTES = 56 * 1024 * 1024
SUBLANES = 8
LANES = 128
BF16_ROWS = 16

ROW_TILE = {M_ALL: M_ALL // 8, M_LAT: M_LAT // 8}
ROW_TILE_HALF = {M_ALL: M_ALL // 16, M_LAT: M_LAT // 16}
RESID_VMEM_BUDGET = 50 * 1024 * 1024
NORM_ROW_TILE = 512
NORM_CHUNK = BF16_ROWS


def _params(n_axes):
    return pltpu.CompilerParams(
        dimension_semantics=("arbitrary",) * n_axes, vmem_limit_bytes=VMEM_LIMIT_BYTES)


def _sigmoid(x):
    return 0.5 * jnp.tanh(0.5 * x) + 0.5


def _mod_spec(layer, which, tn=D_MODEL, grid_rank=1):
    if grid_rank == 1:
        return pl.BlockSpec((None, SUBLANES, tn), lambda i: (layer * N_MOD + which, 0, 0))
    return pl.BlockSpec((None, SUBLANES, tn), lambda j, i: (layer * N_MOD + which, 0, j))


def _rows_of_group(mod_ref, row0, tm):
    r = row0 + lax.broadcasted_iota(jnp.int32, (tm, 1), 0)
    m = mod_ref[...]
    out = m[BATCH:BATCH + 1]
    for b in reversed(range(BATCH)):
        out = jnp.where(r < (b + 1) * SEQ, m[b:b + 1], out)
    return out


def _split_stream_specs(tm, tn, row_of, col_of):
    assert M_LAT % tm == 0 and tm == M_CTX
    last_lat = M_LAT // tm - 1
    return [
        pl.BlockSpec((tm, tn), lambda *ids: (jnp.minimum(row_of(*ids), last_lat), col_of(*ids))),
        pl.BlockSpec((tm, tn), lambda *ids: (0, col_of(*ids))),
    ]


def _stream_tile(h_refs, i, tm, rows=slice(None)):
    if len(h_refs) == 1:
        return h_refs[0][rows, :]
    lat_ref, ctx_ref = h_refs
    return jnp.where(i < M_LAT // tm, lat_ref[rows, :], ctx_ref[rows, :])


def _ada_kernel(c_ref, w_ref, b_ref, o_ref):
    c = c_ref[...]
    s = (c * _sigmoid(c)).astype(BF16)
    o_ref[...] = jnp.dot(s, w_ref[...].astype(BF16), preferred_element_type=F32) + b_ref[...]


def _ada_mod(cvec, ada_w, ada_b):
    depth, d, n = ada_w.shape
    tn = 2048
    out = pl.pallas_call(
        _ada_kernel,
        grid=(depth, n // tn),
        in_specs=[
            pl.BlockSpec((SUBLANES, d), lambda l, j: (0, 0)),
            pl.BlockSpec((None, d, tn), lambda l, j: (l, 0, j)),
            pl.BlockSpec((None, 1, tn), lambda l, j: (l, 0, j)),
        ],
        out_specs=pl.BlockSpec((None, SUBLANES, tn), lambda l, j: (l, 0, j)),
        out_shape=jax.ShapeDtypeStruct((depth, SUBLANES, n), F32),
        compiler_params=_params(2),
        name="ada_mod",
    )(cvec, ada_w, ada_b.reshape(depth, 1, n))
    return out.reshape(depth, SUBLANES, N_MOD, d).transpose(0, 2, 1, 3).reshape(depth * N_MOD, SUBLANES, d)


def _norm_mod_kernel(*refs):
    *h_refs, g_ref, shift_ref, scale_ref, o_ref = refs
    i = pl.program_id(0)
    tm = o_ref.shape[0]
    group = pl.ds(jnp.minimum(i * tm // SEQ, BATCH), 1)
    gain = g_ref[...] * (1 + scale_ref[group, :])
    shift = shift_ref[group, :]

    def chunk(r, carry):
        rows = pl.ds(pl.multiple_of(r * NORM_CHUNK, NORM_CHUNK), NORM_CHUNK)
        x = _stream_tile(h_refs, i, tm, rows)
        y = x * lax.rsqrt(jnp.mean(x * x, axis=-1, keepdims=True) + EPS)
        o_ref[rows, :] = (y * gain + shift).astype(o_ref.dtype)
        return carry

    lax.fori_loop(0, tm // NORM_CHUNK, chunk, 0, unroll=True)


def _norm_mod(h, norm_g, mod, layer, which_norm, n_rows):
    tm = NORM_ROW_TILE
    g = norm_g.reshape(DEPTH * 2, 1, D_MODEL)
    h_arrays = h if isinstance(h, tuple) else (h,)
    h_specs = (_split_stream_specs(tm, D_MODEL, lambda i: i, lambda i: 0) if isinstance(h, tuple)
               else [pl.BlockSpec((tm, D_MODEL), lambda i: (i, 0))])
    return pl.pallas_call(
        _norm_mod_kernel,
        grid=(n_rows // tm,),
        in_specs=[
            *h_specs,
            pl.BlockSpec((None, 1, D_MODEL), lambda i: (layer * 2 + which_norm, 0, 0)),
            _mod_spec(layer, 3 * which_norm),
            _mod_spec(layer, 3 * which_norm + 1),
        ],
        out_specs=pl.BlockSpec((tm, D_MODEL), lambda i: (i, 0)),
        out_shape=jax.ShapeDtypeStruct((M_ALL, D_MODEL), BF16),
        compiler_params=_params(1),
        name="norm_mod",
    )(*h_arrays, g, mod, mod)


def _final_norm_kernel(h_ref, g_ref, o_ref):
    gain = g_ref[...]

    def chunk(r, carry):
        rows = pl.ds(pl.multiple_of(r * NORM_CHUNK, NORM_CHUNK), NORM_CHUNK)
        x = h_ref[rows, :]
        o_ref[rows, :] = x * lax.rsqrt(jnp.mean(x * x, axis=-1, keepdims=True) + EPS) * gain
        return carry

    lax.fori_loop(0, h_ref.shape[0] // NORM_CHUNK, chunk, 0, unroll=True)


def _final_norm(h, final_g):
    tm = NORM_ROW_TILE
    return pl.pallas_call(
        _final_norm_kernel,
        grid=(M_LAT // tm,),
        in_specs=[
            pl.BlockSpec((tm, D_MODEL), lambda i: (i, 0)),
            pl.BlockSpec((1, D_MODEL), lambda i: (0, 0)),
        ],
        out_specs=pl.BlockSpec((tm, D_MODEL), lambda i: (i, 0)),
        out_shape=jax.ShapeDtypeStruct((M_LAT, D_MODEL), F32),
        compiler_params=_params(1),
        name="final_norm",
    )(h, final_g.reshape(1, D_MODEL))


def _mm_kernel(x_ref, w_ref, o_ref, wbf_ref, *, act):
    @pl.when(pl.program_id(1) == 0)
    def _():
        wbf_ref[...] = w_ref[...].astype(BF16)

    acc = jnp.dot(x_ref[...], wbf_ref[...], preferred_element_type=F32)
    if act == "gelu":
        acc = jax.nn.gelu(acc)
    o_ref[...] = acc.astype(o_ref.dtype)


def _mm_resid_kernel(x_ref, w_ref, gate_ref, *refs):
    *res_refs, o_ref, wbf_ref = refs
    i = pl.program_id(1)
    tm = x_ref.shape[0]

    @pl.when(i == 0)
    def _():
        wbf_ref[...] = w_ref[...].astype(BF16)

    acc = jnp.dot(x_ref[...], wbf_ref[...], preferred_element_type=F32)
    o_ref[...] = _stream_tile(res_refs, i, tm) + _rows_of_group(gate_ref, i * tm, tm) * acc


def _matmul(x, w, w_idx, *, n_rows, tn, out_dtype, act=None):
    tm = ROW_TILE[n_rows]
    _, k, n = w.shape
    return pl.pallas_call(
        functools.partial(_mm_kernel, act=act),
        grid=(n // tn, n_rows // tm),
        in_specs=[
            pl.BlockSpec((tm, k), lambda j, i: (i, 0)),
            pl.BlockSpec((None, k, tn), lambda j, i: (w_idx, 0, j)),
        ],
        out_specs=pl.BlockSpec((tm, tn), lambda j, i: (i, j)),
        out_shape=jax.ShapeDtypeStruct((x.shape[0], n), out_dtype),
        scratch_shapes=[pltpu.VMEM((k, tn), BF16)],
        compiler_params=_params(2),
        name="matmul",
    )(x, w)


def _matmul_resid(x, w, w_idx, h, mod, layer, which_gate, *, n_rows, tn):
    _, k, n = w.shape
    w_tile_bytes = k * tn * 4
    n_w_buffers = 1 if w_tile_bytes > 10 * 2**20 else 2
    w_buffers = pl.Buffered(1) if n_w_buffers == 1 else None

    def vmem_bytes(tm):
        return n_w_buffers * w_tile_bytes + w_tile_bytes // 2 + 2 * tm * k * 2 + 4 * tm * tn * 4

    if isinstance(h, tuple):
        tm = M_CTX
        res_arrays, aliases = h, {}
        res_specs = _split_stream_specs(tm, tn, lambda j, i: i, lambda j, i: j)
    else:
        tm = next(t for t in (ROW_TILE[n_rows], ROW_TILE_HALF[n_rows]) if vmem_bytes(t) <= RESID_VMEM_BUDGET)
        res_arrays, aliases = (h,), {3: 0}
        res_specs = [pl.BlockSpec((tm, tn), lambda j, i: (i, j))]
    return pl.pallas_call(
        _mm_resid_kernel,
        grid=(n // tn, n_rows // tm),
        in_specs=[
            pl.BlockSpec((tm, k), lambda j, i: (i, 0)),
            pl.BlockSpec((None, k, tn), lambda j, i: (w_idx, 0, j), pipeline_mode=w_buffers),
            _mod_spec(layer, which_gate, tn, grid_rank=2),
            *res_specs,
        ],
        out_specs=pl.BlockSpec((tm, tn), lambda j, i: (i, j)),
        out_shape=jax.ShapeDtypeStruct((M_ALL, n), F32),
        scratch_shapes=[pltpu.VMEM((k, tn), BF16)],
        input_output_aliases=aliases,
        compiler_params=_params(2),
        name="matmul_resid",
    )(x, w, mod, *res_arrays)


HALO = BF16_ROWS


SEQ_STARTS = tuple(b * SEQ for b in range(BATCH)) + tuple(M_LAT + b * CTX_LEN for b in range(BATCH))
FIX_ROWS = BF16_ROWS


def _is_seq_edge(row):
    edge = row == M_ALL
    for s in SEQ_STARTS:
        edge = edge | (row == s)
    return edge


def _conv3(prev, cur, nxt, cw_ref, cb_ref):
    cw = cw_ref[...]
    y = cb_ref[...] + prev * cw[0:1]
    y = y + cur * cw[1:2]
    return y + nxt * cw[2:3]


def _silu_gate(g, v):
    half_g = 0.5 * g
    return (half_g * jnp.tanh(half_g) + half_g) * v


def _ffn_up_kernel(x_ref, xp_ref, xn_ref, wg_ref, wv_ref, cwg_ref, cwv_ref, cbg_ref, cbv_ref,
                   o_ref, wg_bf, wv_bf, xs_ref, zg_ref, zv_ref):
    i = pl.program_id(1)
    tm = x_ref.shape[0]
    row0 = i * tm

    @pl.when(i == 0)
    def _():
        wg_bf[...] = wg_ref[...].astype(BF16)
        wv_bf[...] = wv_ref[...].astype(BF16)

    no_rows = jnp.zeros_like(xp_ref)
    xs_ref[0:HALO, :] = jnp.where(_is_seq_edge(row0), no_rows, xp_ref[...])
    xs_ref[HALO:HALO + tm, :] = x_ref[...]
    xs_ref[HALO + tm:, :] = jnp.where(_is_seq_edge(row0 + tm), no_rows, xn_ref[...])
    xs = xs_ref[...]
    zg_ref[...] = jnp.dot(xs, wg_bf[...], preferred_element_type=F32)
    zv_ref[...] = jnp.dot(xs, wv_bf[...], preferred_element_type=F32)

    def conv(z_ref, cw_ref, cb_ref):
        return _conv3(z_ref[HALO - 1:HALO - 1 + tm, :], z_ref[HALO:HALO + tm, :],
                      z_ref[HALO + 1:HALO + 1 + tm, :], cw_ref, cb_ref)

    o_ref[...] = _silu_gate(conv(zg_ref, cwg_ref, cbg_ref), conv(zv_ref, cwv_ref, cbv_ref)).astype(o_ref.dtype)

    n_win = 2 * FIX_ROWS + 2 * SUBLANES
    w_row = lax.broadcasted_iota(jnp.int32, (n_win, 1), 0)
    start_row = SUBLANES + FIX_ROWS

    def fixed_conv(z_ref, cw_ref, cb_ref, win):
        z = z_ref[win, :]
        prev = jnp.where(w_row == start_row, 0.0, pltpu.roll(z, 1, 0))
        nxt = jnp.where(w_row == start_row - 1, 0.0, pltpu.roll(z, n_win - 1, 0))
        return _conv3(prev, z, nxt, cw_ref, cb_ref)[SUBLANES:SUBLANES + 2 * FIX_ROWS]

    for s in SEQ_STARTS[1:]:
        @pl.when((s > row0) & (s < row0 + tm))
        def _():
            win = pl.ds(pl.multiple_of(HALO + s - row0 - start_row, SUBLANES), n_win)
            rows = pl.ds(pl.multiple_of(s - row0 - FIX_ROWS, FIX_ROWS), 2 * FIX_ROWS)
            g = fixed_conv(zg_ref, cwg_ref, cbg_ref, win)
            v = fixed_conv(zv_ref, cwv_ref, cbv_ref, win)
            o_ref[rows, :] = _silu_gate(g, v).astype(o_ref.dtype)


def _ffn_up(n, w_up, conv_w, conv_b, layer, *, n_rows):
    tm, tn = ROW_TILE[n_rows], 512
    k = w_up.shape[1]
    nj = D_FF // tn
    halo_per_tile = tm // HALO
    last_halo = M_ALL // HALO - 1
    cb = conv_b.reshape(DEPTH, 1, 2 * D_FF)
    return pl.pallas_call(
        _ffn_up_kernel,
        grid=(nj, n_rows // tm),
        in_specs=[
            pl.BlockSpec((tm, k), lambda j, i: (i, 0)),
            pl.BlockSpec((HALO, k), lambda j, i: (jnp.maximum(i * halo_per_tile - 1, 0), 0)),
            pl.BlockSpec((HALO, k), lambda j, i: (jnp.minimum((i + 1) * halo_per_tile, last_halo), 0)),
            pl.BlockSpec((None, k, tn), lambda j, i: (layer, 0, j)),
            pl.BlockSpec((None, k, tn), lambda j, i: (layer, 0, j + nj)),
            pl.BlockSpec((None, 3, tn), lambda j, i: (layer, 0, j)),
            pl.BlockSpec((None, 3, tn), lambda j, i: (layer, 0, j + nj)),
            pl.BlockSpec((None, 1, tn), lambda j, i: (layer, 0, j)),
            pl.BlockSpec((None, 1, tn), lambda j, i: (layer, 0, j + nj)),
        ],
        out_specs=pl.BlockSpec((tm, tn), lambda j, i: (i, j)),
        out_shape=jax.ShapeDtypeStruct((M_ALL, D_FF), BF16),
        scratch_shapes=[
            pltpu.VMEM((k, tn), BF16),
            pltpu.VMEM((k, tn), BF16),
            pltpu.VMEM((tm + 2 * HALO, k), BF16),
            pltpu.VMEM((tm + 2 * HALO, tn), F32),
            pltpu.VMEM((tm + 2 * HALO, tn), F32),
        ],
        compiler_params=_params(2),
        name="ffn_up_conv_gate",
    )(n, n, n, w_up, w_up, conv_w, conv_w, cb, cb)


def _sgu_kernel(u_ref, v_ref, gv_ref, ws_ref, bs_ref, o_ref):
    tm = u_ref.shape[0]
    gw = A_WIDTH // A_GROUPS
    gv = gv_ref[...]
    for c in range(tm // CHUNK):
        rows = slice(c * CHUNK, (c + 1) * CHUNK)
        v = v_ref[rows, :].astype(F32)
        vn = v * lax.rsqrt(jnp.mean(v * v, axis=-1, keepdims=True) + EPS) * gv
        vn = vn.astype(BF16)
        for g in range(A_GROUPS):
            cols = slice(g * gw, (g + 1) * gw)
            s = jnp.dot(ws_ref[g].astype(BF16), vn[:, cols], preferred_element_type=F32)
            s = s + bs_ref[:, g:g + 1]
            o_ref[rows, cols] = (u_ref[rows, cols].astype(F32) * s).astype(o_ref.dtype)


def _sgu(z, g_v, w_s, b_s, j, *, n_rows):
    tm = 4 * CHUNK
    n_a = g_v.shape[0]
    return pl.pallas_call(
        _sgu_kernel,
        grid=(n_rows // tm,),
        in_specs=[
            pl.BlockSpec((tm, A_WIDTH), lambda i: (i, 0)),
            pl.BlockSpec((tm, A_WIDTH), lambda i: (i, 1)),
            pl.BlockSpec((None, 1, A_WIDTH), lambda i: (j, 0, 0)),
            pl.BlockSpec((None, A_GROUPS, CHUNK, CHUNK), lambda i: (j, 0, 0, 0)),
            pl.BlockSpec((None, CHUNK, A_GROUPS), lambda i: (j, 0, 0)),
        ],
        out_specs=pl.BlockSpec((tm, A_WIDTH), lambda i: (i, 0)),
        out_shape=jax.ShapeDtypeStruct((M_ALL, A_WIDTH), BF16),
        compiler_params=_params(1),
        name="sgu",
    )(z, z, g_v.reshape(n_a, 1, A_WIDTH), w_s, b_s.transpose(0, 2, 1))


NA_DR = 2 * NA_KH - 1
NA_DC = 2 * NA_KW - 1
NA_PAIR_TILES = NA_DR + 1
NA_QROWS = 4
NA_KROWS = NA_KH + NA_QROWS
NA_BLOCK_UNROLL = 14


def _na_bias_pairs(rpb_ref, head, pair_ref):
    shape = (GRID_W, 2 * GRID_W)
    q = lax.broadcasted_iota(jnp.int32, shape, 0)
    lane = lax.broadcasted_iota(jnp.int32, shape, 1)
    kcol = lane & (GRID_W - 1)
    upper = lane >= GRID_W
    dc = jnp.clip(kcol - q, -(NA_KW - 1), NA_KW - 1) + NA_KW - 1
    c0 = jnp.clip(q - NA_KW // 2, 0, GRID_W - NA_KW)
    ok = (kcol >= c0) & (kcol < c0 + NA_KW)
    for d in range(-1, NA_DR):
        acc = jnp.full(shape, NEG_INF, F32)
        for c in range(NA_DC):
            lo = rpb_ref[head, d * NA_DC + c] if d >= 0 else NEG_INF
            hi = rpb_ref[head, (d + 1) * NA_DC + c] if d + 1 < NA_DR else NEG_INF
            acc = jnp.where(dc == c, jnp.where(upper, hi, lo), acc)
        pair_ref[d + 1] = jnp.where(ok, acc * LOG2_E, NEG_INF)


def _na_window_start(m):
    return min(max(m * NA_QROWS - NA_KH // 2, 0), GRID_ROWS - NA_KROWS)


def _na_block_bias(pair_ref, m):
    upper = lax.broadcasted_iota(jnp.int32, (GRID_W, 2 * GRID_W), 1) >= GRID_W
    masked = jnp.full((GRID_W, 2 * GRID_W), NEG_INF, F32)
    ws = _na_window_start(m)
    rows = []
    for a in range(NA_QROWS):
        r = m * NA_QROWS + a
        r0 = min(max(r - NA_KH // 2, 0), GRID_ROWS - NA_KH)
        tiles = []
        for p in range(NA_KROWS // 2):
            key_row = ws + 2 * p
            in_lo = r0 <= key_row < r0 + NA_KH
            in_hi = r0 <= key_row + 1 < r0 + NA_KH
            if not (in_lo or in_hi):
                tiles.append(masked)
                continue
            tile = pair_ref[key_row - r + NA_KH]
            if not in_hi:
                tile = jnp.where(upper, NEG_INF, tile)
            elif not in_lo:
                tile = jnp.where(upper, tile, NEG_INF)
            tiles.append(tile)
        rows.append(jnp.concatenate(tiles, axis=1))
    return jnp.concatenate(rows, axis=0)


def _na_lat_kernel(rpb_ref, q_ref, k_ref, v_ref, kc_ref, vc_ref, o_ref, pair_ref, bias_ref):
    scale = NA_HEAD_DIM ** -0.5 * LOG2_E

    @pl.when(pl.program_id(1) == 0)
    def _():
        _na_bias_pairs(rpb_ref, pl.program_id(0), pair_ref)
        bias_ref[...] = _na_block_bias(pair_ref, 1)

    kc = kc_ref[...]
    vc = vc_ref[...]
    nt = (((1,), (1,)), ((), ()))
    n_q = NA_QROWS * GRID_W
    n_blocks = GRID_ROWS // NA_QROWS

    def attend(qrows, win, bias):
        q = q_ref[qrows, :]
        kw = k_ref[win, :]
        vw = v_ref[win, :]
        s =lax.dot_general(q, kw, nt, preferred_element_type=F32) * scale + bias
        sc = lax.dot_general(q, kc, nt, preferred_element_type=F32) * scale
        mx = jnp.maximum(jnp.max(s, axis=-1, keepdims=True), jnp.max(sc, axis=-1, keepdims=True))
        p_w = jnp.exp2(s - mx)
        p_c = jnp.exp2(sc - mx)
        denom = jnp.sum(p_w, axis=-1, keepdims=True) + jnp.sum(p_c, axis=-1, keepdims=True)
        o = jnp.dot(p_w.astype(BF16), vw, preferred_element_type=F32)
        o = o + jnp.dot(p_c.astype(BF16), vc, preferred_element_type=F32)
        o_ref[qrows, :] = (o / denom).astype(o_ref.dtype)

    def edge_block(m):
        ws = _na_window_start(m)
        attend(slice(m * n_q, (m + 1) * n_q), slice(ws * GRID_W, (ws + NA_KROWS) * GRID_W),
               _na_block_bias(pair_ref, m))

    def interior_block(m, carry):
        qrows = pl.ds(pl.multiple_of(m * n_q, n_q), n_q)
        ws = m * NA_QROWS - NA_KH // 2
        win = pl.ds(pl.multiple_of(ws * GRID_W, GRID_W), NA_KROWS * GRID_W)
        attend(qrows, win, bias_ref[...])
        return carry

    edge_block(0)
    lax.fori_loop(1, n_blocks - 1, interior_block, 0, unroll=NA_BLOCK_UNROLL)
    edge_block(n_blocks - 1)


def _na_ctx_kernel(q_ref, k_ref, v_ref, o_ref):
    scale = NA_HEAD_DIM ** -0.5
    for h in range(NA_HEADS):
        cols = slice(h * NA_HEAD_DIM, (h + 1) * NA_HEAD_DIM)
        s = lax.dot_general(q_ref[:, cols], k_ref[:, cols], (((1,), (1,)), ((), ())),
                            preferred_element_type=F32) * scale
        m = jnp.max(s, axis=-1, keepdims=True)
        p = jnp.exp(s - m)
        denom = jnp.sum(p, axis=-1, keepdims=True)
        o = jnp.dot(p.astype(BF16), v_ref[:, cols], preferred_element_type=F32)
        o_ref[:, cols] = (o / denom).astype(o_ref.dtype)


def _na_attention(qkv, rpb):
    dh, nh = NA_HEAD_DIM, NA_HEADS
    ctx_blk0 = M_LAT // CTX_LEN
    lat = pl.pallas_call(
        _na_lat_kernel,
        grid=(nh, BATCH),
        in_specs=[
            pl.BlockSpec(memory_space=pltpu.SMEM),
            pl.BlockSpec((SEQ, dh), lambda h, b: (b, h)),
            pl.BlockSpec((SEQ, dh), lambda h, b: (b, nh + h)),
            pl.BlockSpec((SEQ, dh), lambda h, b: (b, 2 * nh + h)),
            pl.BlockSpec((CTX_LEN, dh), lambda h, b: (ctx_blk0 + b, nh + h)),
            pl.BlockSpec((CTX_LEN, dh), lambda h, b: (ctx_blk0 + b, 2 * nh + h)),
        ],
        out_specs=pl.BlockSpec((SEQ, dh), lambda h, b: (b, h)),
        out_shape=jax.ShapeDtypeStruct((M_ALL, D_MODEL), BF16),
        scratch_shapes=[
            pltpu.VMEM((NA_PAIR_TILES, GRID_W, 2 * GRID_W), F32),
            pltpu.VMEM((NA_QROWS * GRID_W, NA_KROWS * GRID_W), F32),
        ],
        compiler_params=_params(2),
        name="na_latent",
    )(rpb.reshape(nh, NA_DR * NA_DC), qkv, qkv, qkv, qkv, qkv)
    ctx = pl.pallas_call(
        _na_ctx_kernel,
        grid=(BATCH,),
        in_specs=[
            pl.BlockSpec((CTX_LEN, D_MODEL), lambda b: (ctx_blk0 + b, 0)),
            pl.BlockSpec((CTX_LEN, D_MODEL), lambda b: (ctx_blk0 + b, 1)),
            pl.BlockSpec((CTX_LEN, D_MODEL), lambda b: (ctx_blk0 + b, 2)),
        ],
        out_specs=pl.BlockSpec((CTX_LEN, D_MODEL), lambda b: (b, 0)),
        out_shape=jax.ShapeDtypeStruct((M_CTX, D_MODEL), BF16),
        compiler_params=_params(1),
        name="na_context",
    )(qkv, qkv, qkv)
    return lax.dynamic_update_slice(lat, ctx, (M_LAT, 0))


RNN_LEN = CTX_LEN + SEQ
SCAN_UNROLL = 8


def _shift_rows(x, offset, length):
    if offset == 0:
        return x
    t = lax.broadcasted_iota(jnp.int32, (length, 1), 0)
    rolled = pltpu.roll(x, (-offset) % length, 0)
    return jnp.where((t + offset >= 0) & (t + offset < length), rolled, 0.0)


def _rnn_conv(x, cw, cb):
    length = x.shape[0]
    left = RNN_CONV // 2
    y = cb
    for j in range(RNN_CONV):
        y = y + _shift_rows(x, j - left, length) * cw[j:j + 1]
    return y


def _block_scan(a, b, reverse):
    row = lax.broadcasted_iota(jnp.int32, a.shape, 0)
    for s in (1, 2, 4):
        if reverse:
            keep = row < SUBLANES - s
            shift = SUBLANES - s
        else:
            keep = row >= s
            shift = s
        a_sh = pltpu.roll(a, shift, 0)
        b_sh = pltpu.roll(b, shift, 0)
        b = jnp.where(keep, a * b_sh + b, b)
        a = jnp.where(keep, a * a_sh, a)
    return a, b


def _rglru_kernel(xc_ref, xl_ref, yc_ref, yl_ref, cw_ref, cb_ref, wg_ref, bg_ref, lam_ref,
                  oc_ref, ol_ref, af_ref, bf_ref, ar_ref, br_ref, hf_ref, hr_ref):
    cw = cw_ref[...]
    cb = cb_ref[...]
    xr_c = _rnn_conv(xc_ref[...].astype(F32), cw, cb)
    xr_l = _rnn_conv(xl_ref[...].astype(F32), cw, cb)

    def half_tanh_gate(xb, d, gate):
        w = (0.5 * wg_ref[2 * d + gate]).astype(BF16)
        return jnp.tanh(jnp.dot(xb, w, preferred_element_type=F32) + 0.5 * bg_ref[2 * d + gate])

    def gates(xr, d, a_ref, b_ref, rows):
        xb = xr.astype(BF16)
        half_xr = 0.5 * xr
        t_r = half_tanh_gate(xb, d, 0)
        t_i = half_tanh_gate(xb, d, 1)
        neg_lam = -lam_ref[d:d + 1, :]
        softplus = jnp.maximum(neg_lam, 0.0) + jnp.log1p(jnp.exp(-jnp.abs(neg_lam)))
        c = (-0.5 * RG_C * LOG2_E) * softplus
        a = jnp.exp2(c * t_r + c)
        a_ref[rows, :] = a
        y = 1.0 - a * a
        root = jnp.where(y > 0.0, y * lax.rsqrt(y), 0.0)
        b_ref[rows, :] = root * (half_xr * t_i + half_xr)

    ctx_f, lat_f = slice(0, CTX_LEN), slice(CTX_LEN, RNN_LEN)
    lat_r, ctx_r = slice(0, SEQ), slice(SEQ, RNN_LEN)
    gates(xr_c, 0, af_ref, bf_ref, ctx_f)
    gates(xr_l, 0, af_ref, bf_ref, lat_f)
    gates(xr_c, 1, ar_ref, br_ref, ctx_r)
    gates(xr_l, 1, ar_ref, br_ref, lat_r)

    nblk = RNN_LEN // SUBLANES
    last_row = slice(SUBLANES - 1, SUBLANES)
    first_row = slice(0, 1)
    blk_shape = (SUBLANES, RNN_HEAD_DIM)

    def block(n, carry):
        cf, cr = carry
        rows_f = pl.ds(pl.multiple_of(n * SUBLANES, SUBLANES), SUBLANES)
        rows_r = pl.ds(pl.multiple_of((nblk - 1 - n) * SUBLANES, SUBLANES), SUBLANES)
        a_f, b_f = _block_scan(af_ref[rows_f, :], bf_ref[rows_f, :], reverse=False)
        a_r, b_r = _block_scan(ar_ref[rows_r, :], br_ref[rows_r, :], reverse=True)
        hf_ref[rows_f, :] = a_f * cf + b_f
        hr_ref[rows_r, :] = a_r * cr + b_r
        cf = jnp.broadcast_to(a_f[last_row], blk_shape) * cf + jnp.broadcast_to(b_f[last_row], blk_shape)
        cr = jnp.broadcast_to(a_r[first_row], blk_shape) * cr + jnp.broadcast_to(b_r[first_row], blk_shape)
        return cf, cr

    zero = jnp.zeros(blk_shape, F32)
    lax.fori_loop(0, nblk, block, (zero, zero), unroll=SCAN_UNROLL)

    h_c = hf_ref[ctx_f, :] + hr_ref[ctx_r, :]
    h_l = hf_ref[lat_f, :] + hr_ref[lat_r, :]
    oc_ref[...] = (jax.nn.gelu(yc_ref[...].astype(F32)) * h_c).astype(oc_ref.dtype)
    ol_ref[...] = (jax.nn.gelu(yl_ref[...].astype(F32)) * h_l).astype(ol_ref.dtype)


def _rglru(xin, conv_w, conv_b, w_gate, b_gate, lam, j):
    dh, nh = RNN_HEAD_DIM, RNN_HEADS
    n_c = conv_w.shape[0]
    ctx_blk0 = M_LAT // CTX_LEN
    wg = w_gate.reshape(n_c, 4, nh, dh, dh)
    bg = b_gate.reshape(n_c, 4, nh, 1, dh)
    out_c, out_l = pl.pallas_call(
        _rglru_kernel,
        grid=(BATCH, nh),
        in_specs=[
            pl.BlockSpec((CTX_LEN, dh), lambda b, h: (ctx_blk0 + b, nh + h)),
            pl.BlockSpec((SEQ, dh), lambda b, h: (b, nh + h)),
            pl.BlockSpec((CTX_LEN, dh), lambda b, h: (ctx_blk0 + b, h)),
            pl.BlockSpec((SEQ, dh), lambda b, h: (b, h)),
            pl.BlockSpec((None, RNN_CONV, dh), lambda b, h: (j, 0, h)),
            pl.BlockSpec((None, 1, dh), lambda b, h: (j, 0, h)),
            pl.BlockSpec((None, 4, None, dh, dh), lambda b, h: (j, 0, h, 0, 0)),
            pl.BlockSpec((None, 4, None, 1, dh), lambda b, h: (j, 0, h, 0, 0)),
            pl.BlockSpec((None, 2, dh), lambda b, h: (j, 0, h)),
        ],
        out_specs=[
            pl.BlockSpec((CTX_LEN, dh), lambda b, h: (b, h)),
            pl.BlockSpec((SEQ, dh), lambda b, h: (b, h)),
        ],
        out_shape=[
            jax.ShapeDtypeStruct((M_CTX, RNN_WIDTH), BF16),
            jax.ShapeDtypeStruct((M_ALL, RNN_WIDTH), BF16),
        ],
        scratch_shapes=[pltpu.VMEM((RNN_LEN, dh), F32)] * 6,
        compiler_params=_params(2),
        name="rglru",
    )(xin, xin, xin, xin, conv_w, conv_b.reshape(n_c, 1, RNN_WIDTH), wg, bg, lam)
    return lax.dynamic_update_slice(out_l, out_c, (M_LAT, 0))


def kernel(x, c, ctx, c_ctx, ada_w, ada_b, norm_g, ffn_w_up, ffn_conv_w, ffn_conv_b, ffn_w_down,
           a_w_in, a_g_v, a_w_s, a_b_s, a_w_out, b_w_qkv, b_rpb, b_w_out,
           c_w_in, c_conv_w, c_conv_b, c_w_gate, c_b_gate, c_lam, c_w_out, final_g):
    h = (x.reshape(M_LAT, D_MODEL), ctx.reshape(M_CTX, D_MODEL))
    cvec = jnp.concatenate(
        [c, c_ctx[None], jnp.zeros((SUBLANES - BATCH - 1, D_MODEL), F32)], axis=0)
    mod = _ada_mod(cvec, ada_w, ada_b)

    for i in range(DEPTH):
        last = i == DEPTH - 1
        kind, j = i % N_MIXERS, i // N_MIXERS
        ffn_rows = M_LAT if last else M_ALL
        n_rows = M_LAT if (last and kind == 0) else M_ALL

        n = _norm_mod(h, norm_g, mod, i, 0, n_rows)
        if kind == 0:
            z = _matmul(n, a_w_in, j, n_rows=n_rows, tn=1024, out_dtype=BF16, act="gelu")
            y = _sgu(z, a_g_v, a_w_s, a_b_s, j, n_rows=n_rows)
            w_out = a_w_out
        elif kind == 1:
            qkv = _matmul(n, b_w_qkv, j, n_rows=n_rows, tn=1024, out_dtype=BF16)
            y = _na_attention(qkv, b_rpb[j])
            w_out = b_w_out
        else:
            xin = _matmul(n, c_w_in, j, n_rows=n_rows, tn=1024, out_dtype=F32)
            y = _rglru(xin, c_conv_w, c_conv_b, c_w_gate, c_b_gate, c_lam, j)
            w_out = c_w_out
        h = _matmul_resid(y, w_out, j, h, mod, i, 2, n_rows=n_rows, tn=1024)

        n = _norm_mod(h, norm_g, mod, i, 1, ffn_rows)
        a = _ffn_up(n, ffn_w_up, ffn_conv_w, ffn_conv_b, i, n_rows=ffn_rows)
        h = _matmul_resid(a, ffn_w_down, i, h, mod, i, 5, n_rows=ffn_rows, tn=512)

    return _final_norm(h, final_g).reshape(BATCH, SEQ, D_MODEL)
```

```python
import functools

import jax
import jax.numpy as jnp
from jax import lax
from jax.experimental import pallas as pl
from jax.experimental.pallas import tpu as pltpu

F32 = jnp.float32
BF16 = jnp.bfloat16

D_MODEL = 2048
BATCH = 2
SEQ = 4096
DEPTH = 4
GRID_W = 64
CTX_LEN = 256
N_MIXERS = 3
N_MOD = 6
EPS = 1e-6
NEG_INF = -1e30
LOG2_E = 1.4426950408889634
D_FF = 5632
CHUNK = 128
A_WIDTH = 2 * D_MODEL
A_GROUPS = 16
NA_HEADS = 16
NA_HEAD_DIM = D_MODEL // NA_HEADS
NA_KH = 8
NA_KW = 16
RNN_WIDTH = D_MODEL
RNN_HEADS = 16
RNN_HEAD_DIM = RNN_WIDTH // RNN_HEADS
RNN_CONV = 4
RG_C = 8.0

M_LAT = BATCH * SEQ
M_CTX = BATCH * CTX_LEN
M_ALL = M_LAT + M_CTX
GRID_ROWS = SEQ // GRID_W

VMEM_LIMIT_BYTES = 56 * 1024 * 1024
SUBLANES = 8
LANES = 128
BF16_ROWS = 16

ROW_TILE = {M_ALL: M_ALL // 8, M_LAT: M_LAT // 8}
ROW_TILE_WIDE_K = {M_ALL: M_ALL // 16, M_LAT: M_LAT // 16}
NORM_ROW_TILE = 512
NORM_CHUNK = BF16_ROWS


def _params(n_axes):
    return pltpu.CompilerParams(
        dimension_semantics=("arbitrary",) * n_axes, vmem_limit_bytes=VMEM_LIMIT_BYTES)


def _sigmoid(x):
    return 0.5 * jnp.tanh(0.5 * x) + 0.5


def _mod_spec(layer, which, tn=D_MODEL, grid_rank=1):
    if grid_rank == 1:
        return pl.BlockSpec((None, SUBLANES, tn), lambda i: (layer * N_MOD + which, 0, 0))
    return pl.BlockSpec((None, SUBLANES, tn), lambda j, i: (layer * N_MOD + which, 0, j))


def _rows_of_group(mod_ref, row0, tm):
    r = row0 + lax.broadcasted_iota(jnp.int32, (tm, 1), 0)
    m = mod_ref[...]
    out = m[BATCH:BATCH + 1]
    for b in reversed(range(BATCH)):
        out = jnp.where(r < (b + 1) * SEQ, m[b:b + 1], out)
    return out


def _split_stream_specs(tm, tn, row_of, col_of):
    assert M_LAT % tm == 0 and tm == M_CTX
    last_lat = M_LAT // tm - 1
    return [
        pl.BlockSpec((tm, tn), lambda *ids: (jnp.minimum(row_of(*ids), last_lat), col_of(*ids))),
        pl.BlockSpec((tm, tn), lambda *ids: (0, col_of(*ids))),
    ]


def _stream_tile(h_refs, i, tm, rows=slice(None)):
    if len(h_refs) == 1:
        return h_refs[0][rows, :]
    lat_ref, ctx_ref = h_refs
    return jnp.where(i < M_LAT // tm, lat_ref[rows, :], ctx_ref[rows, :])


def _ada_kernel(c_ref, w_ref, b_ref, o_ref):
    c = c_ref[...]
    s = (c * _sigmoid(c)).astype(BF16)
    o_ref[...] = jnp.dot(s, w_ref[...].astype(BF16), preferred_element_type=F32) + b_ref[...]


def _ada_mod(cvec, ada_w, ada_b):
    depth, d, n = ada_w.shape
    tn = 2048
    out = pl.pallas_call(
        _ada_kernel,
        grid=(depth, n // tn),
        in_specs=[
            pl.BlockSpec((SUBLANES, d), lambda l, j: (0, 0)),
            pl.BlockSpec((None, d, tn), lambda l, j: (l, 0, j)),
            pl.BlockSpec((None, 1, tn), lambda l, j: (l, 0, j)),
        ],
        out_specs=pl.BlockSpec((None, SUBLANES, tn), lambda l, j: (l, 0, j)),
        out_shape=jax.ShapeDtypeStruct((depth, SUBLANES, n), F32),
        compiler_params=_params(2),
        name="ada_mod",
    )(cvec, ada_w, ada_b.reshape(depth, 1, n))
    return out.reshape(depth, SUBLANES, N_MOD, d).transpose(0, 2, 1, 3).reshape(depth * N_MOD, SUBLANES, d)


def _norm_mod_kernel(*refs):
    *h_refs, g_ref, shift_ref, scale_ref, o_ref = refs
    i = pl.program_id(0)
    tm = o_ref.shape[0]
    group = pl.ds(jnp.minimum(i * tm // SEQ, BATCH), 1)
    gain = g_ref[...] * (1 + scale_ref[group, :])
    shift = shift_ref[group, :]

    def chunk(r, carry):
        rows = pl.ds(pl.multiple_of(r * NORM_CHUNK, NORM_CHUNK), NORM_CHUNK)
        x = _stream_tile(h_refs, i, tm, rows)
        y = x * lax.rsqrt(jnp.mean(x * x, axis=-1, keepdims=True) + EPS)
        o_ref[rows, :] = (y * gain + shift).astype(o_ref.dtype)
        return carry

    lax.fori_loop(0, tm // NORM_CHUNK, chunk, 0, unroll=True)


def _norm_mod(h, norm_g, mod, layer, which_norm, n_rows):
    tm = NORM_ROW_TILE
    g = norm_g.reshape(DEPTH * 2, 1, D_MODEL)
    h_arrays = h if isinstance(h, tuple) else (h,)
    h_specs = (_split_stream_specs(tm, D_MODEL, lambda i: i, lambda i: 0) if isinstance(h, tuple)
               else [pl.BlockSpec((tm, D_MODEL), lambda i: (i, 0))])
    return pl.pallas_call(
        _norm_mod_kernel,
        grid=(n_rows // tm,),
        in_specs=[
            *h_specs,
            pl.BlockSpec((None, 1, D_MODEL), lambda i: (layer * 2 + which_norm, 0, 0)),
            _mod_spec(layer, 3 * which_norm),
            _mod_spec(layer, 3 * which_norm + 1),
        ],
        out_specs=pl.BlockSpec((tm, D_MODEL), lambda i: (i, 0)),
        out_shape=jax.ShapeDtypeStruct((M_ALL, D_MODEL), BF16),
        compiler_params=_params(1),
        name="norm_mod",
    )(*h_arrays, g, mod, mod)


def _final_norm_kernel(h_ref, g_ref, o_ref):
    gain = g_ref[...]

    def chunk(r, carry):
        rows = pl.ds(pl.multiple_of(r * NORM_CHUNK, NORM_CHUNK), NORM_CHUNK)
        x = h_ref[rows, :]
        o_ref[rows, :] = x * lax.rsqrt(jnp.mean(x * x, axis=-1, keepdims=True) + EPS) * gain
        return carry

    lax.fori_loop(0, h_ref.shape[0] // NORM_CHUNK, chunk, 0, unroll=True)


def _final_norm(h, final_g):
    tm = NORM_ROW_TILE
    return pl.pallas_call(
        _final_norm_kernel,
        grid=(M_LAT // tm,),
        in_specs=[
            pl.BlockSpec((tm, D_MODEL), lambda i: (i, 0)),
            pl.BlockSpec((1, D_MODEL), lambda i: (0, 0)),
        ],
        out_specs=pl.BlockSpec((tm, D_MODEL), lambda i: (i, 0)),
        out_shape=jax.ShapeDtypeStruct((M_LAT, D_MODEL), F32),
        compiler_params=_params(1),
        name="final_norm",
    )(h, final_g.reshape(1, D_MODEL))


def _mm_kernel(x_ref, w_ref, o_ref, wbf_ref, *, act):
    @pl.when(pl.program_id(1) == 0)
    def _():
        wbf_ref[...] = w_ref[...].astype(BF16)

    acc = jnp.dot(x_ref[...], wbf_ref[...], preferred_element_type=F32)
    if act == "gelu":
        acc = jax.nn.gelu(acc)
    if len(o_ref.shape) == 3:
        for s in range(o_ref.shape[0]):
            o_ref[s] = acc[:, s * o_ref.shape[2]:(s + 1) * o_ref.shape[2]].astype(o_ref.dtype)
    else:
        o_ref[...] = acc.astype(o_ref.dtype)


def _mm_resid_kernel(x_ref, w_ref, gate_ref, *refs):
    *res_refs, o_ref, wbf_ref = refs
    i = pl.program_id(1)
    tm = x_ref.shape[0]

    @pl.when(i == 0)
    def _():
        wbf_ref[...] = w_ref[...].astype(BF16)

    acc = jnp.dot(x_ref[...], wbf_ref[...], preferred_element_type=F32)
    o_ref[...] = _stream_tile(res_refs, i, tm) + _rows_of_group(gate_ref, i * tm, tm) * acc


def _matmul(x, w, w_idx, *, n_rows, tn, out_dtype, act=None, slab=None):
    tm = ROW_TILE[n_rows]
    _, k, n = w.shape
    if slab is None:
        out_spec = pl.BlockSpec((tm, tn), lambda j, i: (i, j))
        out_shape = jax.ShapeDtypeStruct((x.shape[0], n), out_dtype)
    else:
        out_spec = pl.BlockSpec((tn // slab, tm, slab), lambda j, i: (j, i, 0))
        out_shape = jax.ShapeDtypeStruct((n // slab, x.shape[0], slab), out_dtype)
    return pl.pallas_call(
        functools.partial(_mm_kernel, act=act),
        grid=(n // tn, n_rows // tm),
        in_specs=[
            pl.BlockSpec((tm, k), lambda j, i: (i, 0)),
            pl.BlockSpec((None, k, tn), lambda j, i: (w_idx, 0, j)),
        ],
        out_specs=out_spec,
        out_shape=out_shape,
        scratch_shapes=[pltpu.VMEM((k, tn), BF16)],
        compiler_params=_params(2),
        name="matmul",
    )(x, w)


def _matmul_resid(x, w, w_idx, h, mod, layer, which_gate, *, n_rows, tn):
    _, k, n = w.shape
    w_tile_bytes = k * tn * 4
    w_buffers = pl.Buffered(1) if w_tile_bytes > 12 * 2**20 else None
    if isinstance(h, tuple):
        tm = M_CTX
        res_arrays, aliases = h, {}
        res_specs = _split_stream_specs(tm, tn, lambda j, i: i, lambda j, i: j)
    else:
        tm = (ROW_TILE_WIDE_K if w_tile_bytes > 8 * 2**20 else ROW_TILE)[n_rows]
        res_arrays, aliases = (h,), {3: 0}
        res_specs = [pl.BlockSpec((tm, tn), lambda j, i: (i, j))]
    return pl.pallas_call(
        _mm_resid_kernel,
        grid=(n // tn, n_rows // tm),
        in_specs=[
            pl.BlockSpec((tm, k), lambda j, i: (i, 0)),
            pl.BlockSpec((None, k, tn), lambda j, i: (w_idx, 0, j), pipeline_mode=w_buffers),
            _mod_spec(layer, which_gate, tn, grid_rank=2),
            *res_specs,
        ],
        out_specs=pl.BlockSpec((tm, tn), lambda j, i: (i, j)),
        out_shape=jax.ShapeDtypeStruct((M_ALL, n), F32),
        scratch_shapes=[pltpu.VMEM((k, tn), BF16)],
        input_output_aliases=aliases,
        compiler_params=_params(2),
        name="matmul_resid",
    )(x, w, mod, *res_arrays)


HALO = BF16_ROWS


SEQ_STARTS = tuple(b * SEQ for b in range(BATCH)) + tuple(M_LAT + b * CTX_LEN for b in range(BATCH))
FIX_ROWS = BF16_ROWS


def _is_seq_edge(row):
    edge = row == M_ALL
    for s in SEQ_STARTS:
        edge = edge | (row == s)
    return edge


def _conv3(prev, cur, nxt, cw_ref, cb_ref):
    cw = cw_ref[...]
    y = cb_ref[...] + prev * cw[0:1]
    y = y + cur * cw[1:2]
    return y + nxt * cw[2:3]


def _silu_gate(g, v):
    half_g = 0.5 * g
    return (half_g * jnp.tanh(half_g) + half_g) * v


def _ffn_up_kernel(x_ref, xp_ref, xn_ref, wg_ref, wv_ref, cwg_ref, cwv_ref, cbg_ref, cbv_ref,
                   o_ref, wg_bf, wv_bf, xs_ref, zg_ref, zv_ref):
    i = pl.program_id(1)
    tm = x_ref.shape[0]
    row0 = i * tm

    @pl.when(i == 0)
    def _():
        wg_bf[...] = wg_ref[...].astype(BF16)
        wv_bf[...] = wv_ref[...].astype(BF16)

    no_rows = jnp.zeros_like(xp_ref)
    xs_ref[0:HALO, :] = jnp.where(_is_seq_edge(row0), no_rows, xp_ref[...])
    xs_ref[HALO:HALO + tm, :] = x_ref[...]
    xs_ref[HALO + tm:, :] = jnp.where(_is_seq_edge(row0 + tm), no_rows, xn_ref[...])
    xs = xs_ref[...]
    zg_ref[...] = jnp.dot(xs, wg_bf[...], preferred_element_type=F32)
    zv_ref[...] = jnp.dot(xs, wv_bf[...], preferred_element_type=F32)

    def conv(z_ref, cw_ref, cb_ref):
        return _conv3(z_ref[HALO - 1:HALO - 1 + tm, :], z_ref[HALO:HALO + tm, :],
                      z_ref[HALO + 1:HALO + 1 + tm, :], cw_ref, cb_ref)

    o_ref[...] = _silu_gate(conv(zg_ref, cwg_ref, cbg_ref), conv(zv_ref, cwv_ref, cbv_ref)).astype(o_ref.dtype)

    n_win = 2 * FIX_ROWS + 2 * SUBLANES
    w_row = lax.broadcasted_iota(jnp.int32, (n_win, 1), 0)
    start_row = SUBLANES + FIX_ROWS

    def fixed_conv(z_ref, cw_ref, cb_ref, win):
        z = z_ref[win, :]
        prev = jnp.where(w_row == start_row, 0.0, pltpu.roll(z, 1, 0))
        nxt = jnp.where(w_row == start_row - 1, 0.0, pltpu.roll(z, n_win - 1, 0))
        return _conv3(prev, z, nxt, cw_ref, cb_ref)[SUBLANES:SUBLANES + 2 * FIX_ROWS]

    for s in SEQ_STARTS[1:]:
        @pl.when((s > row0) & (s < row0 + tm))
        def _():
            win = pl.ds(pl.multiple_of(HALO + s - row0 - start_row, SUBLANES), n_win)
            rows = pl.ds(pl.multiple_of(s - row0 - FIX_ROWS, FIX_ROWS), 2 * FIX_ROWS)
            g = fixed_conv(zg_ref, cwg_ref, cbg_ref, win)
            v = fixed_conv(zv_ref, cwv_ref, cbv_ref, win)
            o_ref[rows, :] = _silu_gate(g, v).astype(o_ref.dtype)


def _ffn_up(n, w_up, conv_w, conv_b, layer, *, n_rows):
    tm, tn = ROW_TILE[n_rows], 512
    k = w_up.shape[1]
    nj = D_FF // tn
    halo_per_tile = tm // HALO
    last_halo = M_ALL // HALO - 1
    cb = conv_b.reshape(DEPTH, 1, 2 * D_FF)
    return pl.pallas_call(
        _ffn_up_kernel,
        grid=(nj, n_rows // tm),
        in_specs=[
            pl.BlockSpec((tm, k), lambda j, i: (i, 0)),
            pl.BlockSpec((HALO, k), lambda j, i: (jnp.maximum(i * halo_per_tile - 1, 0), 0)),
            pl.BlockSpec((HALO, k), lambda j, i: (jnp.minimum((i + 1) * halo_per_tile, last_halo), 0)),
            pl.BlockSpec((None, k, tn), lambda j, i: (layer, 0, j)),
            pl.BlockSpec((None, k, tn), lambda j, i: (layer, 0, j + nj)),
            pl.BlockSpec((None, 3, tn), lambda j, i: (layer, 0, j)),
            pl.BlockSpec((None, 3, tn), lambda j, i: (layer, 0, j + nj)),
            pl.BlockSpec((None, 1, tn), lambda j, i: (layer, 0, j)),
            pl.BlockSpec((None, 1, tn), lambda j, i: (layer, 0, j + nj)),
        ],
        out_specs=pl.BlockSpec((tm, tn), lambda j, i: (i, j)),
        out_shape=jax.ShapeDtypeStruct((M_ALL, D_FF), BF16),
        scratch_shapes=[
            pltpu.VMEM((k, tn), BF16),
            pltpu.VMEM((k, tn), BF16),
            pltpu.VMEM((tm + 2 * HALO, k), BF16),
            pltpu.VMEM((tm + 2 * HALO, tn), F32),
            pltpu.VMEM((tm + 2 * HALO, tn), F32),
        ],
        compiler_params=_params(2),
        name="ffn_up_conv_gate",
    )(n, n, n, w_up, w_up, conv_w, conv_w, cb, cb)


def _sgu_kernel(u_ref, v_ref, gv_ref, ws_ref, bs_ref, o_ref):
    tm = u_ref.shape[0]
    gw = A_WIDTH // A_GROUPS
    gv = gv_ref[...]
    for c in range(tm // CHUNK):
        rows = slice(c * CHUNK, (c + 1) * CHUNK)
        v = v_ref[rows, :].astype(F32)
        vn = v * lax.rsqrt(jnp.mean(v * v, axis=-1, keepdims=True) + EPS) * gv
        vn = vn.astype(BF16)
        for g in range(A_GROUPS):
            cols = slice(g * gw, (g + 1) * gw)
            s = jnp.dot(ws_ref[g].astype(BF16), vn[:, cols], preferred_element_type=F32)
            s = s + bs_ref[:, g:g + 1]
            o_ref[rows, cols] = (u_ref[rows, cols].astype(F32) * s).astype(o_ref.dtype)


def _sgu(z, g_v, w_s, b_s, j, *, n_rows):
    tm = 4 * CHUNK
    n_a = g_v.shape[0]
    return pl.pallas_call(
        _sgu_kernel,
        grid=(n_rows // tm,),
        in_specs=[
            pl.BlockSpec((tm, A_WIDTH), lambda i: (i, 0)),
            pl.BlockSpec((tm, A_WIDTH), lambda i: (i, 1)),
            pl.BlockSpec((None, 1, A_WIDTH), lambda i: (j, 0, 0)),
            pl.BlockSpec((None, A_GROUPS, CHUNK, CHUNK), lambda i: (j, 0, 0, 0)),
            pl.BlockSpec((None, CHUNK, A_GROUPS), lambda i: (j, 0, 0)),
        ],
        out_specs=pl.BlockSpec((tm, A_WIDTH), lambda i: (i, 0)),
        out_shape=jax.ShapeDtypeStruct((M_ALL, A_WIDTH), BF16),
        compiler_params=_params(1),
        name="sgu",
    )(z, z, g_v.reshape(n_a, 1, A_WIDTH), w_s, b_s.transpose(0, 2, 1))


NA_DR = 2 * NA_KH - 1
NA_DC = 2 * NA_KW - 1
NA_PAIR_TILES = NA_DR + 1
NA_QROWS = 4
NA_KROWS = NA_KH + NA_QROWS
NA_BLOCK_UNROLL = 14


def _na_bias_pairs(rpb_ref, head, pair_ref):
    shape = (GRID_W, 2 * GRID_W)
    q = lax.broadcasted_iota(jnp.int32, shape, 0)
    lane = lax.broadcasted_iota(jnp.int32, shape, 1)
    kcol = lane & (GRID_W - 1)
    upper = lane >= GRID_W
    dc = jnp.clip(kcol - q, -(NA_KW - 1), NA_KW - 1) + NA_KW - 1
    c0 = jnp.clip(q - NA_KW // 2, 0, GRID_W - NA_KW)
    ok = (kcol >= c0) & (kcol < c0 + NA_KW)
    for d in range(-1, NA_DR):
        acc = jnp.full(shape, NEG_INF, F32)
        for c in range(NA_DC):
            lo = rpb_ref[head, d * NA_DC + c] if d >= 0 else NEG_INF
            hi = rpb_ref[head, (d + 1) * NA_DC + c] if d + 1 < NA_DR else NEG_INF
            acc = jnp.where(dc == c, jnp.where(upper, hi, lo), acc)
        pair_ref[d + 1] = jnp.where(ok, acc * LOG2_E, NEG_INF)


def _na_window_start(m):
    return min(max(m * NA_QROWS - NA_KH // 2, 0), GRID_ROWS - NA_KROWS)


def _na_block_bias(pair_ref, m):
    upper = lax.broadcasted_iota(jnp.int32, (GRID_W, 2 * GRID_W), 1) >= GRID_W
    masked = jnp.full((GRID_W, 2 * GRID_W), NEG_INF, F32)
    ws = _na_window_start(m)
    rows = []
    for a in range(NA_QROWS):
        r = m * NA_QROWS + a
        r0 = min(max(r - NA_KH // 2, 0), GRID_ROWS - NA_KH)
        tiles = []
        for p in range(NA_KROWS // 2):
            key_row = ws + 2 * p
            in_lo = r0 <= key_row < r0 + NA_KH
            in_hi = r0 <= key_row + 1 < r0 + NA_KH
            if not (in_lo or in_hi):
                tiles.append(masked)
                continue
            tile = pair_ref[key_row - r + NA_KH]
            if not in_hi:
                tile = jnp.where(upper, NEG_INF, tile)
            elif not in_lo:
                tile = jnp.where(upper, tile, NEG_INF)
            tiles.append(tile)
        rows.append(jnp.concatenate(tiles, axis=1))
    return jnp.concatenate(rows, axis=0)


def _na_lat_kernel(rpb_ref, q_ref, k_ref, v_ref, kc_ref, vc_ref, o_ref, pair_ref, bias_ref):
    scale = NA_HEAD_DIM ** -0.5 * LOG2_E

    @pl.when(pl.program_id(1) == 0)
    def _():
        _na_bias_pairs(rpb_ref, pl.program_id(0), pair_ref)
        bias_ref[...] = _na_block_bias(pair_ref, 1)

    kc = kc_ref[...]
    vc = vc_ref[...]
    nt = (((1,), (1,)), ((), ()))
    n_q = NA_QROWS * GRID_W
    n_blocks = GRID_ROWS // NA_QROWS

    def attend(qrows, win, bias):
        q = q_ref[qrows, :]
        kw = k_ref[win, :]
        vw = v_ref[win, :]
        s =lax.dot_general(q, kw, nt, preferred_element_type=F32) * scale + bias
        sc = lax.dot_general(q, kc, nt, preferred_element_type=F32) * scale
        mx = jnp.maximum(jnp.max(s, axis=-1, keepdims=True), jnp.max(sc, axis=-1, keepdims=True))
        p_w = jnp.exp2(s - mx)
        p_c = jnp.exp2(sc - mx)
        denom = jnp.sum(p_w, axis=-1, keepdims=True) + jnp.sum(p_c, axis=-1, keepdims=True)
        o = jnp.dot(p_w.astype(BF16), vw, preferred_element_type=F32)
        o = o + jnp.dot(p_c.astype(BF16), vc, preferred_element_type=F32)
        o_ref[qrows, :] = (o / denom).astype(o_ref.dtype)

    def edge_block(m):
        ws = _na_window_start(m)
        attend(slice(m * n_q, (m + 1) * n_q), slice(ws * GRID_W, (ws + NA_KROWS) * GRID_W),
               _na_block_bias(pair_ref, m))

    def interior_block(m, carry):
        qrows = pl.ds(pl.multiple_of(m * n_q, n_q), n_q)
        ws = m * NA_QROWS - NA_KH // 2
        win = pl.ds(pl.multiple_of(ws * GRID_W, GRID_W), NA_KROWS * GRID_W)
        attend(qrows, win, bias_ref[...])
        return carry

    edge_block(0)
    lax.fori_loop(1, n_blocks - 1, interior_block, 0, unroll=NA_BLOCK_UNROLL)
    edge_block(n_blocks - 1)


def _na_ctx_kernel(q_ref, k_ref, v_ref, o_ref):
    scale = NA_HEAD_DIM ** -0.5
    for h in range(NA_HEADS):
        cols = slice(h * NA_HEAD_DIM, (h + 1) * NA_HEAD_DIM)
        s = lax.dot_general(q_ref[h], k_ref[h], (((1,), (1,)), ((), ())),
                            preferred_element_type=F32) * scale
        m = jnp.max(s, axis=-1, keepdims=True)
        p = jnp.exp(s - m)
        denom = jnp.sum(p, axis=-1, keepdims=True)
        o = jnp.dot(p.astype(BF16), v_ref[h], preferred_element_type=F32)
        o_ref[:, cols] = (o / denom).astype(o_ref.dtype)


def _na_attention(qkv, rpb):
    dh, nh = NA_HEAD_DIM, NA_HEADS
    ctx_blk0 = M_LAT // CTX_LEN
    lat = pl.pallas_call(
        _na_lat_kernel,
        grid=(nh, BATCH),
        in_specs=[
            pl.BlockSpec(memory_space=pltpu.SMEM),
            pl.BlockSpec((None, SEQ, dh), lambda h, b: (h, b, 0)),
            pl.BlockSpec((None, SEQ, dh), lambda h, b: (nh + h, b, 0)),
            pl.BlockSpec((None, SEQ, dh), lambda h, b: (2 * nh + h, b, 0)),
            pl.BlockSpec((None, CTX_LEN, dh), lambda h, b: (nh + h, ctx_blk0 + b, 0)),
            pl.BlockSpec((None, CTX_LEN, dh), lambda h, b: (2 * nh + h, ctx_blk0 + b, 0)),
        ],
        out_specs=pl.BlockSpec((SEQ, dh), lambda h, b: (b, h)),
        out_shape=jax.ShapeDtypeStruct((M_ALL, D_MODEL), BF16),
        scratch_shapes=[
            pltpu.VMEM((NA_PAIR_TILES, GRID_W, 2 * GRID_W), F32),
            pltpu.VMEM((NA_QROWS * GRID_W, NA_KROWS * GRID_W), F32),
        ],
        compiler_params=_params(2),
        name="na_latent",
    )(rpb.reshape(nh, NA_DR * NA_DC), qkv, qkv, qkv, qkv, qkv)
    ctx = pl.pallas_call(
        _na_ctx_kernel,
        grid=(BATCH,),
        in_specs=[
            pl.BlockSpec((nh, CTX_LEN, dh), lambda b: (0, ctx_blk0 + b, 0)),
            pl.BlockSpec((nh, CTX_LEN, dh), lambda b: (1, ctx_blk0 + b, 0)),
            pl.BlockSpec((nh, CTX_LEN, dh), lambda b: (2, ctx_blk0 + b, 0)),
        ],
        out_specs=pl.BlockSpec((CTX_LEN, D_MODEL), lambda b: (b, 0)),
        out_shape=jax.ShapeDtypeStruct((M_CTX, D_MODEL), BF16),
        compiler_params=_params(1),
        name="na_context",
    )(qkv, qkv, qkv)
    return lax.dynamic_update_slice(lat, ctx, (M_LAT, 0))


RNN_LEN = CTX_LEN + SEQ
SCAN_UNROLL = 8


def _shift_rows(x, offset, length):
    if offset == 0:
        return x
    t = lax.broadcasted_iota(jnp.int32, (length, 1), 0)
    rolled = pltpu.roll(x, (-offset) % length, 0)
    return jnp.where((t + offset >= 0) & (t + offset < length), rolled, 0.0)


def _rnn_conv(x, cw, cb):
    length = x.shape[0]
    left = RNN_CONV // 2
    y = cb
    for j in range(RNN_CONV):
        y = y + _shift_rows(x, j - left, length) * cw[j:j + 1]
    return y


def _block_scan(a, b, reverse):
    row = lax.broadcasted_iota(jnp.int32, a.shape, 0)
    for s in (1, 2, 4):
        if reverse:
            keep = row < SUBLANES - s
            shift = SUBLANES - s
        else:
            keep = row >= s
            shift = s
        a_sh = pltpu.roll(a, shift, 0)
        b_sh = pltpu.roll(b, shift, 0)
        b = jnp.where(keep, a * b_sh + b, b)
        a = jnp.where(keep, a * a_sh, a)
    return a, b


def _rglru_kernel(xc_ref, xl_ref, yc_ref, yl_ref, cw_ref, cb_ref, wg_ref, bg_ref, lam_ref,
                  oc_ref, ol_ref, af_ref, bf_ref, ar_ref, br_ref, hf_ref, hr_ref):
    cw = cw_ref[...]
    cb = cb_ref[...]
    xr_c = _rnn_conv(xc_ref[...].astype(F32), cw, cb)
    xr_l = _rnn_conv(xl_ref[...].astype(F32), cw, cb)

    def half_tanh_gate(xb, d, gate):
        w = (0.5 * wg_ref[2 * d + gate]).astype(BF16)
        return jnp.tanh(jnp.dot(xb, w, preferred_element_type=F32) + 0.5 * bg_ref[2 * d + gate])

    def gates(xr, d, a_ref, b_ref, rows):
        xb = xr.astype(BF16)
        half_xr = 0.5 * xr
        t_r = half_tanh_gate(xb, d, 0)
        t_i = half_tanh_gate(xb, d, 1)
        neg_lam = -lam_ref[d:d + 1, :]
        softplus = jnp.maximum(neg_lam, 0.0) + jnp.log1p(jnp.exp(-jnp.abs(neg_lam)))
        c = (-0.5 * RG_C * LOG2_E) * softplus
        a = jnp.exp2(c * t_r + c)
        a_ref[rows, :] = a
        y = 1.0 - a * a
        root = jnp.where(y > 0.0, y * lax.rsqrt(y), 0.0)
        b_ref[rows, :] = root * (half_xr * t_i + half_xr)

    ctx_f, lat_f = slice(0, CTX_LEN), slice(CTX_LEN, RNN_LEN)
    lat_r, ctx_r = slice(0, SEQ), slice(SEQ, RNN_LEN)
    gates(xr_c, 0, af_ref, bf_ref, ctx_f)
    gates(xr_l, 0, af_ref, bf_ref, lat_f)
    gates(xr_c, 1, ar_ref, br_ref, ctx_r)
    gates(xr_l, 1, ar_ref, br_ref, lat_r)

    nblk = RNN_LEN // SUBLANES
    last_row = slice(SUBLANES - 1, SUBLANES)
    first_row = slice(0, 1)
    blk_shape = (SUBLANES, RNN_HEAD_DIM)

    def block(n, carry):
        cf, cr = carry
        rows_f = pl.ds(pl.multiple_of(n * SUBLANES, SUBLANES), SUBLANES)
        rows_r = pl.ds(pl.multiple_of((nblk - 1 - n) * SUBLANES, SUBLANES), SUBLANES)
        a_f, b_f = _block_scan(af_ref[rows_f, :], bf_ref[rows_f, :], reverse=False)
        a_r, b_r = _block_scan(ar_ref[rows_r, :], br_ref[rows_r, :], reverse=True)
        hf_ref[rows_f, :] = a_f * cf + b_f
        hr_ref[rows_r, :] = a_r * cr + b_r
        cf = jnp.broadcast_to(a_f[last_row], blk_shape) * cf + jnp.broadcast_to(b_f[last_row], blk_shape)
        cr = jnp.broadcast_to(a_r[first_row], blk_shape) * cr + jnp.broadcast_to(b_r[first_row], blk_shape)
        return cf, cr

    zero = jnp.zeros(blk_shape, F32)
    lax.fori_loop(0, nblk, block, (zero, zero), unroll=SCAN_UNROLL)

    h_c = hf_ref[ctx_f, :] + hr_ref[ctx_r, :]
    h_l = hf_ref[lat_f, :] + hr_ref[lat_r, :]
    oc_ref[...] = (jax.nn.gelu(yc_ref[...].astype(F32)) * h_c).astype(oc_ref.dtype)
    ol_ref[...] = (jax.nn.gelu(yl_ref[...].astype(F32)) * h_l).astype(ol_ref.dtype)


def _rglru(xin, conv_w, conv_b, w_gate, b_gate, lam, j):
    dh, nh = RNN_HEAD_DIM, RNN_HEADS
    n_c = conv_w.shape[0]
    ctx_blk0 = M_LAT // CTX_LEN
    wg = w_gate.reshape(n_c, 4, nh, dh, dh)
    bg = b_gate.reshape(n_c, 4, nh, 1, dh)
    out_c, out_l = pl.pallas_call(
        _rglru_kernel,
        grid=(BATCH, nh),
        in_specs=[
            pl.BlockSpec((None, CTX_LEN, dh), lambda b, h: (nh + h, ctx_blk0 + b, 0)),
            pl.BlockSpec((None, SEQ, dh), lambda b, h: (nh + h, b, 0)),
            pl.BlockSpec((None, CTX_LEN, dh), lambda b, h: (h, ctx_blk0 + b, 0)),
            pl.BlockSpec((None, SEQ, dh), lambda b, h: (h, b, 0)),
            pl.BlockSpec((None, RNN_CONV, dh), lambda b, h: (j, 0, h)),
            pl.BlockSpec((None, 1, dh), lambda b, h: (j, 0, h)),
            pl.BlockSpec((None, 4, None, dh, dh), lambda b, h: (j, 0, h, 0, 0)),
            pl.BlockSpec((None, 4, None, 1, dh), lambda b, h: (j, 0, h, 0, 0)),
            pl.BlockSpec((None, 2, dh), lambda b, h: (j, 0, h)),
        ],
        out_specs=[
            pl.BlockSpec((CTX_LEN, dh), lambda b, h: (b, h)),
            pl.BlockSpec((SEQ, dh), lambda b, h: (b, h)),
        ],
        out_shape=[
            jax.ShapeDtypeStruct((M_CTX, RNN_WIDTH), BF16),
            jax.ShapeDtypeStruct((M_ALL, RNN_WIDTH), BF16),
        ],
        scratch_shapes=[pltpu.VMEM((RNN_LEN, dh), F32)] * 6,
        compiler_params=_params(2),
        name="rglru",
    )(xin, xin, xin, xin, conv_w, conv_b.reshape(n_c, 1, RNN_WIDTH), wg, bg, lam)
    return lax.dynamic_update_slice(out_l, out_c, (M_LAT, 0))


def kernel(x, c, ctx, c_ctx, ada_w, ada_b, norm_g, ffn_w_up, ffn_conv_w, ffn_conv_b, ffn_w_down,
           a_w_in, a_g_v, a_w_s, a_b_s, a_w_out, b_w_qkv, b_rpb, b_w_out,
           c_w_in, c_conv_w, c_conv_b, c_w_gate, c_b_gate, c_lam, c_w_out, final_g):
    h = (x.reshape(M_LAT, D_MODEL), ctx.reshape(M_CTX, D_MODEL))
    cvec = jnp.concatenate(
        [c, c_ctx[None], jnp.zeros((SUBLANES - BATCH - 1, D_MODEL), F32)], axis=0)
    mod = _ada_mod(cvec, ada_w, ada_b)

    for i in range(DEPTH):
        last = i == DEPTH - 1
        kind, j = i % N_MIXERS, i // N_MIXERS
        ffn_rows = M_LAT if last else M_ALL
        n_rows = M_LAT if (last and kind == 0) else M_ALL

        n = _norm_mod(h, norm_g, mod, i, 0, n_rows)
        if kind == 0:
            z = _matmul(n, a_w_in, j, n_rows=n_rows, tn=1024, out_dtype=BF16, act="gelu")
            y = _sgu(z, a_g_v, a_w_s, a_b_s, j, n_rows=n_rows)
            w_out = a_w_out
        elif kind == 1:
            qkv = _matmul(n, b_w_qkv, j, n_rows=n_rows, tn=1024, out_dtype=BF16, slab=NA_HEAD_DIM)
            y = _na_attention(qkv, b_rpb[j])
            w_out = b_w_out
        else:
            xin = _matmul(n, c_w_in, j, n_rows=n_rows, tn=1024, out_dtype=F32, slab=RNN_HEAD_DIM)
            y = _rglru(xin, c_conv_w, c_conv_b, c_w_gate, c_b_gate, c_lam, j)
            w_out = c_w_out
        h = _matmul_resid(y, w_out, j, h, mod, i, 2, n_rows=n_rows, tn=1024)

        n = _norm_mod(h, norm_g, mod, i, 1, ffn_rows)
        a = _ffn_up(n, ffn_w_up, ffn_conv_w, ffn_conv_b, i, n_rows=ffn_rows)
        h = _matmul_resid(a, ffn_w_down, i, h, mod, i, 5, n_rows=ffn_rows, tn=512)

    return _final_norm(h, final_g).reshape(BATCH, SEQ, D_MODEL)
```

```python
import functools

import jax
import jax.numpy as jnp
from jax import lax
from jax.experimental import pallas as pl
from jax.experimental.pallas import tpu as pltpu

F32 = jnp.float32
BF16 = jnp.bfloat16

D_MODEL = 2048
BATCH = 2
SEQ = 4096
DEPTH = 4
GRID_W = 64
CTX_LEN = 256
N_MIXERS = 3
N_MOD = 6
EPS = 1e-6
NEG_INF = -1e30
LOG2_E = 1.4426950408889634
D_FF = 5632
CHUNK = 128
A_WIDTH = 2 * D_MODEL
A_GROUPS = 16
NA_HEADS = 16
NA_HEAD_DIM = D_MODEL // NA_HEADS
NA_KH = 8
NA_KW = 16
RNN_WIDTH = D_MODEL
RNN_HEADS = 16
RNN_HEAD_DIM = RNN_WIDTH // RNN_HEADS
RNN_CONV = 4
RG_C = 8.0

M_LAT = BATCH * SEQ
M_CTX = BATCH * CTX_LEN
M_ALL = M_LAT + M_CTX
GRID_ROWS = SEQ // GRID_W

VMEM_LIMIT_BYTES = 56 * 1024 * 1024
SUBLANES = 8
LANES = 128
BF16_ROWS = 16

ROW_TILE = {M_ALL: M_ALL // 8, M_LAT: M_LAT // 8}
ROW_TILE_WIDE_K = {M_ALL: M_ALL // 16, M_LAT: M_LAT // 16}
NORM_ROW_TILE = 512
NORM_CHUNK = BF16_ROWS


def _params(n_axes):
    return pltpu.CompilerParams(
        dimension_semantics=("arbitrary",) * n_axes, vmem_limit_bytes=VMEM_LIMIT_BYTES)


def _sigmoid(x):
    return 0.5 * jnp.tanh(0.5 * x) + 0.5


def _mod_spec(layer, which, tn=D_MODEL, grid_rank=1):
    if grid_rank == 1:
        return pl.BlockSpec((None, SUBLANES, tn), lambda i: (layer * N_MOD + which, 0, 0))
    return pl.BlockSpec((None, SUBLANES, tn), lambda j, i: (layer * N_MOD + which, 0, j))


def _rows_of_group(mod_ref, row0, tm):
    r = row0 + lax.broadcasted_iota(jnp.int32, (tm, 1), 0)
    m = mod_ref[...]
    out = m[BATCH:BATCH + 1]
    for b in reversed(range(BATCH)):
        out = jnp.where(r < (b + 1) * SEQ, m[b:b + 1], out)
    return out


def _split_stream_specs(tm, tn, row_of, col_of):
    assert M_LAT % tm == 0 and tm == M_CTX
    last_lat = M_LAT // tm - 1
    return [
        pl.BlockSpec((tm, tn), lambda *ids: (jnp.minimum(row_of(*ids), last_lat), col_of(*ids))),
        pl.BlockSpec((tm, tn), lambda *ids: (0, col_of(*ids))),
    ]


def _stream_tile(h_refs, i, tm, rows=slice(None)):
    if len(h_refs) == 1:
        return h_refs[0][rows, :]
    lat_ref, ctx_ref = h_refs
    return jnp.where(i < M_LAT // tm, lat_ref[rows, :], ctx_ref[rows, :])


def _ada_kernel(c_ref, w_ref, b_ref, o_ref):
    c = c_ref[...]
    s = (c * _sigmoid(c)).astype(BF16)
    o_ref[...] = jnp.dot(s, w_ref[...].astype(BF16), preferred_element_type=F32) + b_ref[...]


def _ada_mod(cvec, ada_w, ada_b):
    depth, d, n = ada_w.shape
    tn = 2048
    out = pl.pallas_call(
        _ada_kernel,
        grid=(depth, n // tn),
        in_specs=[
            pl.BlockSpec((SUBLANES, d), lambda l, j: (0, 0)),
            pl.BlockSpec((None, d, tn), lambda l, j: (l, 0, j)),
            pl.BlockSpec((None, 1, tn), lambda l, j: (l, 0, j)),
        ],
        out_specs=pl.BlockSpec((None, SUBLANES, tn), lambda l, j: (l, 0, j)),
        out_shape=jax.ShapeDtypeStruct((depth, SUBLANES, n), F32),
        compiler_params=_params(2),
        name="ada_mod",
    )(cvec, ada_w, ada_b.reshape(depth, 1, n))
    return out.reshape(depth, SUBLANES, N_MOD, d).transpose(0, 2, 1, 3).reshape(depth * N_MOD, SUBLANES, d)


def _norm_mod_kernel(*refs):
    *h_refs, g_ref, shift_ref, scale_ref, o_ref = refs
    i = pl.program_id(0)
    tm = o_ref.shape[0]
    group = pl.ds(jnp.minimum(i * tm // SEQ, BATCH), 1)
    gain = g_ref[...] * (1 + scale_ref[group, :])
    shift = shift_ref[group, :]

    def chunk(r, carry):
        rows = pl.ds(pl.multiple_of(r * NORM_CHUNK, NORM_CHUNK), NORM_CHUNK)
        x = _stream_tile(h_refs, i, tm, rows)
        y = x * lax.rsqrt(jnp.mean(x * x, axis=-1, keepdims=True) + EPS)
        o_ref[rows, :] = (y * gain + shift).astype(o_ref.dtype)
        return carry

    lax.fori_loop(0, tm // NORM_CHUNK, chunk, 0, unroll=True)


def _norm_mod(h, norm_g, mod, layer, which_norm, n_rows):
    tm = NORM_ROW_TILE
    g = norm_g.reshape(DEPTH * 2, 1, D_MODEL)
    h_arrays = h if isinstance(h, tuple) else (h,)
    h_specs = (_split_stream_specs(tm, D_MODEL, lambda i: i, lambda i: 0) if isinstance(h, tuple)
               else [pl.BlockSpec((tm, D_MODEL), lambda i: (i, 0))])
    return pl.pallas_call(
        _norm_mod_kernel,
        grid=(n_rows // tm,),
        in_specs=[
            *h_specs,
            pl.BlockSpec((None, 1, D_MODEL), lambda i: (layer * 2 + which_norm, 0, 0)),
            _mod_spec(layer, 3 * which_norm),
            _mod_spec(layer, 3 * which_norm + 1),
        ],
        out_specs=pl.BlockSpec((tm, D_MODEL), lambda i: (i, 0)),
        out_shape=jax.ShapeDtypeStruct((M_ALL, D_MODEL), BF16),
        compiler_params=_params(1),
        name="norm_mod",
    )(*h_arrays, g, mod, mod)


def _final_norm_kernel(h_ref, g_ref, o_ref):
    gain = g_ref[...]

    def chunk(r, carry):
        rows = pl.ds(pl.multiple_of(r * NORM_CHUNK, NORM_CHUNK), NORM_CHUNK)
        x = h_ref[rows, :]
        o_ref[rows, :] = x * lax.rsqrt(jnp.mean(x * x, axis=-1, keepdims=True) + EPS) * gain
        return carry

    lax.fori_loop(0, h_ref.shape[0] // NORM_CHUNK, chunk, 0, unroll=True)


def _final_norm(h, final_g):
    tm = NORM_ROW_TILE
    return pl.pallas_call(
        _final_norm_kernel,
        grid=(M_LAT // tm,),
        in_specs=[
            pl.BlockSpec((tm, D_MODEL), lambda i: (i, 0)),
            pl.BlockSpec((1, D_MODEL), lambda i: (0, 0)),
        ],
        out_specs=pl.BlockSpec((tm, D_MODEL), lambda i: (i, 0)),
        out_shape=jax.ShapeDtypeStruct((M_LAT, D_MODEL), F32),
        compiler_params=_params(1),
        name="final_norm",
    )(h, final_g.reshape(1, D_MODEL))


def _mm_kernel(x_ref, w_ref, o_ref, wbf_ref, *, act):
    @pl.when(pl.program_id(1) == 0)
    def _():
        wbf_ref[...] = w_ref[...].astype(BF16)

    acc = jnp.dot(x_ref[...], wbf_ref[...], preferred_element_type=F32)
    if act == "gelu":
        acc = jax.nn.gelu(acc)
    o_ref[...] = acc.astype(o_ref.dtype)


def _mm_resid_kernel(x_ref, w_ref, gate_ref, *refs):
    *res_refs, o_ref, wbf_ref = refs
    i = pl.program_id(1)
    tm = x_ref.shape[0]

    @pl.when(i == 0)
    def _():
        wbf_ref[...] = w_ref[...].astype(BF16)

    acc = jnp.dot(x_ref[...], wbf_ref[...], preferred_element_type=F32)
    o_ref[...] = _stream_tile(res_refs, i, tm) + _rows_of_group(gate_ref, i * tm, tm) * acc


def _matmul(x, w, w_idx, *, n_rows, tn, out_dtype, act=None):
    tm = ROW_TILE[n_rows]
    _, k, n = w.shape
    return pl.pallas_call(
        functools.partial(_mm_kernel, act=act),
        grid=(n // tn, n_rows // tm),
        in_specs=[
            pl.BlockSpec((tm, k), lambda j, i: (i, 0)),
            pl.BlockSpec((None, k, tn), lambda j, i: (w_idx, 0, j)),
        ],
        out_specs=pl.BlockSpec((tm, tn), lambda j, i: (i, j)),
        out_shape=jax.ShapeDtypeStruct((x.shape[0], n), out_dtype),
        scratch_shapes=[pltpu.VMEM((k, tn), BF16)],
        compiler_params=_params(2),
        name="matmul",
    )(x, w)


def _matmul_resid(x, w, w_idx, h, mod, layer, which_gate, *, n_rows, tn):
    _, k, n = w.shape
    w_tile_bytes = k * tn * 4
    w_buffers = pl.Buffered(1) if w_tile_bytes > 12 * 2**20 else None
    if isinstance(h, tuple):
        tm = M_CTX
        res_arrays, aliases = h, {}
        res_specs = _split_stream_specs(tm, tn, lambda j, i: i, lambda j, i: j)
    else:
        tm = (ROW_TILE_WIDE_K if w_tile_bytes > 8 * 2**20 else ROW_TILE)[n_rows]
        res_arrays, aliases = (h,), {3: 0}
        res_specs = [pl.BlockSpec((tm, tn), lambda j, i: (i, j))]
    return pl.pallas_call(
        _mm_resid_kernel,
        grid=(n // tn, n_rows // tm),
        in_specs=[
            pl.BlockSpec((tm, k), lambda j, i: (i, 0)),
            pl.BlockSpec((None, k, tn), lambda j, i: (w_idx, 0, j), pipeline_mode=w_buffers),
            _mod_spec(layer, which_gate, tn, grid_rank=2),
            *res_specs,
        ],
        out_specs=pl.BlockSpec((tm, tn), lambda j, i: (i, j)),
        out_shape=jax.ShapeDtypeStruct((M_ALL, n), F32),
        scratch_shapes=[pltpu.VMEM((k, tn), BF16)],
        input_output_aliases=aliases,
        compiler_params=_params(2),
        name="matmul_resid",
    )(x, w, mod, *res_arrays)


HALO = BF16_ROWS


SEQ_STARTS = tuple(b * SEQ for b in range(BATCH)) + tuple(M_LAT + b * CTX_LEN for b in range(BATCH))
FIX_ROWS = BF16_ROWS


def _is_seq_edge(row):
    edge = row == M_ALL
    for s in SEQ_STARTS:
        edge = edge | (row == s)
    return edge


def _conv3(prev, cur, nxt, cw_ref, cb_ref):
    reps = (prev.shape[0] // SUBLANES, 1)
    tap = [jnp.tile(cw_ref[j * SUBLANES:(j + 1) * SUBLANES, :], reps) for j in range(3)]
    y = jnp.tile(cb_ref[...], reps) + prev * tap[0]
    y = y + cur * tap[1]
    return y + nxt * tap[2]


def _silu_gate(g, v):
    half_g = 0.5 * g
    return (half_g * jnp.tanh(half_g) + half_g) * v


def _ffn_up_kernel(x_ref, xp_ref, xn_ref, wg_ref, wv_ref, cwg_ref, cwv_ref, cbg_ref, cbv_ref,
                   o_ref, wg_bf, wv_bf, xs_ref, zg_ref, zv_ref):
    i = pl.program_id(1)
    tm = x_ref.shape[0]
    row0 = i * tm

    @pl.when(i == 0)
    def _():
        wg_bf[...] = wg_ref[...].astype(BF16)
        wv_bf[...] = wv_ref[...].astype(BF16)

    no_rows = jnp.zeros_like(xp_ref)
    xs_ref[0:HALO, :] = jnp.where(_is_seq_edge(row0), no_rows, xp_ref[...])
    xs_ref[HALO:HALO + tm, :] = x_ref[...]
    xs_ref[HALO + tm:, :] = jnp.where(_is_seq_edge(row0 + tm), no_rows, xn_ref[...])
    xs = xs_ref[...]
    zg_ref[...] = jnp.dot(xs, wg_bf[...], preferred_element_type=F32)
    zv_ref[...] = jnp.dot(xs, wv_bf[...], preferred_element_type=F32)

    def conv(z_ref, cw_ref, cb_ref):
        return _conv3(z_ref[HALO - 1:HALO - 1 + tm, :], z_ref[HALO:HALO + tm, :],
                      z_ref[HALO + 1:HALO + 1 + tm, :], cw_ref, cb_ref)

    o_ref[...] = _silu_gate(conv(zg_ref, cwg_ref, cbg_ref), conv(zv_ref, cwv_ref, cbv_ref)).astype(o_ref.dtype)

    n_win = 2 * FIX_ROWS + 2 * SUBLANES
    w_row = lax.broadcasted_iota(jnp.int32, (n_win, 1), 0)
    start_row = SUBLANES + FIX_ROWS

    def fixed_conv(z_ref, cw_ref, cb_ref, win):
        z = z_ref[win, :]
        prev = jnp.where(w_row == start_row, 0.0, pltpu.roll(z, 1, 0))
        nxt = jnp.where(w_row == start_row - 1, 0.0, pltpu.roll(z, n_win - 1, 0))
        return _conv3(prev, z, nxt, cw_ref, cb_ref)[SUBLANES:SUBLANES + 2 * FIX_ROWS]

    for s in SEQ_STARTS[1:]:
        @pl.when((s > row0) & (s < row0 + tm))
        def _():
            win = pl.ds(pl.multiple_of(HALO + s - row0 - start_row, SUBLANES), n_win)
            rows = pl.ds(pl.multiple_of(s - row0 - FIX_ROWS, FIX_ROWS), 2 * FIX_ROWS)
            g = fixed_conv(zg_ref, cwg_ref, cbg_ref, win)
            v = fixed_conv(zv_ref, cwv_ref, cbv_ref, win)
            o_ref[rows, :] = _silu_gate(g, v).astype(o_ref.dtype)


def _ffn_up(n, w_up, conv_w, conv_b, layer, *, n_rows):
    tm, tn = ROW_TILE[n_rows], 512
    k = w_up.shape[1]
    nj = D_FF // tn
    halo_per_tile = tm // HALO
    last_halo = M_ALL // HALO - 1
    cw = jnp.repeat(conv_w, SUBLANES, axis=1)
    cb = jnp.broadcast_to(conv_b[:, None, :], (DEPTH, SUBLANES, 2 * D_FF))
    return pl.pallas_call(
        _ffn_up_kernel,
        grid=(nj, n_rows // tm),
        in_specs=[
            pl.BlockSpec((tm, k), lambda j, i: (i, 0)),
            pl.BlockSpec((HALO, k), lambda j, i: (jnp.maximum(i * halo_per_tile - 1, 0), 0)),
            pl.BlockSpec((HALO, k), lambda j, i: (jnp.minimum((i + 1) * halo_per_tile, last_halo), 0)),
            pl.BlockSpec((None, k, tn), lambda j, i: (layer, 0, j)),
            pl.BlockSpec((None, k, tn), lambda j, i: (layer, 0, j + nj)),
            pl.BlockSpec((None, 3 * SUBLANES, tn), lambda j, i: (layer, 0, j)),
            pl.BlockSpec((None, 3 * SUBLANES, tn), lambda j, i: (layer, 0, j + nj)),
            pl.BlockSpec((None, SUBLANES, tn), lambda j, i: (layer, 0, j)),
            pl.BlockSpec((None, SUBLANES, tn), lambda j, i: (layer, 0, j + nj)),
        ],
        out_specs=pl.BlockSpec((tm, tn), lambda j, i: (i, j)),
        out_shape=jax.ShapeDtypeStruct((M_ALL, D_FF), BF16),
        scratch_shapes=[
            pltpu.VMEM((k, tn), BF16),
            pltpu.VMEM((k, tn), BF16),
            pltpu.VMEM((tm + 2 * HALO, k), BF16),
            pltpu.VMEM((tm + 2 * HALO, tn), F32),
            pltpu.VMEM((tm + 2 * HALO, tn), F32),
        ],
        compiler_params=_params(2),
        name="ffn_up_conv_gate",
    )(n, n, n, w_up, w_up, cw, cw, cb, cb)


def _sgu_kernel(u_ref, v_ref, gv_ref, ws_ref, bs_ref, o_ref):
    tm = u_ref.shape[0]
    gw = A_WIDTH // A_GROUPS
    gv = gv_ref[...]
    for c in range(tm // CHUNK):
        rows = slice(c * CHUNK, (c + 1) * CHUNK)
        v = v_ref[rows, :].astype(F32)
        vn = v * lax.rsqrt(jnp.mean(v * v, axis=-1, keepdims=True) + EPS) * gv
        vn = vn.astype(BF16)
        for g in range(A_GROUPS):
            cols = slice(g * gw, (g + 1) * gw)
            s = jnp.dot(ws_ref[g].astype(BF16), vn[:, cols], preferred_element_type=F32)
            s = s + bs_ref[:, g:g + 1]
            o_ref[rows, cols] = (u_ref[rows, cols].astype(F32) * s).astype(o_ref.dtype)


def _sgu(z, g_v, w_s, b_s, j, *, n_rows):
    tm = 4 * CHUNK
    n_a = g_v.shape[0]
    return pl.pallas_call(
        _sgu_kernel,
        grid=(n_rows // tm,),
        in_specs=[
            pl.BlockSpec((tm, A_WIDTH), lambda i: (i, 0)),
            pl.BlockSpec((tm, A_WIDTH), lambda i: (i, 1)),
            pl.BlockSpec((None, 1, A_WIDTH), lambda i: (j, 0, 0)),
            pl.BlockSpec((None, A_GROUPS, CHUNK, CHUNK), lambda i: (j, 0, 0, 0)),
            pl.BlockSpec((None, CHUNK, A_GROUPS), lambda i: (j, 0, 0)),
        ],
        out_specs=pl.BlockSpec((tm, A_WIDTH), lambda i: (i, 0)),
        out_shape=jax.ShapeDtypeStruct((M_ALL, A_WIDTH), BF16),
        compiler_params=_params(1),
        name="sgu",
    )(z, z, g_v.reshape(n_a, 1, A_WIDTH), w_s, b_s.transpose(0, 2, 1))


NA_DR = 2 * NA_KH - 1
NA_DC = 2 * NA_KW - 1
NA_PAIR_TILES = NA_DR + 1
NA_QROWS = 4
NA_KROWS = NA_KH + NA_QROWS
NA_BLOCK_UNROLL = 14


def _na_bias_pairs(rpb_ref, head, pair_ref):
    shape = (GRID_W, 2 * GRID_W)
    q = lax.broadcasted_iota(jnp.int32, shape, 0)
    lane = lax.broadcasted_iota(jnp.int32, shape, 1)
    kcol = lane & (GRID_W - 1)
    upper = lane >= GRID_W
    dc = jnp.clip(kcol - q, -(NA_KW - 1), NA_KW - 1) + NA_KW - 1
    c0 = jnp.clip(q - NA_KW // 2, 0, GRID_W - NA_KW)
    ok = (kcol >= c0) & (kcol < c0 + NA_KW)
    for d in range(-1, NA_DR):
        acc = jnp.full(shape, NEG_INF, F32)
        for c in range(NA_DC):
            lo = rpb_ref[head, d * NA_DC + c] if d >= 0 else NEG_INF
            hi = rpb_ref[head, (d + 1) * NA_DC + c] if d + 1 < NA_DR else NEG_INF
            acc = jnp.where(dc == c, jnp.where(upper, hi, lo), acc)
        pair_ref[d + 1] = jnp.where(ok, acc * LOG2_E, NEG_INF)


def _na_window_start(m):
    return min(max(m * NA_QROWS - NA_KH // 2, 0), GRID_ROWS - NA_KROWS)


def _na_block_bias(pair_ref, m):
    upper = lax.broadcasted_iota(jnp.int32, (GRID_W, 2 * GRID_W), 1) >= GRID_W
    masked = jnp.full((GRID_W, 2 * GRID_W), NEG_INF, F32)
    ws = _na_window_start(m)
    rows = []
    for a in range(NA_QROWS):
        r = m * NA_QROWS + a
        r0 = min(max(r - NA_KH // 2, 0), GRID_ROWS - NA_KH)
        tiles = []
        for p in range(NA_KROWS // 2):
            key_row = ws + 2 * p
            in_lo = r0 <= key_row < r0 + NA_KH
            in_hi = r0 <= key_row + 1 < r0 + NA_KH
            if not (in_lo or in_hi):
                tiles.append(masked)
                continue
            tile = pair_ref[key_row - r + NA_KH]
            if not in_hi:
                tile = jnp.where(upper, NEG_INF, tile)
            elif not in_lo:
                tile = jnp.where(upper, tile, NEG_INF)
            tiles.append(tile)
        rows.append(jnp.concatenate(tiles, axis=1))
    return jnp.concatenate(rows, axis=0)


def _na_lat_kernel(rpb_ref, q_ref, k_ref, v_ref, kc_ref, vc_ref, o_ref, pair_ref, bias_ref):
    scale = NA_HEAD_DIM ** -0.5 * LOG2_E

    @pl.when(pl.program_id(1) == 0)
    def _():
        _na_bias_pairs(rpb_ref, pl.program_id(0), pair_ref)
        bias_ref[...] = _na_block_bias(pair_ref, 1)

    kc = kc_ref[...]
    vc = vc_ref[...]
    nt = (((1,), (1,)), ((), ()))
    n_q = NA_QROWS * GRID_W
    n_blocks = GRID_ROWS // NA_QROWS

    def attend(qrows, win, bias):
        q = q_ref[qrows, :]
        kw = k_ref[win, :]
        vw = v_ref[win, :]
        s =lax.dot_general(q, kw, nt, preferred_element_type=F32) * scale + bias
        sc = lax.dot_general(q, kc, nt, preferred_element_type=F32) * scale
        mx = jnp.maximum(jnp.max(s, axis=-1, keepdims=True), jnp.max(sc, axis=-1, keepdims=True))
        p_w = jnp.exp2(s - mx)
        p_c = jnp.exp2(sc - mx)
        denom = jnp.sum(p_w, axis=-1, keepdims=True) + jnp.sum(p_c, axis=-1, keepdims=True)
        o = jnp.dot(p_w.astype(BF16), vw, preferred_element_type=F32)
        o = o + jnp.dot(p_c.astype(BF16), vc, preferred_element_type=F32)
        o_ref[qrows, :] = (o / denom).astype(o_ref.dtype)

    def edge_block(m):
        ws = _na_window_start(m)
        attend(slice(m * n_q, (m + 1) * n_q), slice(ws * GRID_W, (ws + NA_KROWS) * GRID_W),
               _na_block_bias(pair_ref, m))

    def interior_block(m, carry):
        qrows = pl.ds(pl.multiple_of(m * n_q, n_q), n_q)
        ws = m * NA_QROWS - NA_KH // 2
        win = pl.ds(pl.multiple_of(ws * GRID_W, GRID_W), NA_KROWS * GRID_W)
        attend(qrows, win, bias_ref[...])
        return carry

    edge_block(0)
    lax.fori_loop(1, n_blocks - 1, interior_block, 0, unroll=NA_BLOCK_UNROLL)
    edge_block(n_blocks - 1)


def _na_ctx_kernel(q_ref, k_ref, v_ref, o_ref):
    scale = NA_HEAD_DIM ** -0.5
    for h in range(NA_HEADS):
        cols = slice(h * NA_HEAD_DIM, (h + 1) * NA_HEAD_DIM)
        s = lax.dot_general(q_ref[:, cols], k_ref[:, cols], (((1,), (1,)), ((), ())),
                            preferred_element_type=F32) * scale
        m = jnp.max(s, axis=-1, keepdims=True)
        p = jnp.exp(s - m)
        denom = jnp.sum(p, axis=-1, keepdims=True)
        o = jnp.dot(p.astype(BF16), v_ref[:, cols], preferred_element_type=F32)
        o_ref[:, cols] = (o / denom).astype(o_ref.dtype)


def _na_attention(qkv, rpb):
    dh, nh = NA_HEAD_DIM, NA_HEADS
    ctx_blk0 = M_LAT // CTX_LEN
    lat = pl.pallas_call(
        _na_lat_kernel,
        grid=(nh, BATCH),
        in_specs=[
            pl.BlockSpec(memory_space=pltpu.SMEM),
            pl.BlockSpec((SEQ, dh), lambda h, b: (b, h)),
            pl.BlockSpec((SEQ, dh), lambda h, b: (b, nh + h)),
            pl.BlockSpec((SEQ, dh), lambda h, b: (b, 2 * nh + h)),
            pl.BlockSpec((CTX_LEN, dh), lambda h, b: (ctx_blk0 + b, nh + h)),
            pl.BlockSpec((CTX_LEN, dh), lambda h, b: (ctx_blk0 + b, 2 * nh + h)),
        ],
        out_specs=pl.BlockSpec((SEQ, dh), lambda h, b: (b, h)),
        out_shape=jax.ShapeDtypeStruct((M_ALL, D_MODEL), BF16),
        scratch_shapes=[
            pltpu.VMEM((NA_PAIR_TILES, GRID_W, 2 * GRID_W), F32),
            pltpu.VMEM((NA_QROWS * GRID_W, NA_KROWS * GRID_W), F32),
        ],
        compiler_params=_params(2),
        name="na_latent",
    )(rpb.reshape(nh, NA_DR * NA_DC), qkv, qkv, qkv, qkv, qkv)
    ctx = pl.pallas_call(
        _na_ctx_kernel,
        grid=(BATCH,),
        in_specs=[
            pl.BlockSpec((CTX_LEN, D_MODEL), lambda b: (ctx_blk0 + b, 0)),
            pl.BlockSpec((CTX_LEN, D_MODEL), lambda b: (ctx_blk0 + b, 1)),
            pl.BlockSpec((CTX_LEN, D_MODEL), lambda b: (ctx_blk0 + b, 2)),
        ],
        out_specs=pl.BlockSpec((CTX_LEN, D_MODEL), lambda b: (b, 0)),
        out_shape=jax.ShapeDtypeStruct((M_CTX, D_MODEL), BF16),
        compiler_params=_params(1),
        name="na_context",
    )(qkv, qkv, qkv)
    return lax.dynamic_update_slice(lat, ctx, (M_LAT, 0))


RNN_LEN = CTX_LEN + SEQ
SCAN_UNROLL = 8


def _shift_rows(x, offset, length):
    if offset == 0:
        return x
    t = lax.broadcasted_iota(jnp.int32, (length, 1), 0)
    rolled = pltpu.roll(x, (-offset) % length, 0)
    return jnp.where((t + offset >= 0) & (t + offset < length), rolled, 0.0)


def _rnn_conv(x, cw, cb):
    length = x.shape[0]
    left = RNN_CONV // 2
    y = cb
    for j in range(RNN_CONV):
        y = y + _shift_rows(x, j - left, length) * cw[j:j + 1]
    return y


def _block_scan(a, b, reverse):
    row = lax.broadcasted_iota(jnp.int32, a.shape, 0)
    for s in (1, 2, 4):
        if reverse:
            keep = row < SUBLANES - s
            shift = SUBLANES - s
        else:
            keep = row >= s
            shift = s
        a_sh = pltpu.roll(a, shift, 0)
        b_sh = pltpu.roll(b, shift, 0)
        b = jnp.where(keep, a * b_sh + b, b)
        a = jnp.where(keep, a * a_sh, a)
    return a, b


def _rglru_kernel(xc_ref, xl_ref, yc_ref, yl_ref, cw_ref, cb_ref, wg_ref, bg_ref, lam_ref,
                  oc_ref, ol_ref, af_ref, bf_ref, ar_ref, br_ref, hf_ref, hr_ref):
    cw = cw_ref[...]
    cb = cb_ref[...]
    xr_c = _rnn_conv(xc_ref[...].astype(F32), cw, cb)
    xr_l = _rnn_conv(xl_ref[...].astype(F32), cw, cb)

    def half_tanh_gate(xb, d, gate):
        w = (0.5 * wg_ref[2 * d + gate]).astype(BF16)
        return jnp.tanh(jnp.dot(xb, w, preferred_element_type=F32) + 0.5 * bg_ref[2 * d + gate])

    def gates(xr, d, a_ref, b_ref, rows):
        xb = xr.astype(BF16)
        half_xr = 0.5 * xr
        t_r = half_tanh_gate(xb, d, 0)
        t_i = half_tanh_gate(xb, d, 1)
        neg_lam = -lam_ref[d:d + 1, :]
        softplus = jnp.maximum(neg_lam, 0.0) + jnp.log1p(jnp.exp(-jnp.abs(neg_lam)))
        c = (-0.5 * RG_C * LOG2_E) * softplus
        a = jnp.exp2(c * t_r + c)
        a_ref[rows, :] = a
        y = 1.0 - a * a
        root = jnp.where(y > 0.0, y * lax.rsqrt(y), 0.0)
        b_ref[rows, :] = root * (half_xr * t_i + half_xr)

    ctx_f, lat_f = slice(0, CTX_LEN), slice(CTX_LEN, RNN_LEN)
    lat_r, ctx_r = slice(0, SEQ), slice(SEQ, RNN_LEN)
    gates(xr_c, 0, af_ref, bf_ref, ctx_f)
    gates(xr_l, 0, af_ref, bf_ref, lat_f)
    gates(xr_c, 1, ar_ref, br_ref, ctx_r)
    gates(xr_l, 1, ar_ref, br_ref, lat_r)

    nblk = RNN_LEN // SUBLANES
    last_row = slice(SUBLANES - 1, SUBLANES)
    first_row = slice(0, 1)
    blk_shape = (SUBLANES, RNN_HEAD_DIM)

    def block(n, carry):
        cf, cr = carry
        rows_f = pl.ds(pl.multiple_of(n * SUBLANES, SUBLANES), SUBLANES)
        rows_r = pl.ds(pl.multiple_of((nblk - 1 - n) * SUBLANES, SUBLANES), SUBLANES)
        a_f, b_f = _block_scan(af_ref[rows_f, :], bf_ref[rows_f, :], reverse=False)
        a_r, b_r = _block_scan(ar_ref[rows_r, :], br_ref[rows_r, :], reverse=True)
        hf_ref[rows_f, :] = a_f * cf + b_f
        hr_ref[rows_r, :] = a_r * cr + b_r
        cf = jnp.broadcast_to(a_f[last_row], blk_shape) * cf + jnp.broadcast_to(b_f[last_row], blk_shape)
        cr = jnp.broadcast_to(a_r[first_row], blk_shape) * cr + jnp.broadcast_to(b_r[first_row], blk_shape)
        return cf, cr

    zero = jnp.zeros(blk_shape, F32)
    lax.fori_loop(0, nblk, block, (zero, zero), unroll=SCAN_UNROLL)

    h_c = hf_ref[ctx_f, :] + hr_ref[ctx_r, :]
    h_l = hf_ref[lat_f, :] + hr_ref[lat_r, :]
    oc_ref[...] = (jax.nn.gelu(yc_ref[...].astype(F32)) * h_c).astype(oc_ref.dtype)
    ol_ref[...] = (jax.nn.gelu(yl_ref[...].astype(F32)) * h_l).astype(ol_ref.dtype)


def _rglru(xin, conv_w, conv_b, w_gate, b_gate, lam, j):
    dh, nh = RNN_HEAD_DIM, RNN_HEADS
    n_c = conv_w.shape[0]
    ctx_blk0 = M_LAT // CTX_LEN
    wg = w_gate.reshape(n_c, 4, nh, dh, dh)
    bg = b_gate.reshape(n_c, 4, nh, 1, dh)
    out_c, out_l = pl.pallas_call(
        _rglru_kernel,
        grid=(BATCH, nh),
        in_specs=[
            pl.BlockSpec((CTX_LEN, dh), lambda b, h: (ctx_blk0 + b, nh + h)),
            pl.BlockSpec((SEQ, dh), lambda b, h: (b, nh + h)),
            pl.BlockSpec((CTX_LEN, dh), lambda b, h: (ctx_blk0 + b, h)),
            pl.BlockSpec((SEQ, dh), lambda b, h: (b, h)),
            pl.BlockSpec((None, RNN_CONV, dh), lambda b, h: (j, 0, h)),
            pl.BlockSpec((None, 1, dh), lambda b, h: (j, 0, h)),
            pl.BlockSpec((None, 4, None, dh, dh), lambda b, h: (j, 0, h, 0, 0)),
            pl.BlockSpec((None, 4, None, 1, dh), lambda b, h: (j, 0, h, 0, 0)),
            pl.BlockSpec((None, 2, dh), lambda b, h: (j, 0, h)),
        ],
        out_specs=[
            pl.BlockSpec((CTX_LEN, dh), lambda b, h: (b, h)),
            pl.BlockSpec((SEQ, dh), lambda b, h: (b, h)),
        ],
        out_shape=[
            jax.ShapeDtypeStruct((M_CTX, RNN_WIDTH), BF16),
            jax.ShapeDtypeStruct((M_ALL, RNN_WIDTH), BF16),
        ],
        scratch_shapes=[pltpu.VMEM((RNN_LEN, dh), F32)] * 6,
        compiler_params=_params(2),
        name="rglru",
    )(xin, xin, xin, xin, conv_w, conv_b.reshape(n_c, 1, RNN_WIDTH), wg, bg, lam)
    return lax.dynamic_update_slice(out_l, out_c, (M_LAT, 0))


def kernel(x, c, ctx, c_ctx, ada_w, ada_b, norm_g, ffn_w_up, ffn_conv_w, ffn_conv_b, ffn_w_down,
           a_w_in, a_g_v, a_w_s, a_b_s, a_w_out, b_w_qkv, b_rpb, b_w_out,
           c_w_in, c_conv_w, c_conv_b, c_w_gate, c_b_gate, c_lam, c_w_out, final_g):
    h = (x.reshape(M_LAT, D_MODEL), ctx.reshape(M_CTX, D_MODEL))
    cvec = jnp.concatenate(
        [c, c_ctx[None], jnp.zeros((SUBLANES - BATCH - 1, D_MODEL), F32)], axis=0)
    mod = _ada_mod(cvec, ada_w, ada_b)

    for i in range(DEPTH):
        last = i == DEPTH - 1
        kind, j = i % N_MIXERS, i // N_MIXERS
        ffn_rows = M_LAT if last else M_ALL
        n_rows = M_LAT if (last and kind == 0) else M_ALL

        n = _norm_mod(h, norm_g, mod, i, 0, n_rows)
        if kind == 0:
            z = _matmul(n, a_w_in, j, n_rows=n_rows, tn=1024, out_dtype=BF16, act="gelu")
            y = _sgu(z, a_g_v, a_w_s, a_b_s, j, n_rows=n_rows)
            w_out = a_w_out
        elif kind == 1:
            qkv = _matmul(n, b_w_qkv, j, n_rows=n_rows, tn=1024, out_dtype=BF16)
            y = _na_attention(qkv, b_rpb[j])
            w_out = b_w_out
        else:
            xin = _matmul(n, c_w_in, j, n_rows=n_rows, tn=1024, out_dtype=F32)
            y = _rglru(xin, c_conv_w, c_conv_b, c_w_gate, c_b_gate, c_lam, j)
            w_out = c_w_out
        h = _matmul_resid(y, w_out, j, h, mod, i, 2, n_rows=n_rows, tn=1024)

        n = _norm_mod(h, norm_g, mod, i, 1, ffn_rows)
        a = _ffn_up(n, ffn_w_up, ffn_conv_w, ffn_conv_b, i, n_rows=ffn_rows)
        h = _matmul_resid(a, ffn_w_down, i, h, mod, i, 5, n_rows=ffn_rows, tn=512)

    return _final_norm(h, final_g).reshape(BATCH, SEQ, D_MODEL)
```

```python
import functools

import jax
import jax.numpy as jnp
from jax import lax
from jax.experimental import pallas as pl
from jax.experimental.pallas import tpu as pltpu

F32 = jnp.float32
BF16 = jnp.bfloat16

D_MODEL = 2048
BATCH = 2
SEQ = 4096
DEPTH = 4
GRID_W = 64
CTX_LEN = 256
N_MIXERS = 3
N_MOD = 6
EPS = 1e-6
NEG_INF = -1e30
LOG2_E = 1.4426950408889634
D_FF = 5632
CHUNK = 128
A_WIDTH = 2 * D_MODEL
A_GROUPS = 16
NA_HEADS = 16
NA_HEAD_DIM = D_MODEL // NA_HEADS
NA_KH = 8
NA_KW = 16
RNN_WIDTH = D_MODEL
RNN_HEADS = 16
RNN_HEAD_DIM = RNN_WIDTH // RNN_HEADS
RNN_CONV = 4
RG_C = 8.0

M_LAT = BATCH * SEQ
M_CTX = BATCH * CTX_LEN
M_ALL = M_LAT + M_CTX
GRID_ROWS = SEQ // GRID_W

VMEM_LIMIT_BYTES = 56 * 1024 * 1024
SUBLANES = 8
LANES = 128
BF16_ROWS = 16

ROW_TILE = {M_ALL: M_ALL // 8, M_LAT: M_LAT // 8}
ROW_TILE_WIDE_K = {M_ALL: M_ALL // 16, M_LAT: M_LAT // 16}
NORM_ROW_TILE = 512
NORM_CHUNK = BF16_ROWS


def _params(n_axes):
    return pltpu.CompilerParams(
        dimension_semantics=("arbitrary",) * n_axes, vmem_limit_bytes=VMEM_LIMIT_BYTES)


def _sigmoid(x):
    return 0.5 * jnp.tanh(0.5 * x) + 0.5


def _mod_spec(layer, which, tn=D_MODEL, grid_rank=1):
    if grid_rank == 1:
        return pl.BlockSpec((None, SUBLANES, tn), lambda i: (layer * N_MOD + which, 0, 0))
    return pl.BlockSpec((None, SUBLANES, tn), lambda j, i: (layer * N_MOD + which, 0, j))


def _rows_of_group(mod_ref, row0, tm):
    r = row0 + lax.broadcasted_iota(jnp.int32, (tm, 1), 0)
    m = mod_ref[...]
    out = m[BATCH:BATCH + 1]
    for b in reversed(range(BATCH)):
        out = jnp.where(r < (b + 1) * SEQ, m[b:b + 1], out)
    return out


def _split_stream_specs(tm, tn, row_of, col_of):
    assert M_LAT % tm == 0 and tm == M_CTX
    last_lat = M_LAT // tm - 1
    return [
        pl.BlockSpec((tm, tn), lambda *ids: (jnp.minimum(row_of(*ids), last_lat), col_of(*ids))),
        pl.BlockSpec((tm, tn), lambda *ids: (0, col_of(*ids))),
    ]


def _stream_tile(h_refs, i, tm, rows=slice(None)):
    if len(h_refs) == 1:
        return h_refs[0][rows, :]
    lat_ref, ctx_ref = h_refs
    return jnp.where(i < M_LAT // tm, lat_ref[rows, :], ctx_ref[rows, :])


def _ada_kernel(c_ref, w_ref, b_ref, o_ref):
    c = c_ref[...]
    s = (c * _sigmoid(c)).astype(BF16)
    o_ref[...] = jnp.dot(s, w_ref[...].astype(BF16), preferred_element_type=F32) + b_ref[...]


def _ada_mod(cvec, ada_w, ada_b):
    depth, d, n = ada_w.shape
    tn = 2048
    out = pl.pallas_call(
        _ada_kernel,
        grid=(depth, n // tn),
        in_specs=[
            pl.BlockSpec((SUBLANES, d), lambda l, j: (0, 0)),
            pl.BlockSpec((None, d, tn), lambda l, j: (l, 0, j)),
            pl.BlockSpec((None, 1, tn), lambda l, j: (l, 0, j)),
        ],
        out_specs=pl.BlockSpec((None, SUBLANES, tn), lambda l, j: (l, 0, j)),
        out_shape=jax.ShapeDtypeStruct((depth, SUBLANES, n), F32),
        compiler_params=_params(2),
        name="ada_mod",
    )(cvec, ada_w, ada_b.reshape(depth, 1, n))
    return out.reshape(depth, SUBLANES, N_MOD, d).transpose(0, 2, 1, 3).reshape(depth * N_MOD, SUBLANES, d)


def _norm_mod_kernel(*refs):
    *h_refs, g_ref, shift_ref, scale_ref, o_ref = refs
    i = pl.program_id(0)
    tm = o_ref.shape[0]
    group = pl.ds(jnp.minimum(i * tm // SEQ, BATCH), 1)
    gain = g_ref[...] * (1 + scale_ref[group, :])
    shift = shift_ref[group, :]

    def chunk(r, carry):
        rows = pl.ds(pl.multiple_of(r * NORM_CHUNK, NORM_CHUNK), NORM_CHUNK)
        x = _stream_tile(h_refs, i, tm, rows)
        y = x * lax.rsqrt(jnp.mean(x * x, axis=-1, keepdims=True) + EPS)
        o_ref[rows, :] = (y * gain + shift).astype(o_ref.dtype)
        return carry

    lax.fori_loop(0, tm // NORM_CHUNK, chunk, 0, unroll=True)


def _norm_mod(h, norm_g, mod, layer, which_norm, n_rows):
    tm = NORM_ROW_TILE
    g = norm_g.reshape(DEPTH * 2, 1, D_MODEL)
    h_arrays = h if isinstance(h, tuple) else (h,)
    h_specs = (_split_stream_specs(tm, D_MODEL, lambda i: i, lambda i: 0) if isinstance(h, tuple)
               else [pl.BlockSpec((tm, D_MODEL), lambda i: (i, 0))])
    return pl.pallas_call(
        _norm_mod_kernel,
        grid=(n_rows // tm,),
        in_specs=[
            *h_specs,
            pl.BlockSpec((None, 1, D_MODEL), lambda i: (layer * 2 + which_norm, 0, 0)),
            _mod_spec(layer, 3 * which_norm),
            _mod_spec(layer, 3 * which_norm + 1),
        ],
        out_specs=pl.BlockSpec((tm, D_MODEL), lambda i: (i, 0)),
        out_shape=jax.ShapeDtypeStruct((M_ALL, D_MODEL), BF16),
        compiler_params=_params(1),
        name="norm_mod",
    )(*h_arrays, g, mod, mod)


def _final_norm_kernel(h_ref, g_ref, o_ref):
    gain = g_ref[...]

    def chunk(r, carry):
        rows = pl.ds(pl.multiple_of(r * NORM_CHUNK, NORM_CHUNK), NORM_CHUNK)
        x = h_ref[rows, :]
        o_ref[rows, :] = x * lax.rsqrt(jnp.mean(x * x, axis=-1, keepdims=True) + EPS) * gain
        return carry

    lax.fori_loop(0, h_ref.shape[0] // NORM_CHUNK, chunk, 0, unroll=True)


def _final_norm(h, final_g):
    tm = NORM_ROW_TILE
    return pl.pallas_call(
        _final_norm_kernel,
        grid=(M_LAT // tm,),
        in_specs=[
            pl.BlockSpec((tm, D_MODEL), lambda i: (i, 0)),
            pl.BlockSpec((1, D_MODEL), lambda i: (0, 0)),
        ],
        out_specs=pl.BlockSpec((tm, D_MODEL), lambda i: (i, 0)),
        out_shape=jax.ShapeDtypeStruct((M_LAT, D_MODEL), F32),
        compiler_params=_params(1),
        name="final_norm",
    )(h, final_g.reshape(1, D_MODEL))


def _mm_kernel(x_hbm, w_ref, o_hbm, wbf_ref, xbuf, obuf, xsem, osem, *, act, tm, n_tiles, n_col_steps):
    j = pl.program_id(0)
    tn = wbf_ref.shape[1]
    wbf_ref[...] = w_ref[...].astype(BF16)

    def x_copy(i):
        return pltpu.make_async_copy(x_hbm.at[pl.ds(i * tm, tm), :], xbuf.at[i % 2], xsem.at[i % 2])

    def o_copy(i):
        cols = pl.ds(pl.multiple_of(j * tn, tn), tn)
        return pltpu.make_async_copy(obuf.at[i % 2], o_hbm.at[pl.ds(i * tm, tm), cols], osem.at[i % 2])

    @pl.when(j == 0)
    def _():
        x_copy(0).start()

    for i in range(n_tiles):
        if i + 1 < n_tiles:
            x_copy(i + 1).start()
        x_copy(i).wait()
        acc = jnp.dot(xbuf[i % 2], wbf_ref[...], preferred_element_type=F32)
        if act == "gelu":
            acc = jax.nn.gelu(acc)
        if i >= 2:
            o_copy(i - 2).wait()
        obuf[i % 2] = acc.astype(obuf.dtype)
        o_copy(i).start()

    @pl.when(j + 1 < n_col_steps)
    def _():
        x_copy(0).start()

    o_copy(n_tiles - 2).wait()
    o_copy(n_tiles - 1).wait()


def _mm_resid_kernel(x_ref, w_ref, gate_ref, *refs):
    *res_refs, o_ref, wbf_ref = refs
    i = pl.program_id(1)
    tm = x_ref.shape[0]

    @pl.when(i == 0)
    def _():
        wbf_ref[...] = w_ref[...].astype(BF16)

    acc = jnp.dot(x_ref[...], wbf_ref[...], preferred_element_type=F32)
    o_ref[...] = _stream_tile(res_refs, i, tm) + _rows_of_group(gate_ref, i * tm, tm) * acc


def _matmul(x, w, w_idx, *, n_rows, tn, out_dtype, act=None):
    tm = ROW_TILE[n_rows]
    _, k, n = w.shape
    n_tiles, n_col_steps = n_rows // tm, n // tn
    assert n_tiles % 2 == 0 and n_tiles >= 2
    return pl.pallas_call(
        functools.partial(_mm_kernel, act=act, tm=tm, n_tiles=n_tiles, n_col_steps=n_col_steps),
        grid=(n_col_steps,),
        in_specs=[
            pl.BlockSpec(memory_space=pl.ANY),
            pl.BlockSpec((None, k, tn), lambda j: (w_idx, 0, j)),
        ],
        out_specs=pl.BlockSpec(memory_space=pl.ANY),
        out_shape=jax.ShapeDtypeStruct((x.shape[0], n), out_dtype),
        scratch_shapes=[
            pltpu.VMEM((k, tn), BF16),
            pltpu.VMEM((2, tm, k), BF16),
            pltpu.VMEM((2, tm, tn), out_dtype),
            pltpu.SemaphoreType.DMA((2,)),
            pltpu.SemaphoreType.DMA((2,)),
        ],
        compiler_params=_params(1),
        name="matmul",
    )(x, w)


def _matmul_resid(x, w, w_idx, h, mod, layer, which_gate, *, n_rows, tn):
    _, k, n = w.shape
    w_tile_bytes = k * tn * 4
    w_buffers = pl.Buffered(1) if w_tile_bytes > 12 * 2**20 else None
    if isinstance(h, tuple):
        tm = M_CTX
        res_arrays, aliases = h, {}
        res_specs = _split_stream_specs(tm, tn, lambda j, i: i, lambda j, i: j)
    else:
        tm = (ROW_TILE_WIDE_K if w_tile_bytes > 8 * 2**20 else ROW_TILE)[n_rows]
        res_arrays, aliases = (h,), {3: 0}
        res_specs = [pl.BlockSpec((tm, tn), lambda j, i: (i, j))]
    return pl.pallas_call(
        _mm_resid_kernel,
        grid=(n // tn, n_rows // tm),
        in_specs=[
            pl.BlockSpec((tm, k), lambda j, i: (i, 0)),
            pl.BlockSpec((None, k, tn), lambda j, i: (w_idx, 0, j), pipeline_mode=w_buffers),
            _mod_spec(layer, which_gate, tn, grid_rank=2),
            *res_specs,
        ],
        out_specs=pl.BlockSpec((tm, tn), lambda j, i: (i, j)),
        out_shape=jax.ShapeDtypeStruct((M_ALL, n), F32),
        scratch_shapes=[pltpu.VMEM((k, tn), BF16)],
        input_output_aliases=aliases,
        compiler_params=_params(2),
        name="matmul_resid",
    )(x, w, mod, *res_arrays)


HALO = BF16_ROWS


SEQ_STARTS = tuple(b * SEQ for b in range(BATCH)) + tuple(M_LAT + b * CTX_LEN for b in range(BATCH))
FIX_ROWS = BF16_ROWS


def _is_seq_edge(row):
    edge = row == M_ALL
    for s in SEQ_STARTS:
        edge = edge | (row == s)
    return edge


def _conv3(prev, cur, nxt, cw_ref, cb_ref):
    reps = (prev.shape[0] // SUBLANES, 1)
    tap = [jnp.tile(cw_ref[j * SUBLANES:(j + 1) * SUBLANES, :], reps) for j in range(3)]
    y = jnp.tile(cb_ref[...], reps) + prev * tap[0]
    y = y + cur * tap[1]
    return y + nxt * tap[2]


def _silu_gate(g, v):
    half_g = 0.5 * g
    return (half_g * jnp.tanh(half_g) + half_g) * v


def _ffn_up_kernel(x_ref, xp_ref, xn_ref, wg_ref, wv_ref, cwg_ref, cwv_ref, cbg_ref, cbv_ref,
                   o_ref, wg_bf, wv_bf, xs_ref, zg_ref, zv_ref):
    i = pl.program_id(1)
    tm = x_ref.shape[0]
    row0 = i * tm

    @pl.when(i == 0)
    def _():
        wg_bf[...] = wg_ref[...].astype(BF16)
        wv_bf[...] = wv_ref[...].astype(BF16)

    no_rows = jnp.zeros_like(xp_ref)
    xs_ref[0:HALO, :] = jnp.where(_is_seq_edge(row0), no_rows, xp_ref[...])
    xs_ref[HALO:HALO + tm, :] = x_ref[...]
    xs_ref[HALO + tm:, :] = jnp.where(_is_seq_edge(row0 + tm), no_rows, xn_ref[...])
    xs = xs_ref[...]
    zg_ref[...] = jnp.dot(xs, wg_bf[...], preferred_element_type=F32)
    zv_ref[...] = jnp.dot(xs, wv_bf[...], preferred_element_type=F32)

    def conv(z_ref, cw_ref, cb_ref):
        return _conv3(z_ref[HALO - 1:HALO - 1 + tm, :], z_ref[HALO:HALO + tm, :],
                      z_ref[HALO + 1:HALO + 1 + tm, :], cw_ref, cb_ref)

    o_ref[...] = _silu_gate(conv(zg_ref, cwg_ref, cbg_ref), conv(zv_ref, cwv_ref, cbv_ref)).astype(o_ref.dtype)

    n_win = 2 * FIX_ROWS + 2 * SUBLANES
    w_row = lax.broadcasted_iota(jnp.int32, (n_win, 1), 0)
    start_row = SUBLANES + FIX_ROWS

    def fixed_conv(z_ref, cw_ref, cb_ref, win):
        z = z_ref[win, :]
        prev = jnp.where(w_row == start_row, 0.0, pltpu.roll(z, 1, 0))
        nxt = jnp.where(w_row == start_row - 1, 0.0, pltpu.roll(z, n_win - 1, 0))
        return _conv3(prev, z, nxt, cw_ref, cb_ref)[SUBLANES:SUBLANES + 2 * FIX_ROWS]

    for s in SEQ_STARTS[1:]:
        @pl.when((s > row0) & (s < row0 + tm))
        def _():
            win = pl.ds(pl.multiple_of(HALO + s - row0 - start_row, SUBLANES), n_win)
            rows = pl.ds(pl.multiple_of(s - row0 - FIX_ROWS, FIX_ROWS), 2 * FIX_ROWS)
            g = fixed_conv(zg_ref, cwg_ref, cbg_ref, win)
            v = fixed_conv(zv_ref, cwv_ref, cbv_ref, win)
            o_ref[rows, :] = _silu_gate(g, v).astype(o_ref.dtype)


def _ffn_up(n, w_up, conv_w, conv_b, layer, *, n_rows):
    tm, tn = ROW_TILE[n_rows], 512
    k = w_up.shape[1]
    nj = D_FF // tn
    halo_per_tile = tm // HALO
    last_halo = M_ALL // HALO - 1
    cw = jnp.repeat(conv_w, SUBLANES, axis=1)
    cb = jnp.broadcast_to(conv_b[:, None, :], (DEPTH, SUBLANES, 2 * D_FF))
    return pl.pallas_call(
        _ffn_up_kernel,
        grid=(nj, n_rows // tm),
        in_specs=[
            pl.BlockSpec((tm, k), lambda j, i: (i, 0)),
            pl.BlockSpec((HALO, k), lambda j, i: (jnp.maximum(i * halo_per_tile - 1, 0), 0)),
            pl.BlockSpec((HALO, k), lambda j, i: (jnp.minimum((i + 1) * halo_per_tile, last_halo), 0)),
            pl.BlockSpec((None, k, tn), lambda j, i: (layer, 0, j)),
            pl.BlockSpec((None, k, tn), lambda j, i: (layer, 0, j + nj)),
            pl.BlockSpec((None, 3 * SUBLANES, tn), lambda j, i: (layer, 0, j)),
            pl.BlockSpec((None, 3 * SUBLANES, tn), lambda j, i: (layer, 0, j + nj)),
            pl.BlockSpec((None, SUBLANES, tn), lambda j, i: (layer, 0, j)),
            pl.BlockSpec((None, SUBLANES, tn), lambda j, i: (layer, 0, j + nj)),
        ],
        out_specs=pl.BlockSpec((tm, tn), lambda j, i: (i, j)),
        out_shape=jax.ShapeDtypeStruct((M_ALL, D_FF), BF16),
        scratch_shapes=[
            pltpu.VMEM((k, tn), BF16),
            pltpu.VMEM((k, tn), BF16),
            pltpu.VMEM((tm + 2 * HALO, k), BF16),
            pltpu.VMEM((tm + 2 * HALO, tn), F32),
            pltpu.VMEM((tm + 2 * HALO, tn), F32),
        ],
        compiler_params=_params(2),
        name="ffn_up_conv_gate",
    )(n, n, n, w_up, w_up, cw, cw, cb, cb)


def _sgu_kernel(u_ref, v_ref, gv_ref, ws_ref, bs_ref, o_ref):
    tm = u_ref.shape[0]
    gw = A_WIDTH // A_GROUPS
    gv = gv_ref[...]
    for c in range(tm // CHUNK):
        rows = slice(c * CHUNK, (c + 1) * CHUNK)
        v = v_ref[rows, :].astype(F32)
        vn = v * lax.rsqrt(jnp.mean(v * v, axis=-1, keepdims=True) + EPS) * gv
        vn = vn.astype(BF16)
        for g in range(A_GROUPS):
            cols = slice(g * gw, (g + 1) * gw)
            s = jnp.dot(ws_ref[g].astype(BF16), vn[:, cols], preferred_element_type=F32)
            s = s + bs_ref[:, g:g + 1]
            o_ref[rows, cols] = (u_ref[rows, cols].astype(F32) * s).astype(o_ref.dtype)


def _sgu(z, g_v, w_s, b_s, j, *, n_rows):
    tm = 4 * CHUNK
    n_a = g_v.shape[0]
    return pl.pallas_call(
        _sgu_kernel,
        grid=(n_rows // tm,),
        in_specs=[
            pl.BlockSpec((tm, A_WIDTH), lambda i: (i, 0)),
            pl.BlockSpec((tm, A_WIDTH), lambda i: (i, 1)),
            pl.BlockSpec((None, 1, A_WIDTH), lambda i: (j, 0, 0)),
            pl.BlockSpec((None, A_GROUPS, CHUNK, CHUNK), lambda i: (j, 0, 0, 0)),
            pl.BlockSpec((None, CHUNK, A_GROUPS), lambda i: (j, 0, 0)),
        ],
        out_specs=pl.BlockSpec((tm, A_WIDTH), lambda i: (i, 0)),
        out_shape=jax.ShapeDtypeStruct((M_ALL, A_WIDTH), BF16),
        compiler_params=_params(1),
        name="sgu",
    )(z, z, g_v.reshape(n_a, 1, A_WIDTH), w_s, b_s.transpose(0, 2, 1))


NA_DR = 2 * NA_KH - 1
NA_DC = 2 * NA_KW - 1
NA_PAIR_TILES = NA_DR + 1
NA_QROWS = 4
NA_KROWS = NA_KH + NA_QROWS
NA_BLOCK_UNROLL = 14


def _na_bias_pairs(rpb_ref, head, pair_ref):
    shape = (GRID_W, 2 * GRID_W)
    q = lax.broadcasted_iota(jnp.int32, shape, 0)
    lane = lax.broadcasted_iota(jnp.int32, shape, 1)
    kcol = lane & (GRID_W - 1)
    upper = lane >= GRID_W
    dc = jnp.clip(kcol - q, -(NA_KW - 1), NA_KW - 1) + NA_KW - 1
    c0 = jnp.clip(q - NA_KW // 2, 0, GRID_W - NA_KW)
    ok = (kcol >= c0) & (kcol < c0 + NA_KW)
    for d in range(-1, NA_DR):
        acc = jnp.full(shape, NEG_INF, F32)
        for c in range(NA_DC):
            lo = rpb_ref[head, d * NA_DC + c] if d >= 0 else NEG_INF
            hi = rpb_ref[head, (d + 1) * NA_DC + c] if d + 1 < NA_DR else NEG_INF
            acc = jnp.where(dc == c, jnp.where(upper, hi, lo), acc)
        pair_ref[d + 1] = jnp.where(ok, acc * LOG2_E, NEG_INF)


def _na_window_start(m):
    return min(max(m * NA_QROWS - NA_KH // 2, 0), GRID_ROWS - NA_KROWS)


def _na_block_bias(pair_ref, m):
    upper = lax.broadcasted_iota(jnp.int32, (GRID_W, 2 * GRID_W), 1) >= GRID_W
    masked = jnp.full((GRID_W, 2 * GRID_W), NEG_INF, F32)
    ws = _na_window_start(m)
    rows = []
    for a in range(NA_QROWS):
        r = m * NA_QROWS + a
        r0 = min(max(r - NA_KH // 2, 0), GRID_ROWS - NA_KH)
        tiles = []
        for p in range(NA_KROWS // 2):
            key_row = ws + 2 * p
            in_lo = r0 <= key_row < r0 + NA_KH
            in_hi = r0 <= key_row + 1 < r0 + NA_KH
            if not (in_lo or in_hi):
                tiles.append(masked)
                continue
            tile = pair_ref[key_row - r + NA_KH]
            if not in_hi:
                tile = jnp.where(upper, NEG_INF, tile)
            elif not in_lo:
                tile = jnp.where(upper, tile, NEG_INF)
            tiles.append(tile)
        rows.append(jnp.concatenate(tiles, axis=1))
    return jnp.concatenate(rows, axis=0)


def _na_lat_kernel(rpb_ref, q_ref, k_ref, v_ref, kc_ref, vc_ref, o_ref, pair_ref, bias_ref):
    scale = NA_HEAD_DIM ** -0.5 * LOG2_E

    @pl.when(pl.program_id(1) == 0)
    def _():
        _na_bias_pairs(rpb_ref, pl.program_id(0), pair_ref)
        bias_ref[...] = _na_block_bias(pair_ref, 1)

    kc = kc_ref[...]
    vc = vc_ref[...]
    nt = (((1,), (1,)), ((), ()))
    n_q = NA_QROWS * GRID_W
    n_blocks = GRID_ROWS // NA_QROWS

    def attend(qrows, win, bias):
        q = q_ref[qrows, :]
        kw = k_ref[win, :]
        vw = v_ref[win, :]
        s =lax.dot_general(q, kw, nt, preferred_element_type=F32) * scale + bias
        sc = lax.dot_general(q, kc, nt, preferred_element_type=F32) * scale
        mx = jnp.maximum(jnp.max(s, axis=-1, keepdims=True), jnp.max(sc, axis=-1, keepdims=True))
        p_w = jnp.exp2(s - mx)
        p_c = jnp.exp2(sc - mx)
        denom = jnp.sum(p_w, axis=-1, keepdims=True) + jnp.sum(p_c, axis=-1, keepdims=True)
        o = jnp.dot(p_w.astype(BF16), vw, preferred_element_type=F32)
        o = o + jnp.dot(p_c.astype(BF16), vc, preferred_element_type=F32)
        o_ref[qrows, :] = (o / denom).astype(o_ref.dtype)

    def edge_block(m):
        ws = _na_window_start(m)
        attend(slice(m * n_q, (m + 1) * n_q), slice(ws * GRID_W, (ws + NA_KROWS) * GRID_W),
               _na_block_bias(pair_ref, m))

    def interior_block(m, carry):
        qrows = pl.ds(pl.multiple_of(m * n_q, n_q), n_q)
        ws = m * NA_QROWS - NA_KH // 2
        win = pl.ds(pl.multiple_of(ws * GRID_W, GRID_W), NA_KROWS * GRID_W)
        attend(qrows, win, bias_ref[...])
        return carry

    edge_block(0)
    lax.fori_loop(1, n_blocks - 1, interior_block, 0, unroll=NA_BLOCK_UNROLL)
    edge_block(n_blocks - 1)


def _na_ctx_kernel(q_ref, k_ref, v_ref, o_ref):
    scale = NA_HEAD_DIM ** -0.5
    for h in range(NA_HEADS):
        cols = slice(h * NA_HEAD_DIM, (h + 1) * NA_HEAD_DIM)
        s = lax.dot_general(q_ref[:, cols], k_ref[:, cols], (((1,), (1,)), ((), ())),
                            preferred_element_type=F32) * scale
        m = jnp.max(s, axis=-1, keepdims=True)
        p = jnp.exp(s - m)
        denom = jnp.sum(p, axis=-1, keepdims=True)
        o = jnp.dot(p.astype(BF16), v_ref[:, cols], preferred_element_type=F32)
        o_ref[:, cols] = (o / denom).astype(o_ref.dtype)


def _na_attention(qkv, rpb):
    dh, nh = NA_HEAD_DIM, NA_HEADS
    ctx_blk0 = M_LAT // CTX_LEN
    lat = pl.pallas_call(
        _na_lat_kernel,
        grid=(nh, BATCH),
        in_specs=[
            pl.BlockSpec(memory_space=pltpu.SMEM),
            pl.BlockSpec((SEQ, dh), lambda h, b: (b, h)),
            pl.BlockSpec((SEQ, dh), lambda h, b: (b, nh + h)),
            pl.BlockSpec((SEQ, dh), lambda h, b: (b, 2 * nh + h)),
            pl.BlockSpec((CTX_LEN, dh), lambda h, b: (ctx_blk0 + b, nh + h)),
            pl.BlockSpec((CTX_LEN, dh), lambda h, b: (ctx_blk0 + b, 2 * nh + h)),
        ],
        out_specs=pl.BlockSpec((SEQ, dh), lambda h, b: (b, h)),
        out_shape=jax.ShapeDtypeStruct((M_ALL, D_MODEL), BF16),
        scratch_shapes=[
            pltpu.VMEM((NA_PAIR_TILES, GRID_W, 2 * GRID_W), F32),
            pltpu.VMEM((NA_QROWS * GRID_W, NA_KROWS * GRID_W), F32),
        ],
        compiler_params=_params(2),
        name="na_latent",
    )(rpb.reshape(nh, NA_DR * NA_DC), qkv, qkv, qkv, qkv, qkv)
    ctx = pl.pallas_call(
        _na_ctx_kernel,
        grid=(BATCH,),
        in_specs=[
            pl.BlockSpec((CTX_LEN, D_MODEL), lambda b: (ctx_blk0 + b, 0)),
            pl.BlockSpec((CTX_LEN, D_MODEL), lambda b: (ctx_blk0 + b, 1)),
            pl.BlockSpec((CTX_LEN, D_MODEL), lambda b: (ctx_blk0 + b, 2)),
        ],
        out_specs=pl.BlockSpec((CTX_LEN, D_MODEL), lambda b: (b, 0)),
        out_shape=jax.ShapeDtypeStruct((M_CTX, D_MODEL), BF16),
        compiler_params=_params(1),
        name="na_context",
    )(qkv, qkv, qkv)
    return lax.dynamic_update_slice(lat, ctx, (M_LAT, 0))


RNN_LEN = CTX_LEN + SEQ
SCAN_UNROLL = 8


def _shift_rows(x, offset, length):
    if offset == 0:
        return x
    t = lax.broadcasted_iota(jnp.int32, (length, 1), 0)
    rolled = pltpu.roll(x, (-offset) % length, 0)
    return jnp.where((t + offset >= 0) & (t + offset < length), rolled, 0.0)


def _rnn_conv(x, cw, cb):
    length = x.shape[0]
    left = RNN_CONV // 2
    y = cb
    for j in range(RNN_CONV):
        y = y + _shift_rows(x, j - left, length) * cw[j:j + 1]
    return y


def _block_scan(a, b, reverse):
    row = lax.broadcasted_iota(jnp.int32, a.shape, 0)
    for s in (1, 2, 4):
        if reverse:
            keep = row < SUBLANES - s
            shift = SUBLANES - s
        else:
            keep = row >= s
            shift = s
        a_sh = pltpu.roll(a, shift, 0)
        b_sh = pltpu.roll(b, shift, 0)
        b = jnp.where(keep, a * b_sh + b, b)
        a = jnp.where(keep, a * a_sh, a)
    return a, b


def _rglru_kernel(xc_ref, xl_ref, yc_ref, yl_ref, cw_ref, cb_ref, wg_ref, bg_ref, lam_ref,
                  oc_ref, ol_ref, af_ref, bf_ref, ar_ref, br_ref, hf_ref, hr_ref):
    cw = cw_ref[...]
    cb = cb_ref[...]
    xr_c = _rnn_conv(xc_ref[...].astype(F32), cw, cb)
    xr_l = _rnn_conv(xl_ref[...].astype(F32), cw, cb)

    def half_tanh_gate(xb, d, gate):
        w = (0.5 * wg_ref[2 * d + gate]).astype(BF16)
        return jnp.tanh(jnp.dot(xb, w, preferred_element_type=F32) + 0.5 * bg_ref[2 * d + gate])

    def gates(xr, d, a_ref, b_ref, rows):
        xb = xr.astype(BF16)
        half_xr = 0.5 * xr
        t_r = half_tanh_gate(xb, d, 0)
        t_i = half_tanh_gate(xb, d, 1)
        neg_lam = -lam_ref[d:d + 1, :]
        softplus = jnp.maximum(neg_lam, 0.0) + jnp.log1p(jnp.exp(-jnp.abs(neg_lam)))
        c = (-0.5 * RG_C * LOG2_E) * softplus
        a = jnp.exp2(c * t_r + c)
        a_ref[rows, :] = a
        y = 1.0 - a * a
        root = jnp.where(y > 0.0, y * lax.rsqrt(y), 0.0)
        b_ref[rows, :] = root * (half_xr * t_i + half_xr)

    ctx_f, lat_f = slice(0, CTX_LEN), slice(CTX_LEN, RNN_LEN)
    lat_r, ctx_r = slice(0, SEQ), slice(SEQ, RNN_LEN)
    gates(xr_c, 0, af_ref, bf_ref, ctx_f)
    gates(xr_l, 0, af_ref, bf_ref, lat_f)
    gates(xr_c, 1, ar_ref, br_ref, ctx_r)
    gates(xr_l, 1, ar_ref, br_ref, lat_r)

    nblk = RNN_LEN // SUBLANES
    last_row = slice(SUBLANES - 1, SUBLANES)
    first_row = slice(0, 1)
    blk_shape = (SUBLANES, RNN_HEAD_DIM)

    def block(n, carry):
        cf, cr = carry
        rows_f = pl.ds(pl.multiple_of(n * SUBLANES, SUBLANES), SUBLANES)
        rows_r = pl.ds(pl.multiple_of((nblk - 1 - n) * SUBLANES, SUBLANES), SUBLANES)
        a_f, b_f = _block_scan(af_ref[rows_f, :], bf_ref[rows_f, :], reverse=False)
        a_r, b_r = _block_scan(ar_ref[rows_r, :], br_ref[rows_r, :], reverse=True)
        hf_ref[rows_f, :] = a_f * cf + b_f
        hr_ref[rows_r, :] = a_r * cr + b_r
        cf = jnp.broadcast_to(a_f[last_row], blk_shape) * cf + jnp.broadcast_to(b_f[last_row], blk_shape)
        cr = jnp.broadcast_to(a_r[first_row], blk_shape) * cr + jnp.broadcast_to(b_r[first_row], blk_shape)
        return cf, cr

    zero = jnp.zeros(blk_shape, F32)
    lax.fori_loop(0, nblk, block, (zero, zero), unroll=SCAN_UNROLL)

    h_c = hf_ref[ctx_f, :] + hr_ref[ctx_r, :]
    h_l = hf_ref[lat_f, :] + hr_ref[lat_r, :]
    oc_ref[...] = (jax.nn.gelu(yc_ref[...].astype(F32)) * h_c).astype(oc_ref.dtype)
    ol_ref[...] = (jax.nn.gelu(yl_ref[...].astype(F32)) * h_l).astype(ol_ref.dtype)


def _rglru(xin, conv_w, conv_b, w_gate, b_gate, lam, j):
    dh, nh = RNN_HEAD_DIM, RNN_HEADS
    n_c = conv_w.shape[0]
    ctx_blk0 = M_LAT // CTX_LEN
    wg = w_gate.reshape(n_c, 4, nh, dh, dh)
    bg = b_gate.reshape(n_c, 4, nh, 1, dh)
    out_c, out_l = pl.pallas_call(
        _rglru_kernel,
        grid=(BATCH, nh),
        in_specs=[
            pl.BlockSpec((CTX_LEN, dh), lambda b, h: (ctx_blk0 + b, nh + h)),
            pl.BlockSpec((SEQ, dh), lambda b, h: (b, nh + h)),
            pl.BlockSpec((CTX_LEN, dh), lambda b, h: (ctx_blk0 + b, h)),
            pl.BlockSpec((SEQ, dh), lambda b, h: (b, h)),
            pl.BlockSpec((None, RNN_CONV, dh), lambda b, h: (j, 0, h)),
            pl.BlockSpec((None, 1, dh), lambda b, h: (j, 0, h)),
            pl.BlockSpec((None, 4, None, dh, dh), lambda b, h: (j, 0, h, 0, 0)),
            pl.BlockSpec((None, 4, None, 1, dh), lambda b, h: (j, 0, h, 0, 0)),
            pl.BlockSpec((None, 2, dh), lambda b, h: (j, 0, h)),
        ],
        out_specs=[
            pl.BlockSpec((CTX_LEN, dh), lambda b, h: (b, h)),
            pl.BlockSpec((SEQ, dh), lambda b, h: (b, h)),
        ],
        out_shape=[
            jax.ShapeDtypeStruct((M_CTX, RNN_WIDTH), BF16),
            jax.ShapeDtypeStruct((M_ALL, RNN_WIDTH), BF16),
        ],
        scratch_shapes=[pltpu.VMEM((RNN_LEN, dh), F32)] * 6,
        compiler_params=_params(2),
        name="rglru",
    )(xin, xin, xin, xin, conv_w, conv_b.reshape(n_c, 1, RNN_WIDTH), wg, bg, lam)
    return lax.dynamic_update_slice(out_l, out_c, (M_LAT, 0))


def kernel(x, c, ctx, c_ctx, ada_w, ada_b, norm_g, ffn_w_up, ffn_conv_w, ffn_conv_b, ffn_w_down,
           a_w_in, a_g_v, a_w_s, a_b_s, a_w_out, b_w_qkv, b_rpb, b_w_out,
           c_w_in, c_conv_w, c_conv_b, c_w_gate, c_b_gate, c_lam, c_w_out, final_g):
    h = (x.reshape(M_LAT, D_MODEL), ctx.reshape(M_CTX, D_MODEL))
    cvec = jnp.concatenate(
        [c, c_ctx[None], jnp.zeros((SUBLANES - BATCH - 1, D_MODEL), F32)], axis=0)
    mod = _ada_mod(cvec, ada_w, ada_b)

    for i in range(DEPTH):
        last = i == DEPTH - 1
        kind, j = i % N_MIXERS, i // N_MIXERS
        ffn_rows = M_LAT if last else M_ALL
        n_rows = M_LAT if (last and kind == 0) else M_ALL

        n = _norm_mod(h, norm_g, mod, i, 0, n_rows)
        if kind == 0:
            z = _matmul(n, a_w_in, j, n_rows=n_rows, tn=1024, out_dtype=BF16, act="gelu")
            y = _sgu(z, a_g_v, a_w_s, a_b_s, j, n_rows=n_rows)
            w_out = a_w_out
        elif kind == 1:
            qkv = _matmul(n, b_w_qkv, j, n_rows=n_rows, tn=1024, out_dtype=BF16)
            y = _na_attention(qkv, b_rpb[j])
            w_out = b_w_out
        else:
            xin = _matmul(n, c_w_in, j, n_rows=n_rows, tn=1024, out_dtype=F32)
            y = _rglru(xin, c_conv_w, c_conv_b, c_w_gate, c_b_gate, c_lam, j)
            w_out = c_w_out
        h = _matmul_resid(y, w_out, j, h, mod, i, 2, n_rows=n_rows, tn=1024)

        n = _norm_mod(h, norm_g, mod, i, 1, ffn_rows)
        a = _ffn_up(n, ffn_w_up, ffn_conv_w, ffn_conv_b, i, n_rows=ffn_rows)
        h = _matmul_resid(a, ffn_w_down, i, h, mod, i, 5, n_rows=ffn_rows, tn=512)

    return _final_norm(h, final_g).reshape(BATCH, SEQ, D_MODEL)
```

```python
import functools

import jax
import jax.numpy as jnp
from jax import lax
from jax.experimental import pallas as pl
from jax.experimental.pallas import tpu as pltpu

F32 = jnp.float32
BF16 = jnp.bfloat16

D_MODEL = 2048
BATCH = 2
SEQ = 4096
DEPTH = 4
GRID_W = 64
CTX_LEN = 256
N_MIXERS = 3
N_MOD = 6
EPS = 1e-6
NEG_INF = -1e30
LOG2_E = 1.4426950408889634
D_FF = 5632
CHUNK = 128
A_WIDTH = 2 * D_MODEL
A_GROUPS = 16
NA_HEADS = 16
NA_HEAD_DIM = D_MODEL // NA_HEADS
NA_KH = 8
NA_KW = 16
RNN_WIDTH = D_MODEL
RNN_HEADS = 16
RNN_HEAD_DIM = RNN_WIDTH // RNN_HEADS
RNN_CONV = 4
RG_C = 8.0

M_LAT = BATCH * SEQ
M_CTX = BATCH * CTX_LEN
M_ALL = M_LAT + M_CTX
GRID_ROWS = SEQ // GRID_W

VMEM_LIMIT_BYTES = 56 * 1024 * 1024
SUBLANES = 8
LANES = 128
BF16_ROWS = 16

ROW_TILE = {M_ALL: M_ALL // 8, M_LAT: M_LAT // 8}
ROW_TILE_WIDE_K = {M_ALL: M_ALL // 16, M_LAT: M_LAT // 16}
NORM_ROW_TILE = 512
NORM_CHUNK = BF16_ROWS


def _params(n_axes):
    return pltpu.CompilerParams(
        dimension_semantics=("arbitrary",) * n_axes, vmem_limit_bytes=VMEM_LIMIT_BYTES)


def _sigmoid(x):
    return 0.5 * jnp.tanh(0.5 * x) + 0.5


def _mod_spec(layer, which, tn=D_MODEL, grid_rank=1):
    if grid_rank == 1:
        return pl.BlockSpec((None, SUBLANES, tn), lambda i: (layer * N_MOD + which, 0, 0))
    return pl.BlockSpec((None, SUBLANES, tn), lambda j, i: (layer * N_MOD + which, 0, j))


def _rows_of_group(mod_ref, row0, tm):
    r = row0 + lax.broadcasted_iota(jnp.int32, (tm, 1), 0)
    m = mod_ref[...]
    out = m[BATCH:BATCH + 1]
    for b in reversed(range(BATCH)):
        out = jnp.where(r < (b + 1) * SEQ, m[b:b + 1], out)
    return out


def _split_stream_specs(tm, tn, row_of, col_of):
    assert M_LAT % tm == 0 and tm == M_CTX
    last_lat = M_LAT // tm - 1
    return [
        pl.BlockSpec((tm, tn), lambda *ids: (jnp.minimum(row_of(*ids), last_lat), col_of(*ids))),
        pl.BlockSpec((tm, tn), lambda *ids: (0, col_of(*ids))),
    ]


def _stream_tile(h_refs, i, tm, rows=slice(None)):
    if len(h_refs) == 1:
        return h_refs[0][rows, :]
    lat_ref, ctx_ref = h_refs
    return jnp.where(i < M_LAT // tm, lat_ref[rows, :], ctx_ref[rows, :])


def _ada_kernel(c_ref, w_ref, b_ref, o_ref):
    c = c_ref[...]
    s = (c * _sigmoid(c)).astype(BF16)
    o_ref[...] = jnp.dot(s, w_ref[...].astype(BF16), preferred_element_type=F32) + b_ref[...]


def _ada_mod(cvec, ada_w, ada_b):
    depth, d, n = ada_w.shape
    tn = 2048
    out = pl.pallas_call(
        _ada_kernel,
        grid=(depth, n // tn),
        in_specs=[
            pl.BlockSpec((SUBLANES, d), lambda l, j: (0, 0)),
            pl.BlockSpec((None, d, tn), lambda l, j: (l, 0, j)),
            pl.BlockSpec((None, 1, tn), lambda l, j: (l, 0, j)),
        ],
        out_specs=pl.BlockSpec((None, SUBLANES, tn), lambda l, j: (l, 0, j)),
        out_shape=jax.ShapeDtypeStruct((depth, SUBLANES, n), F32),
        compiler_params=_params(2),
        name="ada_mod",
    )(cvec, ada_w, ada_b.reshape(depth, 1, n))
    return out.reshape(depth, SUBLANES, N_MOD, d).transpose(0, 2, 1, 3).reshape(depth * N_MOD, SUBLANES, d)


def _norm_mod_kernel(*refs):
    *h_refs, g_ref, shift_ref, scale_ref, o_ref = refs
    i = pl.program_id(0)
    tm = o_ref.shape[0]
    group = pl.ds(jnp.minimum(i * tm // SEQ, BATCH), 1)
    gain = g_ref[...] * (1 + scale_ref[group, :])
    shift = shift_ref[group, :]

    def chunk(r, carry):
        rows = pl.ds(pl.multiple_of(r * NORM_CHUNK, NORM_CHUNK), NORM_CHUNK)
        x = _stream_tile(h_refs, i, tm, rows)
        y = x * lax.rsqrt(jnp.mean(x * x, axis=-1, keepdims=True) + EPS)
        o_ref[rows, :] = (y * gain + shift).astype(o_ref.dtype)
        return carry

    lax.fori_loop(0, tm // NORM_CHUNK, chunk, 0, unroll=True)


def _norm_mod(h, norm_g, mod, layer, which_norm, n_rows):
    tm = NORM_ROW_TILE
    g = norm_g.reshape(DEPTH * 2, 1, D_MODEL)
    h_arrays = h if isinstance(h, tuple) else (h,)
    h_specs = (_split_stream_specs(tm, D_MODEL, lambda i: i, lambda i: 0) if isinstance(h, tuple)
               else [pl.BlockSpec((tm, D_MODEL), lambda i: (i, 0))])
    return pl.pallas_call(
        _norm_mod_kernel,
        grid=(n_rows // tm,),
        in_specs=[
            *h_specs,
            pl.BlockSpec((None, 1, D_MODEL), lambda i: (layer * 2 + which_norm, 0, 0)),
            _mod_spec(layer, 3 * which_norm),
            _mod_spec(layer, 3 * which_norm + 1),
        ],
        out_specs=pl.BlockSpec((tm, D_MODEL), lambda i: (i, 0)),
        out_shape=jax.ShapeDtypeStruct((M_ALL, D_MODEL), BF16),
        compiler_params=_params(1),
        name="norm_mod",
    )(*h_arrays, g, mod, mod)


def _final_norm_kernel(h_ref, g_ref, o_ref):
    gain = g_ref[...]

    def chunk(r, carry):
        rows = pl.ds(pl.multiple_of(r * NORM_CHUNK, NORM_CHUNK), NORM_CHUNK)
        x = h_ref[rows, :]
        o_ref[rows, :] = x * lax.rsqrt(jnp.mean(x * x, axis=-1, keepdims=True) + EPS) * gain
        return carry

    lax.fori_loop(0, h_ref.shape[0] // NORM_CHUNK, chunk, 0, unroll=True)


def _final_norm(h, final_g):
    tm = NORM_ROW_TILE
    return pl.pallas_call(
        _final_norm_kernel,
        grid=(M_LAT // tm,),
        in_specs=[
            pl.BlockSpec((tm, D_MODEL), lambda i: (i, 0)),
            pl.BlockSpec((1, D_MODEL), lambda i: (0, 0)),
        ],
        out_specs=pl.BlockSpec((tm, D_MODEL), lambda i: (i, 0)),
        out_shape=jax.ShapeDtypeStruct((M_LAT, D_MODEL), F32),
        compiler_params=_params(1),
        name="final_norm",
    )(h, final_g.reshape(1, D_MODEL))


def _mm_kernel(x_ref, w_ref, o_ref, wbf_ref, *, act):
    @pl.when(pl.program_id(1) == 0)
    def _():
        wbf_ref[...] = w_ref[...].astype(BF16)

    acc = jnp.dot(x_ref[...], wbf_ref[...], preferred_element_type=F32)
    if act == "gelu":
        acc = jax.nn.gelu(acc)
    o_ref[...] = acc.astype(o_ref.dtype)


def _mm_resid_kernel(x_ref, w_ref, gate_ref, *refs):
    *res_refs, o_ref, wbf_ref = refs
    i = pl.program_id(1)
    tm = x_ref.shape[0]

    @pl.when(i == 0)
    def _():
        wbf_ref[...] = w_ref[...].astype(BF16)

    acc = jnp.dot(x_ref[...], wbf_ref[...], preferred_element_type=F32)
    o_ref[...] = _stream_tile(res_refs, i, tm) + _rows_of_group(gate_ref, i * tm, tm) * acc


def _matmul(x, w, w_idx, *, n_rows, tn, out_dtype, act=None):
    tm = ROW_TILE[n_rows]
    _, k, n = w.shape
    return pl.pallas_call(
        functools.partial(_mm_kernel, act=act),
        grid=(n // tn, n_rows // tm),
        in_specs=[
            pl.BlockSpec((tm, k), lambda j, i: (i, 0)),
            pl.BlockSpec((None, k, tn), lambda j, i: (w_idx, 0, j)),
        ],
        out_specs=pl.BlockSpec((tm, tn), lambda j, i: (i, j)),
        out_shape=jax.ShapeDtypeStruct((x.shape[0], n), out_dtype),
        scratch_shapes=[pltpu.VMEM((k, tn), BF16)],
        compiler_params=_params(2),
        name="matmul",
    )(x, w)


def _matmul_resid(x, w, w_idx, h, mod, layer, which_gate, *, n_rows, tn):
    _, k, n = w.shape
    w_tile_bytes = k * tn * 4
    w_buffers = pl.Buffered(1) if w_tile_bytes > 12 * 2**20 else None
    if isinstance(h, tuple):
        tm = M_CTX
        res_arrays, aliases = h, {}
        res_specs = _split_stream_specs(tm, tn, lambda j, i: i, lambda j, i: j)
    else:
        tm = (ROW_TILE_WIDE_K if w_tile_bytes > 8 * 2**20 else ROW_TILE)[n_rows]
        res_arrays, aliases = (h,), {3: 0}
        res_specs = [pl.BlockSpec((tm, tn), lambda j, i: (i, j))]
    return pl.pallas_call(
        _mm_resid_kernel,
        grid=(n // tn, n_rows // tm),
        in_specs=[
            pl.BlockSpec((tm, k), lambda j, i: (i, 0)),
            pl.BlockSpec((None, k, tn), lambda j, i: (w_idx, 0, j), pipeline_mode=w_buffers),
            _mod_spec(layer, which_gate, tn, grid_rank=2),
            *res_specs,
        ],
        out_specs=pl.BlockSpec((tm, tn), lambda j, i: (i, j)),
        out_shape=jax.ShapeDtypeStruct((M_ALL, n), F32),
        scratch_shapes=[pltpu.VMEM((k, tn), BF16)],
        input_output_aliases=aliases,
        compiler_params=_params(2),
        name="matmul_resid",
    )(x, w, mod, *res_arrays)


HALO = BF16_ROWS


SEQ_STARTS = tuple(b * SEQ for b in range(BATCH)) + tuple(M_LAT + b * CTX_LEN for b in range(BATCH))
FIX_ROWS = BF16_ROWS


def _is_seq_edge(row):
    edge = row == M_ALL
    for s in SEQ_STARTS:
        edge = edge | (row == s)
    return edge


def _conv3(prev, cur, nxt, cw_ref, cb_ref):
    reps = (prev.shape[0] // SUBLANES, 1)
    tap = [jnp.tile(cw_ref[j * SUBLANES:(j + 1) * SUBLANES, :], reps) for j in range(3)]
    y = jnp.tile(cb_ref[...], reps) + prev * tap[0]
    y = y + cur * tap[1]
    return y + nxt * tap[2]


def _silu_gate(g, v):
    half_g = 0.5 * g
    return (half_g * jnp.tanh(half_g) + half_g) * v


def _ffn_up_kernel(x_ref, xp_ref, xn_ref, wg_ref, wv_ref, cwg_ref, cwv_ref, cbg_ref, cbv_ref,
                   o_ref, wg_bf, wv_bf, xs_ref, zg_ref, zv_ref):
    i = pl.program_id(1)
    tm = x_ref.shape[0]
    row0 = i * tm

    @pl.when(i == 0)
    def _():
        wg_bf[...] = wg_ref[...].astype(BF16)
        wv_bf[...] = wv_ref[...].astype(BF16)

    no_rows = jnp.zeros_like(xp_ref)
    xs_ref[0:HALO, :] = jnp.where(_is_seq_edge(row0), no_rows, xp_ref[...])
    xs_ref[HALO:HALO + tm, :] = x_ref[...]
    xs_ref[HALO + tm:, :] = jnp.where(_is_seq_edge(row0 + tm), no_rows, xn_ref[...])
    xs = xs_ref[...]
    zg_ref[...] = jnp.dot(xs, wg_bf[...], preferred_element_type=F32)
    zv_ref[...] = jnp.dot(xs, wv_bf[...], preferred_element_type=F32)

    def conv(z_ref, cw_ref, cb_ref):
        return _conv3(z_ref[HALO - 1:HALO - 1 + tm, :], z_ref[HALO:HALO + tm, :],
                      z_ref[HALO + 1:HALO + 1 + tm, :], cw_ref, cb_ref)

    o_ref[...] = _silu_gate(conv(zg_ref, cwg_ref, cbg_ref), conv(zv_ref, cwv_ref, cbv_ref)).astype(o_ref.dtype)

    n_win = 2 * FIX_ROWS + 2 * SUBLANES
    w_row = lax.broadcasted_iota(jnp.int32, (n_win, 1), 0)
    start_row = SUBLANES + FIX_ROWS

    def fixed_conv(z_ref, cw_ref, cb_ref, win):
        z = z_ref[win, :]
        prev = jnp.where(w_row == start_row, 0.0, pltpu.roll(z, 1, 0))
        nxt = jnp.where(w_row == start_row - 1, 0.0, pltpu.roll(z, n_win - 1, 0))
        return _conv3(prev, z, nxt, cw_ref, cb_ref)[SUBLANES:SUBLANES + 2 * FIX_ROWS]

    for s in SEQ_STARTS[1:]:
        @pl.when((s > row0) & (s < row0 + tm))
        def _():
            win = pl.ds(pl.multiple_of(HALO + s - row0 - start_row, SUBLANES), n_win)
            rows = pl.ds(pl.multiple_of(s - row0 - FIX_ROWS, FIX_ROWS), 2 * FIX_ROWS)
            g = fixed_conv(zg_ref, cwg_ref, cbg_ref, win)
            v = fixed_conv(zv_ref, cwv_ref, cbv_ref, win)
            o_ref[rows, :] = _silu_gate(g, v).astype(o_ref.dtype)


def _ffn_up(n, w_up, conv_w, conv_b, layer, *, n_rows):
    tm, tn = ROW_TILE[n_rows], 512
    k = w_up.shape[1]
    nj = D_FF // tn
    halo_per_tile = tm // HALO
    last_halo = M_ALL // HALO - 1
    cw = jnp.repeat(conv_w, SUBLANES, axis=1)
    cb = jnp.broadcast_to(conv_b[:, None, :], (DEPTH, SUBLANES, 2 * D_FF))
    return pl.pallas_call(
        _ffn_up_kernel,
        grid=(nj, n_rows // tm),
        in_specs=[
            pl.BlockSpec((tm, k), lambda j, i: (i, 0)),
            pl.BlockSpec((HALO, k), lambda j, i: (jnp.maximum(i * halo_per_tile - 1, 0), 0)),
            pl.BlockSpec((HALO, k), lambda j, i: (jnp.minimum((i + 1) * halo_per_tile, last_halo), 0)),
            pl.BlockSpec((None, k, tn), lambda j, i: (layer, 0, j)),
            pl.BlockSpec((None, k, tn), lambda j, i: (layer, 0, j + nj)),
            pl.BlockSpec((None, 3 * SUBLANES, tn), lambda j, i: (layer, 0, j)),
            pl.BlockSpec((None, 3 * SUBLANES, tn), lambda j, i: (layer, 0, j + nj)),
            pl.BlockSpec((None, SUBLANES, tn), lambda j, i: (layer, 0, j)),
            pl.BlockSpec((None, SUBLANES, tn), lambda j, i: (layer, 0, j + nj)),
        ],
        out_specs=pl.BlockSpec((tm, tn), lambda j, i: (i, j)),
        out_shape=jax.ShapeDtypeStruct((M_ALL, D_FF), BF16),
        scratch_shapes=[
            pltpu.VMEM((k, tn), BF16),
            pltpu.VMEM((k, tn), BF16),
            pltpu.VMEM((tm + 2 * HALO, k), BF16),
            pltpu.VMEM((tm + 2 * HALO, tn), F32),
            pltpu.VMEM((tm + 2 * HALO, tn), F32),
        ],
        compiler_params=_params(2),
        name="ffn_up_conv_gate",
    )(n, n, n, w_up, w_up, cw, cw, cb, cb)


def _sgu_kernel(u_ref, v_ref, gv_ref, ws_ref, bs_ref, o_ref):
    tm = u_ref.shape[0]
    gw = A_WIDTH // A_GROUPS
    gv = gv_ref[...]
    for c in range(tm // CHUNK):
        rows = slice(c * CHUNK, (c + 1) * CHUNK)
        v = v_ref[rows, :].astype(F32)
        vn = v * lax.rsqrt(jnp.mean(v * v, axis=-1, keepdims=True) + EPS) * gv
        vn = vn.astype(BF16)
        for g in range(A_GROUPS):
            cols = slice(g * gw, (g + 1) * gw)
            s = jnp.dot(ws_ref[g].astype(BF16), vn[:, cols], preferred_element_type=F32)
            s = s + bs_ref[:, g:g + 1]
            o_ref[rows, cols] = (u_ref[rows, cols].astype(F32) * s).astype(o_ref.dtype)


def _sgu(z, g_v, w_s, b_s, j, *, n_rows):
    tm = 4 * CHUNK
    n_a = g_v.shape[0]
    return pl.pallas_call(
        _sgu_kernel,
        grid=(n_rows // tm,),
        in_specs=[
            pl.BlockSpec((tm, A_WIDTH), lambda i: (i, 0)),
            pl.BlockSpec((tm, A_WIDTH), lambda i: (i, 1)),
            pl.BlockSpec((None, 1, A_WIDTH), lambda i: (j, 0, 0)),
            pl.BlockSpec((None, A_GROUPS, CHUNK, CHUNK), lambda i: (j, 0, 0, 0)),
            pl.BlockSpec((None, CHUNK, A_GROUPS), lambda i: (j, 0, 0)),
        ],
        out_specs=pl.BlockSpec((tm, A_WIDTH), lambda i: (i, 0)),
        out_shape=jax.ShapeDtypeStruct((M_ALL, A_WIDTH), BF16),
        compiler_params=_params(1),
        name="sgu",
    )(z, z, g_v.reshape(n_a, 1, A_WIDTH), w_s, b_s.transpose(0, 2, 1))


NA_DR = 2 * NA_KH - 1
NA_DC = 2 * NA_KW - 1
NA_PAIR_TILES = NA_DR + 1
NA_QROWS = 4
NA_KROWS = NA_KH + NA_QROWS
NA_BLOCK_UNROLL = 14


def _na_bias_pairs(rpb_ref, head, pair_ref):
    shape = (GRID_W, 2 * GRID_W)
    q = lax.broadcasted_iota(jnp.int32, shape, 0)
    lane = lax.broadcasted_iota(jnp.int32, shape, 1)
    kcol = lane & (GRID_W - 1)
    upper = lane >= GRID_W
    dc = jnp.clip(kcol - q, -(NA_KW - 1), NA_KW - 1) + NA_KW - 1
    c0 = jnp.clip(q - NA_KW // 2, 0, GRID_W - NA_KW)
    ok = (kcol >= c0) & (kcol < c0 + NA_KW)
    for d in range(-1, NA_DR):
        acc = jnp.full(shape, NEG_INF, F32)
        for c in range(NA_DC):
            lo = rpb_ref[head, d * NA_DC + c] if d >= 0 else NEG_INF
            hi = rpb_ref[head, (d + 1) * NA_DC + c] if d + 1 < NA_DR else NEG_INF
            acc = jnp.where(dc == c, jnp.where(upper, hi, lo), acc)
        pair_ref[d + 1] = jnp.where(ok, acc * LOG2_E, NEG_INF)


def _na_window_start(m):
    return min(max(m * NA_QROWS - NA_KH // 2, 0), GRID_ROWS - NA_KROWS)


def _na_block_bias(pair_ref, m):
    upper = lax.broadcasted_iota(jnp.int32, (GRID_W, 2 * GRID_W), 1) >= GRID_W
    masked = jnp.full((GRID_W, 2 * GRID_W), NEG_INF, F32)
    ws = _na_window_start(m)
    rows = []
    for a in range(NA_QROWS):
        r = m * NA_QROWS + a
        r0 = min(max(r - NA_KH // 2, 0), GRID_ROWS - NA_KH)
        tiles = []
        for p in range(NA_KROWS // 2):
            key_row = ws + 2 * p
            in_lo = r0 <= key_row < r0 + NA_KH
            in_hi = r0 <= key_row + 1 < r0 + NA_KH
            if not (in_lo or in_hi):
                tiles.append(masked)
                continue
            tile = pair_ref[key_row - r + NA_KH]
            if not in_hi:
                tile = jnp.where(upper, NEG_INF, tile)
            elif not in_lo:
                tile = jnp.where(upper, tile, NEG_INF)
            tiles.append(tile)
        rows.append(jnp.concatenate(tiles, axis=1))
    return jnp.concatenate(rows, axis=0)


def _na_lat_kernel(rpb_ref, q_ref, k_ref, v_ref, qc_ref, kc_ref, vc_ref, o_ref, oc_ref, pair_ref, bias_ref):
    scale = NA_HEAD_DIM ** -0.5 * LOG2_E

    s_cc = lax.dot_general(qc_ref[...], kc_ref[...], (((1,), (1,)), ((), ())),
                           preferred_element_type=F32) * scale
    p_cc = jnp.exp2(s_cc - jnp.max(s_cc, axis=-1, keepdims=True))
    o_cc = jnp.dot(p_cc.astype(BF16), vc_ref[...], preferred_element_type=F32)
    oc_ref[...] = (o_cc / jnp.sum(p_cc, axis=-1, keepdims=True)).astype(oc_ref.dtype)

    @pl.when(pl.program_id(1) == 0)
    def _():
        _na_bias_pairs(rpb_ref, pl.program_id(0), pair_ref)
        bias_ref[...] = _na_block_bias(pair_ref, 1)

    kc = kc_ref[...]
    vc = vc_ref[...]
    nt = (((1,), (1,)), ((), ()))
    n_q = NA_QROWS * GRID_W
    n_blocks = GRID_ROWS // NA_QROWS

    def attend(qrows, win, bias):
        q = q_ref[qrows, :]
        kw = k_ref[win, :]
        vw = v_ref[win, :]
        s =lax.dot_general(q, kw, nt, preferred_element_type=F32) * scale + bias
        sc = lax.dot_general(q, kc, nt, preferred_element_type=F32) * scale
        mx = jnp.maximum(jnp.max(s, axis=-1, keepdims=True), jnp.max(sc, axis=-1, keepdims=True))
        p_w = jnp.exp2(s - mx)
        p_c = jnp.exp2(sc - mx)
        denom = jnp.sum(p_w, axis=-1, keepdims=True) + jnp.sum(p_c, axis=-1, keepdims=True)
        o = jnp.dot(p_w.astype(BF16), vw, preferred_element_type=F32)
        o = o + jnp.dot(p_c.astype(BF16), vc, preferred_element_type=F32)
        o_ref[qrows, :] = (o / denom).astype(o_ref.dtype)

    def edge_block(m):
        ws = _na_window_start(m)
        attend(slice(m * n_q, (m + 1) * n_q), slice(ws * GRID_W, (ws + NA_KROWS) * GRID_W),
               _na_block_bias(pair_ref, m))

    def interior_block(m, carry):
        qrows = pl.ds(pl.multiple_of(m * n_q, n_q), n_q)
        ws = m * NA_QROWS - NA_KH // 2
        win = pl.ds(pl.multiple_of(ws * GRID_W, GRID_W), NA_KROWS * GRID_W)
        attend(qrows, win, bias_ref[...])
        return carry

    edge_block(0)
    lax.fori_loop(1, n_blocks - 1, interior_block, 0, unroll=NA_BLOCK_UNROLL)
    edge_block(n_blocks - 1)


def _na_attention(qkv, rpb):
    dh, nh = NA_HEAD_DIM, NA_HEADS
    ctx_blk0 = M_LAT // CTX_LEN
    lat, ctx = pl.pallas_call(
        _na_lat_kernel,
        grid=(nh, BATCH),
        in_specs=[
            pl.BlockSpec(memory_space=pltpu.SMEM),
            pl.BlockSpec((SEQ, dh), lambda h, b: (b, h)),
            pl.BlockSpec((SEQ, dh), lambda h, b: (b, nh + h)),
            pl.BlockSpec((SEQ, dh), lambda h, b: (b, 2 * nh + h)),
            pl.BlockSpec((CTX_LEN, dh), lambda h, b: (ctx_blk0 + b, h)),
            pl.BlockSpec((CTX_LEN, dh), lambda h, b: (ctx_blk0 + b, nh + h)),
            pl.BlockSpec((CTX_LEN, dh), lambda h, b: (ctx_blk0 + b, 2 * nh + h)),
        ],
        out_specs=[
            pl.BlockSpec((SEQ, dh), lambda h, b: (b, h)),
            pl.BlockSpec((CTX_LEN, dh), lambda h, b: (b, h)),
        ],
        out_shape=[
            jax.ShapeDtypeStruct((M_ALL, D_MODEL), BF16),
            jax.ShapeDtypeStruct((M_CTX, D_MODEL), BF16),
        ],
        scratch_shapes=[
            pltpu.VMEM((NA_PAIR_TILES, GRID_W, 2 * GRID_W), F32),
            pltpu.VMEM((NA_QROWS * GRID_W, NA_KROWS * GRID_W), F32),
        ],
        compiler_params=_params(2),
        name="na_latent",
    )(rpb.reshape(nh, NA_DR * NA_DC), qkv, qkv, qkv, qkv, qkv, qkv)
    return lax.dynamic_update_slice(lat, ctx, (M_LAT, 0))


RNN_LEN = CTX_LEN + SEQ
SCAN_UNROLL = 8


def _shift_rows(x, offset, length):
    if offset == 0:
        return x
    t = lax.broadcasted_iota(jnp.int32, (length, 1), 0)
    rolled = pltpu.roll(x, (-offset) % length, 0)
    return jnp.where((t + offset >= 0) & (t + offset < length), rolled, 0.0)


def _rnn_conv(x, cw, cb):
    length = x.shape[0]
    left = RNN_CONV // 2
    y = cb
    for j in range(RNN_CONV):
        y = y + _shift_rows(x, j - left, length) * cw[j:j + 1]
    return y


def _block_scan(a, b, reverse):
    row = lax.broadcasted_iota(jnp.int32, a.shape, 0)
    for s in (1, 2, 4):
        if reverse:
            keep = row < SUBLANES - s
            shift = SUBLANES - s
        else:
            keep = row >= s
            shift = s
        a_sh = pltpu.roll(a, shift, 0)
        b_sh = pltpu.roll(b, shift, 0)
        b = jnp.where(keep, a * b_sh + b, b)
        a = jnp.where(keep, a * a_sh, a)
    return a, b


def _rglru_kernel(xc_ref, xl_ref, yc_ref, yl_ref, cw_ref, cb_ref, wg_ref, bg_ref, lam_ref,
                  oc_ref, ol_ref, af_ref, bf_ref, ar_ref, br_ref, hf_ref, hr_ref):
    cw = cw_ref[...]
    cb = cb_ref[...]
    xr_c = _rnn_conv(xc_ref[...].astype(F32), cw, cb)
    xr_l = _rnn_conv(xl_ref[...].astype(F32), cw, cb)

    def half_tanh_gate(xb, d, gate):
        w = (0.5 * wg_ref[2 * d + gate]).astype(BF16)
        return jnp.tanh(jnp.dot(xb, w, preferred_element_type=F32) + 0.5 * bg_ref[2 * d + gate])

    def gates(xr, d, a_ref, b_ref, rows):
        xb = xr.astype(BF16)
        half_xr = 0.5 * xr
        t_r = half_tanh_gate(xb, d, 0)
        t_i = half_tanh_gate(xb, d, 1)
        neg_lam = -lam_ref[d:d + 1, :]
        softplus = jnp.maximum(neg_lam, 0.0) + jnp.log1p(jnp.exp(-jnp.abs(neg_lam)))
        c = (-0.5 * RG_C * LOG2_E) * softplus
        a = jnp.exp2(c * t_r + c)
        a_ref[rows, :] = a
        y = 1.0 - a * a
        root = jnp.where(y > 0.0, y * lax.rsqrt(y), 0.0)
        b_ref[rows, :] = root * (half_xr * t_i + half_xr)

    ctx_f, lat_f = slice(0, CTX_LEN), slice(CTX_LEN, RNN_LEN)
    lat_r, ctx_r = slice(0, SEQ), slice(SEQ, RNN_LEN)
    gates(xr_c, 0, af_ref, bf_ref, ctx_f)
    gates(xr_l, 0, af_ref, bf_ref, lat_f)
    gates(xr_c, 1, ar_ref, br_ref, ctx_r)
    gates(xr_l, 1, ar_ref, br_ref, lat_r)

    nblk = RNN_LEN // SUBLANES
    last_row = slice(SUBLANES - 1, SUBLANES)
    first_row = slice(0, 1)
    blk_shape = (SUBLANES, RNN_HEAD_DIM)

    def block(n, carry):
        cf, cr = carry
        rows_f = pl.ds(pl.multiple_of(n * SUBLANES, SUBLANES), SUBLANES)
        rows_r = pl.ds(pl.multiple_of((nblk - 1 - n) * SUBLANES, SUBLANES), SUBLANES)
        a_f, b_f = _block_scan(af_ref[rows_f, :], bf_ref[rows_f, :], reverse=False)
        a_r, b_r = _block_scan(ar_ref[rows_r, :], br_ref[rows_r, :], reverse=True)
        hf_ref[rows_f, :] = a_f * cf + b_f
        hr_ref[rows_r, :] = a_r * cr + b_r
        cf = jnp.broadcast_to(a_f[last_row], blk_shape) * cf + jnp.broadcast_to(b_f[last_row], blk_shape)
        cr = jnp.broadcast_to(a_r[first_row], blk_shape) * cr + jnp.broadcast_to(b_r[first_row], blk_shape)
        return cf, cr

    zero = jnp.zeros(blk_shape, F32)
    lax.fori_loop(0, nblk, block, (zero, zero), unroll=SCAN_UNROLL)

    h_c = hf_ref[ctx_f, :] + hr_ref[ctx_r, :]
    h_l = hf_ref[lat_f, :] + hr_ref[lat_r, :]
    oc_ref[...] = (jax.nn.gelu(yc_ref[...].astype(F32)) * h_c).astype(oc_ref.dtype)
    ol_ref[...] = (jax.nn.gelu(yl_ref[...].astype(F32)) * h_l).astype(ol_ref.dtype)


def _rglru(xin, conv_w, conv_b, w_gate, b_gate, lam, j):
    dh, nh = RNN_HEAD_DIM, RNN_HEADS
    n_c = conv_w.shape[0]
    ctx_blk0 = M_LAT // CTX_LEN
    wg = w_gate.reshape(n_c, 4, nh, dh, dh)
    bg = b_gate.reshape(n_c, 4, nh, 1, dh)
    out_c, out_l = pl.pallas_call(
        _rglru_kernel,
        grid=(BATCH, nh),
        in_specs=[
            pl.BlockSpec((CTX_LEN, dh), lambda b, h: (ctx_blk0 + b, nh + h)),
            pl.BlockSpec((SEQ, dh), lambda b, h: (b, nh + h)),
            pl.BlockSpec((CTX_LEN, dh), lambda b, h: (ctx_blk0 + b, h)),
            pl.BlockSpec((SEQ, dh), lambda b, h: (b, h)),
            pl.BlockSpec((None, RNN_CONV, dh), lambda b, h: (j, 0, h)),
            pl.BlockSpec((None, 1, dh), lambda b, h: (j, 0, h)),
            pl.BlockSpec((None, 4, None, dh, dh), lambda b, h: (j, 0, h, 0, 0)),
            pl.BlockSpec((None, 4, None, 1, dh), lambda b, h: (j, 0, h, 0, 0)),
            pl.BlockSpec((None, 2, dh), lambda b, h: (j, 0, h)),
        ],
        out_specs=[
            pl.BlockSpec((CTX_LEN, dh), lambda b, h: (b, h)),
            pl.BlockSpec((SEQ, dh), lambda b, h: (b, h)),
        ],
        out_shape=[
            jax.ShapeDtypeStruct((M_CTX, RNN_WIDTH), BF16),
            jax.ShapeDtypeStruct((M_ALL, RNN_WIDTH), BF16),
        ],
        scratch_shapes=[pltpu.VMEM((RNN_LEN, dh), F32)] * 6,
        compiler_params=_params(2),
        name="rglru",
    )(xin, xin, xin, xin, conv_w, conv_b.reshape(n_c, 1, RNN_WIDTH), wg, bg, lam)
    return lax.dynamic_update_slice(out_l, out_c, (M_LAT, 0))


def kernel(x, c, ctx, c_ctx, ada_w, ada_b, norm_g, ffn_w_up, ffn_conv_w, ffn_conv_b, ffn_w_down,
           a_w_in, a_g_v, a_w_s, a_b_s, a_w_out, b_w_qkv, b_rpb, b_w_out,
           c_w_in, c_conv_w, c_conv_b, c_w_gate, c_b_gate, c_lam, c_w_out, final_g):
    h = (x.reshape(M_LAT, D_MODEL), ctx.reshape(M_CTX, D_MODEL))
    cvec = jnp.concatenate(
        [c, c_ctx[None], jnp.zeros((SUBLANES - BATCH - 1, D_MODEL), F32)], axis=0)
    mod = _ada_mod(cvec, ada_w, ada_b)

    for i in range(DEPTH):
        last = i == DEPTH - 1
        kind, j = i % N_MIXERS, i // N_MIXERS
        ffn_rows = M_LAT if last else M_ALL
        n_rows = M_LAT if (last and kind == 0) else M_ALL

        n = _norm_mod(h, norm_g, mod, i, 0, n_rows)
        if kind == 0:
            z = _matmul(n, a_w_in, j, n_rows=n_rows, tn=1024, out_dtype=BF16, act="gelu")
            y = _sgu(z, a_g_v, a_w_s, a_b_s, j, n_rows=n_rows)
            w_out = a_w_out
        elif kind == 1:
            qkv = _matmul(n, b_w_qkv, j, n_rows=n_rows, tn=1024, out_dtype=BF16)
            y = _na_attention(qkv, b_rpb[j])
            w_out = b_w_out
        else:
            xin = _matmul(n, c_w_in, j, n_rows=n_rows, tn=1024, out_dtype=F32)
            y = _rglru(xin, c_conv_w, c_conv_b, c_w_gate, c_b_gate, c_lam, j)
            w_out = c_w_out
        h = _matmul_resid(y, w_out, j, h, mod, i, 2, n_rows=n_rows, tn=1024)

        n = _norm_mod(h, norm_g, mod, i, 1, ffn_rows)
        a = _ffn_up(n, ffn_w_up, ffn_conv_w, ffn_conv_b, i, n_rows=ffn_rows)
        h = _matmul_resid(a, ffn_w_down, i, h, mod, i, 5, n_rows=ffn_rows, tn=512)

    return _final_norm(h, final_g).reshape(BATCH, SEQ, D_MODEL)
```
